```python
import math
import jax, jax.numpy as jnp
from jax import lax
import numpy as np

D_MODEL = 1024
BATCH = 32
SEQ = 256
DEPTH = 4
DEC_BATCH = 4
DEC_SEQ = 2048
PAST_LEN = 256

GRID_W = 64
N_MIXERS = 3
N_LAYERS_A = (DEPTH + 2) // 3
N_LAYERS_B = (DEPTH + 1) // 3
N_LAYERS_C = DEPTH // 3
N_DENSE = (DEPTH + 1) // 2
N_MOE = DEPTH // 2

CHUNK = 128
GMLP_GROUPS = 8
GMLP_GC = D_MODEL // GMLP_GROUPS

D_RNN = 1280
RNN_BLOCKS = 10
RNN_BS = D_RNN // RNN_BLOCKS
CONV_W = 4
CONV_PAD_L = (CONV_W - 1) // 2
CONV_PAD_R = CONV_W - 1 - CONV_PAD_L
RGLRU_C = 8.0

DIFF_HD = 64
DIFF_VD = 2 * DIFF_HD
DIFF_HEADS = D_MODEL // DIFF_VD
ROPE_THETA = 10000.0
Q_BLOCK = 128

D_FF = 2816
N_EXPERTS = 8
TOP_K = 2
D_FF_EXPERT = 2816

kernel_name = "hybrid_diffusion_gmlp_rglru_diffattn_step"


def _rmsnorm(x, g, eps=1e-6):
    xf = x.astype(jnp.float32)
    y = xf * lax.rsqrt(jnp.mean(xf * xf, axis=-1, keepdims=True) + eps)
    return (y * g.astype(jnp.float32)).astype(x.dtype)


def _layernorm(x, g, eps=1e-5):
    xf = x.astype(jnp.float32)
    mu = jnp.mean(xf, axis=-1, keepdims=True)
    xc = xf - mu
    y = xc * lax.rsqrt(jnp.mean(xc * xc, axis=-1, keepdims=True) + eps)
    return (y * g.astype(jnp.float32)).astype(x.dtype)


def _adaln(cond, w, b):
    m = (jax.nn.silu(cond) @ w + b).reshape(cond.shape[0], 1, 6, D_MODEL)
    return tuple(m[:, :, j] for j in range(6))


def _modulate(h, shift, scale):
    return h * (1.0 + scale) + shift


def _chunk_gmlp(h, w_in, ln_g, w_s, b_s, w_out):
    B, T, _ = h.shape
    u, v = jnp.split(jax.nn.gelu(h @ w_in), 2, axis=-1)
    v = _layernorm(v, ln_g).reshape(B, T // CHUNK, CHUNK, GMLP_GROUPS, GMLP_GC)
    s = jnp.einsum("bnpgc,gqp->bnqgc", v, w_s) + b_s.T[:, :, None]
    return (u * s.reshape(B, T, D_MODEL)) @ w_out


def _centred_dwconv(x, w, b):
    T = x.shape[1]
    xp = jnp.pad(x, ((0, 0), (CONV_PAD_L, CONV_PAD_R), (0, 0)))
    y = b
    for k in range(CONV_W):
        y = y + xp[:, k:k + T] * w[k]
    return y


def _linear_scan(a, b, h0):
    b = b.at[:, 0].add(a[:, 0] * h0)

    def comb(left, right):
        al, bl = left
        ar, br = right
        return al * ar, ar * bl + br

    _, h = lax.associative_scan(comb, (a, b), axis=1)
    return h


def _rglru_dir(xc, w_gate, b_gate, lam, h0):
    B, T, _ = xc.shape
    xb = xc.reshape(B, T, RNN_BLOCKS, RNN_BS)
    g = jnp.einsum("btnk,gnkj->gbtnj", xb, w_gate.astype(jnp.float32))
    g = g.reshape(2, B, T, D_RNN) + b_gate.astype(jnp.float32)[:, None, None, :]
    r = jax.nn.sigmoid(g[0])
    i = jax.nn.sigmoid(g[1])
    log_a = -RGLRU_C * r * jax.nn.softplus(-lam.astype(jnp.float32))
    a = jnp.exp(log_a)
    mult = jnp.sqrt(-jnp.expm1(2.0 * log_a))
    return _linear_scan(a, mult * (i * xc), h0.astype(jnp.float32))


def _rglru_mixer(h, w_in, conv_w, conv_b, w_gate, b_gate, lam, w_out, h0_f, h0_b):
    yb, xb = jnp.split(h @ w_in, 2, axis=-1)
    yb = jax.nn.gelu(yb)
    xc = _centred_dwconv(xb, conv_w, conv_b).astype(jnp.float32)
    hf = _rglru_dir(xc, w_gate[0], b_gate[0], lam[0], h0_f)
    hb_rev = _rglru_dir(jnp.flip(xc, axis=1), w_gate[1], b_gate[1], lam[1], h0_b)
    hsum = hf + jnp.flip(hb_rev, axis=1)
    out = (hsum.astype(h.dtype) * yb) @ w_out
    final = jnp.stack([hf[:, -1], hb_rev[:, -1]], axis=1).astype(h.dtype)
    return out, final


def _rope_2d(x):
    T = x.shape[1]
    rows_n = T // GRID_W
    row = jnp.repeat(jnp.arange(rows_n), GRID_W)
    col = jnp.tile(jnp.arange(GRID_W), rows_n)
    half = DIFF_HD // 2
    freqs = ROPE_THETA ** (-jnp.arange(0, half, 2, dtype=jnp.float32) / half)

    def rot(xa, pos):
        ang = pos.astype(jnp.float32)[:, None] * freqs[None]
        cos = jnp.cos(ang)[None, :, None, None, :]
        sin = jnp.sin(ang)[None, :, None, None, :]
        x1, x2 = xa[..., :half // 2], xa[..., half // 2:]
        return jnp.concatenate([x1 * cos - x2 * sin, x2 * cos + x1 * sin], axis=-1)

    xf = x.astype(jnp.float32)
    out = jnp.concatenate([rot(xf[..., :half], row), rot(xf[..., half:], col)], axis=-1)
    return out.astype(x.dtype)


def _diff_attention_blocks(q, k_all, v_all, lam):
    B, T = q.shape[:2]
    nb = T // Q_BLOCK
    qb = q.reshape(B, nb, Q_BLOCK, DIFF_HEADS, 2, DIFF_HD).transpose(1, 0, 2, 3, 4, 5)
    scale = DIFF_HD ** -0.5

    def one_block(qblk):
        s = jnp.einsum("bqhcd,bkhcd->bhcqk", qblk, k_all).astype(jnp.float32) * scale
        p = jax.nn.softmax(s, axis=-1)
        w = p[:, :, 0] - lam * p[:, :, 1]
        return jnp.einsum("bhqk,bkhe->bqhe", w.astype(v_all.dtype), v_all)

    o = lax.map(one_block, qb)
    return o.transpose(1, 0, 2, 3, 4).reshape(B, T, DIFF_HEADS, DIFF_VD)


def _diff_attn_mixer(h, w_qkv, lam_p, subln_g, w_out, lambda_init, ctx_k, ctx_v):
    B, T, _ = h.shape
    q, k, v = jnp.split(h @ w_qkv, 3, axis=-1)
    q = q.reshape(B, T, DIFF_HEADS, 2, DIFF_HD)
    k = k.reshape(B, T, DIFF_HEADS, 2, DIFF_HD)
    v = v.reshape(B, T, DIFF_HEADS, DIFF_VD)
    if ctx_k is None:
        k_all, v_all = k, v
    else:
        q = _rope_2d(q)
        k = _rope_2d(k)
        k_all = jnp.concatenate([k, ctx_k.reshape(B, -1, DIFF_HEADS, 2, DIFF_HD)], axis=1)
        v_all = jnp.concatenate([v, ctx_v], axis=1)
    lp = lam_p.astype(jnp.float32)
    lam = jnp.exp(jnp.sum(lp[0] * lp[1])) - jnp.exp(jnp.sum(lp[2] * lp[3])) + lambda_init
    o = _diff_attention_blocks(q, k_all, v_all, lam)
    o = _rmsnorm(o, subln_g) * (1.0 - lambda_init)
    out = o.reshape(B, T, D_MODEL) @ w_out
    return out, k.reshape(B, T, DIFF_HEADS, 2 * DIFF_HD), v


def _swiglu(x, w_gu, w_down):
    g, u = jnp.split(x @ w_gu, 2, axis=-1)
    return (jax.nn.silu(g) * u) @ w_down


def _moe(x, router, w_gu, w_down):
    B, T, D = x.shape
    xt = x.reshape(B * T, D)
    logits = (xt @ router).astype(jnp.float32)
    topv, topi = lax.top_k(logits, TOP_K)
    w = jax.nn.softmax(topv, axis=-1)
    gates = jnp.sum(jax.nn.one_hot(topi, N_EXPERTS, dtype=jnp.float32) * w[..., None], axis=1)
    gates = gates.astype(x.dtype)
    out = gates[:, 0:1] * _swiglu(xt, w_gu[0], w_down[0])
    for e in range(1, N_EXPERTS):
        out = out + gates[:, e:e + 1] * _swiglu(xt, w_gu[e], w_down[e])
    return out.reshape(B, T, D)


def setup_inputs(seed: int = 0) -> dict:
    key = jax.random.key(seed)
    keys = list(jax.random.split(key, 40))

    def nrm(shape, scale=1.0):
        return jax.random.normal(keys.pop(), shape, jnp.float32) * scale

    D = D_MODEL
    u = jax.random.uniform(keys.pop(), (N_LAYERS_B, 2, D_RNN), jnp.float32, 0.9, 0.999)
    p = jnp.exp(jnp.log(u) / RGLRU_C)
    rg_lambda = jnp.log(p) - jnp.log1p(-p)
    return {
        "x_prompt": nrm((BATCH, SEQ, D)),
        "x_sample": nrm((DEC_BATCH, DEC_SEQ, D)),
        "state_rglru": nrm((DEC_BATCH, N_LAYERS_B, 2, D_RNN), 0.5),
        "cache_k": nrm((DEC_BATCH, N_LAYERS_C, PAST_LEN, DIFF_HEADS, 2 * DIFF_HD)),
        "cache_v": nrm((DEC_BATCH, N_LAYERS_C, PAST_LEN, DIFF_HEADS, DIFF_VD)),
        "c": nrm((DEC_BATCH, D)),
        "c_ctx": nrm((D,)),
        "norm_g": 1.0 + nrm((DEPTH, 2, D), 0.1),
        "ada_w": nrm((DEPTH, D, 6 * D), 0.5 * D ** -0.5),
        "ada_b": nrm((DEPTH, 6 * D), 0.02),
        "gm_w_in": nrm((N_LAYERS_A, D, 2 * D), D ** -0.5),
        "gm_ln_g": 1.0 + nrm((N_LAYERS_A, D), 0.1),
        "gm_w_s": nrm((N_LAYERS_A, GMLP_GROUPS, CHUNK, CHUNK), CHUNK ** -0.5),
        "gm_b_s": 1.0 + nrm((N_LAYERS_A, GMLP_GROUPS, CHUNK), 0.1),
        "gm_w_out": nrm((N_LAYERS_A, D, D), D ** -0.5),
        "rg_w_in": nrm((N_LAYERS_B, D, 2 * D_RNN), D ** -0.5),
        "rg_conv_w": nrm((N_LAYERS_B, CONV_W, D_RNN), CONV_W ** -0.5),
        "rg_conv_b": nrm((N_LAYERS_B, D_RNN), 0.02),
        "rg_w_gate": nrm((N_LAYERS_B, 2, 2, RNN_BLOCKS, RNN_BS, RNN_BS), RNN_BS ** -0.5),
        "rg_b_gate": nrm((N_LAYERS_B, 2, 2, D_RNN), 0.02),
        "rg_lambda": rg_lambda,
        "rg_w_out": nrm((N_LAYERS_B, D_RNN, D), D_RNN ** -0.5),
        "att_w_qkv": nrm((N_LAYERS_C, D, 3 * D), D ** -0.5),
        "att_lambda": nrm((N_LAYERS_C, 4, DIFF_HD), 0.1),
        "att_subln_g": 1.0 + nrm((N_LAYERS_C, DIFF_VD), 0.1),
        "att_w_out": nrm((N_LAYERS_C, D, D), D ** -0.5),
        "ff_w_gu": nrm((N_DENSE, D, 2 * D_FF), D ** -0.5),
        "ff_w_down": nrm((N_DENSE, D_FF, D), D_FF ** -0.5),
        "moe_router": nrm((N_MOE, D, N_EXPERTS), D ** -0.5),
        "moe_w_gu": nrm((N_MOE, N_EXPERTS, D, 2 * D_FF_EXPERT), D ** -0.5),
        "moe_w_down": nrm((N_MOE, N_EXPERTS, D_FF_EXPERT, D), D_FF_EXPERT ** -0.5),
        "final_g": 1.0 + nrm((D,), 0.1),
    }


def reference(x_prompt, x_sample, state_rglru, cache_k, cache_v, c, c_ctx,
              norm_g, ada_w, ada_b,
              gm_w_in, gm_ln_g, gm_w_s, gm_b_s, gm_w_out,
              rg_w_in, rg_conv_w, rg_conv_b, rg_w_gate, rg_b_gate, rg_lambda, rg_w_out,
              att_w_qkv, att_lambda, att_subln_g, att_w_out,
              ff_w_gu, ff_w_down, moe_router, moe_w_gu, moe_w_down, final_g):
    yp, yl = x_prompt, x_sample
    cond_ctx = c_ctx[None]
    new_rnn, new_k, new_v = [], [], []
    for i in range(DEPTH):
        sh1p, sc1p, g1p, sh2p, sc2p, g2p = _adaln(cond_ctx, ada_w[i], ada_b[i])
        sh1l, sc1l, g1l, sh2l, sc2l, g2l = _adaln(c, ada_w[i], ada_b[i])
        hp = _modulate(_rmsnorm(yp, norm_g[i, 0]), sh1p, sc1p)
        hl = _modulate(_rmsnorm(yl, norm_g[i, 0]), sh1l, sc1l)
        kind, slot = i % N_MIXERS, i // N_MIXERS
        if kind == 0:
            wa = (gm_w_in[slot], gm_ln_g[slot], gm_w_s[slot], gm_b_s[slot], gm_w_out[slot])
            op = _chunk_gmlp(hp, *wa)
            ol = _chunk_gmlp(hl, *wa)
        elif kind == 1:
            wb = (rg_w_in[slot], rg_conv_w[slot], rg_conv_b[slot], rg_w_gate[slot],
                  rg_b_gate[slot], rg_lambda[slot], rg_w_out[slot])
            h0 = jnp.zeros((yp.shape[0], D_RNN), jnp.float32)
            op, st = _rglru_mixer(hp, *wb, h0, h0)
            ol, _ = _rglru_mixer(hl, *wb, state_rglru[:, slot, 0], state_rglru[:, slot, 1])
            new_rnn.append(st)
        else:
            lambda_init = 0.8 - 0.6 * math.exp(-0.3 * i)
            wc = (att_w_qkv[slot], att_lambda[slot], att_subln_g[slot], att_w_out[slot], lambda_init)
            op, kc, vc = _diff_attn_mixer(hp, *wc, None, None)
            ol, _, _ = _diff_attn_mixer(hl, *wc, cache_k[:, slot], cache_v[:, slot])
            new_k.append(kc)
            new_v.append(vc)
        yp = yp + g1p * op
        yl = yl + g1l * ol
        hp = _modulate(_rmsnorm(yp, norm_g[i, 1]), sh2p, sc2p)
        hl = _modulate(_rmsnorm(yl, norm_g[i, 1]), sh2l, sc2l)
        fslot = i // 2
        if i % 2 == 0:
            fp = _swiglu(hp, ff_w_gu[fslot], ff_w_down[fslot])
            fl = _swiglu(hl, ff_w_gu[fslot], ff_w_down[fslot])
        else:
            fp = _moe(hp, moe_router[fslot], moe_w_gu[fslot], moe_w_down[fslot])
            fl = _moe(hl, moe_router[fslot], moe_w_gu[fslot], moe_w_down[fslot])
        yp = yp + g2p * fp
        yl = yl + g2l * fl
    y_prompt = _rmsnorm(yp, final_g)
    y_sample = _rmsnorm(yl, final_g)
    new_state_rglru = jnp.stack(new_rnn, axis=1)
    new_cache_k = jnp.stack(new_k, axis=1)
    new_cache_v = jnp.stack(new_v, axis=1)
    return (y_prompt, y_sample, new_state_rglru, new_cache_k, new_cache_v)
```

```python
import functools
import math

import jax
import jax.numpy as jnp
from jax import lax
from jax.experimental import pallas as pl
from jax.experimental.pallas import tpu as pltpu

F32 = jnp.float32
BF16 = jnp.bfloat16
I32 = jnp.int32

D = 1024
DEPTH = 4
B_P, T_P = 32, 256
B_L, T_L = 4, 2048
PAST = 256
N_P = B_P * T_P
N_L = B_L * T_L
N_TOK = N_P + N_L
N_COND = 8
GRID_W = 64
CHUNK = 128
GROUPS = 8
D_RNN = 1280
RNN_BS = 128
CONV_W = 4
RGLRU_C = 8.0
HD = 64
VD = 128
HEADS = 8
ROPE_THETA = 10000.0
D_FF = 2816
N_EXP = 8

LANE = 128
SUBLANE = 8
VMEM_LIMIT = 56 * 1024 * 1024
TM = 512
TM_E = 256
MAX_TILES = 2 * N_TOK // TM_E + N_EXP
TM_DMA = 256
SLAB = D // LANE


def _cparams(sem):
    return pltpu.CompilerParams(dimension_semantics=sem, vmem_limit_bytes=VMEM_LIMIT)


def _cond_of_tile(i, tm):
    r0 = i * tm
    return jnp.where(r0 < N_P, 0, 1 + (r0 - N_P) // T_L)


def _modnorm(x, g, shift, scale):
    ms = jnp.mean(x * x, axis=-1, keepdims=True)
    h = x * lax.rsqrt(ms + 1e-6) * g
    return h * (1.0 + scale) + shift


def _adaln_kernel(c_ref, w_ref, b_ref, o_ref):
    s = jax.nn.silu(c_ref[...]).astype(BF16)
    o_ref[...] = jnp.dot(s, w_ref[...].astype(BF16), preferred_element_type=F32) + b_ref[...]


def _adaln(conds, ada_w, ada_b):
    tn = 1536
    n6 = 6 * D
    out = pl.pallas_call(
        _adaln_kernel,
        grid=(DEPTH, n6 // tn),
        in_specs=[
            pl.BlockSpec((N_COND, D), lambda l, j: (0, 0)),
            pl.BlockSpec((None, D, tn), lambda l, j: (l, 0, j)),
            pl.BlockSpec((None, 1, tn), lambda l, j: (l, 0, j)),
        ],
        out_specs=pl.BlockSpec((None, N_COND, tn), lambda l, j: (l, 0, j)),
        out_shape=jax.ShapeDtypeStruct((DEPTH, N_COND, n6), F32),
        compiler_params=_cparams(("arbitrary", "arbitrary")),
        name="adaln",
    )(conds, ada_w, ada_b.reshape(DEPTH, 1, n6))
    return out.reshape(DEPTH, N_COND, 6, D)


def _mod_matmul_kernel(x_ref, g_ref, mod_ref, w_ref, o_ref, *, tn, gelu_cols):
    h = _modnorm(x_ref[...], g_ref[...], mod_ref[0:1, :], mod_ref[1:2, :]).astype(BF16)
    for c in range(o_ref.shape[1] // tn):
        acc = jnp.dot(h, w_ref[:, c * tn:(c + 1) * tn], preferred_element_type=F32)
        if c * tn < gelu_cols:
            acc = jax.nn.gelu(acc)
        o_ref[:, c * tn:(c + 1) * tn] = acc


def _mod_matmul(y, g, mod, w_bf16, *, tn, gelu_cols):
    nout = w_bf16.shape[1]
    return pl.pallas_call(
        functools.partial(_mod_matmul_kernel, tn=tn, gelu_cols=gelu_cols),
        grid=(N_TOK // TM,),
        in_specs=[
            pl.BlockSpec((TM, D), lambda i: (i, 0)),
            pl.BlockSpec((1, D), lambda i: (0, 0)),
            pl.BlockSpec((None, 6, D), lambda i: (_cond_of_tile(i, TM), 0, 0)),
            pl.BlockSpec((D, nout), lambda i: (0, 0)),
        ],
        out_specs=pl.BlockSpec((TM, nout), lambda i: (i, 0)),
        out_shape=jax.ShapeDtypeStruct((N_TOK, nout), F32),
        compiler_params=_cparams(("arbitrary",)),
        name="mod_matmul",
    )(y, g.reshape(1, D), mod, w_bf16)


def _matmul_res_kernel(a_ref, w_ref, y_ref, mod_ref, o_ref):
    acc = jnp.dot(a_ref[...], w_ref[...], preferred_element_type=F32)
    o_ref[...] = y_ref[...] + mod_ref[2:3, :] * acc


def _matmul_res(a_bf16, w_bf16, y, mod):
    k = a_bf16.shape[1]
    return pl.pallas_call(
        _matmul_res_kernel,
        grid=(N_TOK // TM,),
        in_specs=[
            pl.BlockSpec((TM, k), lambda i: (i, 0)),
            pl.BlockSpec((k, D), lambda i: (0, 0)),
            pl.BlockSpec((TM, D), lambda i: (i, 0)),
            pl.BlockSpec((None, 6, D), lambda i: (_cond_of_tile(i, TM), 0, 0)),
        ],
        out_specs=pl.BlockSpec((TM, D), lambda i: (i, 0)),
        out_shape=jax.ShapeDtypeStruct((N_TOK, D), F32),
        compiler_params=_cparams(("arbitrary",)),
        name="matmul_res",
    )(a_bf16, w_bf16, y, mod)


def _gmlp_kernel(uv_ref, lng_ref, ws_ref, bs_ref, o_ref):
    rows = uv_ref.shape[0]
    v = uv_ref[:, D:]
    vc = v - jnp.mean(v, axis=-1, keepdims=True)
    vn = vc * lax.rsqrt(jnp.mean(vc * vc, axis=-1, keepdims=True) + 1e-5) * lng_ref[...]
    vn = vn.astype(BF16)
    for ci in range(rows // CHUNK):
        r0 = ci * CHUNK
        for g in range(GROUPS):
            c0 = g * LANE
            s = jnp.dot(ws_ref[g], vn[r0:r0 + CHUNK, c0:c0 + LANE], preferred_element_type=F32)
            s = s + bs_ref[:, g:g + 1]
            o_ref[r0:r0 + CHUNK, c0:c0 + LANE] = (uv_ref[r0:r0 + CHUNK, c0:c0 + LANE] * s).astype(BF16)


def _gmlp_spatial(uv, ln_g, w_s_bf16, b_s_t):
    rows = 2 * CHUNK
    return pl.pallas_call(
        _gmlp_kernel,
        grid=(N_TOK // rows,),
        in_specs=[
            pl.BlockSpec((rows, 2 * D), lambda i: (i, 0)),
            pl.BlockSpec((1, D), lambda i: (0, 0)),
            pl.BlockSpec((GROUPS, CHUNK, CHUNK), lambda i: (0, 0, 0)),
            pl.BlockSpec((CHUNK, GROUPS), lambda i: (0, 0)),
        ],
        out_specs=pl.BlockSpec((rows, D), lambda i: (i, 0)),
        out_shape=jax.ShapeDtypeStruct((N_TOK, D), BF16),
        compiler_params=_cparams(("arbitrary",)),
        name="gmlp_spatial",
    )(uv, ln_g.reshape(1, D), w_s_bf16, b_s_t)


def _rglru_kernel(yb_ref, xb_ref, cw_ref, cb_ref, wg_ref, bg_ref, lam_ref, h0_ref,
                  o_ref, fin_ref, af_ref, bf_ref, ab_ref, bb_ref, *, n_seq, t_len, row_chunk):
    stride = t_len + SUBLANE
    tix = lax.broadcasted_iota(I32, (t_len, 1), 0)
    sp_f = jax.nn.softplus(-lam_ref[0:1, :])
    sp_b = jax.nn.softplus(-lam_ref[1:2, :])
    for g in range(n_seq):
        x = xb_ref[g * t_len:(g + 1) * t_len, :]
        xm1 = jnp.where(tix >= 1, pltpu.roll(x, 1, 0), 0.0)
        xp1 = jnp.where(tix <= t_len - 2, pltpu.roll(x, t_len - 1, 0), 0.0)
        xp2 = jnp.where(tix <= t_len - 3, pltpu.roll(x, t_len - 2, 0), 0.0)
        xc = cb_ref[...] + xm1 * cw_ref[0:1, :]
        xc = xc + x * cw_ref[1:2, :]
        xc = xc + xp1 * cw_ref[2:3, :]
        xc = xc + xp2 * cw_ref[3:4, :]
        for r0 in range(0, t_len, row_chunk):
            xcc = xc[r0:r0 + row_chunk]
            gates = jnp.dot(xcc.astype(BF16), wg_ref[...], preferred_element_type=F32) + bg_ref[...]
            d0 = g * stride + r0
            for d, (sp, a_ref, b_ref) in enumerate(((sp_f, af_ref, bf_ref), (sp_b, ab_ref, bb_ref))):
                r = jax.nn.sigmoid(gates[:, (2 * d) * LANE:(2 * d + 1) * LANE])
                i = jax.nn.sigmoid(gates[:, (2 * d + 1) * LANE:(2 * d + 2) * LANE])
                log_a = (-RGLRU_C * r) * sp
                a = jnp.exp(log_a)
                one_minus_a2 = jnp.tanh(-log_a) * (a * a + 1.0)
                a_ref[d0:d0 + row_chunk, :] = a
                b_ref[d0:d0 + row_chunk, :] = jnp.sqrt(one_minus_a2) * (i * xcc)

    def step(s, carry):
        hf, hb = carry
        tf = pl.ds(s, n_seq, stride=stride)
        tb = pl.ds(t_len - 1 - s, n_seq, stride=stride)
        hf = af_ref[tf, :] * hf + bf_ref[tf, :]
        hb = ab_ref[tb, :] * hb + bb_ref[tb, :]
        bf_ref[tf, :] = hf
        bb_ref[tb, :] = hb
        return hf, hb

    hf, hb = lax.fori_loop(0, t_len, step, (h0_ref[0], h0_ref[1]))
    fin_ref[0] = hf
    fin_ref[1] = hb
    for g in range(n_seq):
        rs = slice(g * stride, g * stride + t_len)
        hsum = bf_ref[rs, :] + bb_ref[rs, :]
        o_ref[g * t_len:(g + 1) * t_len, :] = (hsum * yb_ref[g * t_len:(g + 1) * t_len, :]).astype(BF16)


def _rglru(yx, cw, cb, wg_cat, bg_cat, lam, h0, *, row0, n_batch, n_seq, t_len):
    rows = n_seq * t_len
    nb = D_RNN // RNN_BS
    rb0 = row0 // rows
    kern = functools.partial(_rglru_kernel, n_seq=n_seq, t_len=t_len, row_chunk=min(t_len, 512))
    scan_buf = pltpu.VMEM((n_seq * (t_len + SUBLANE), RNN_BS), F32)
    return pl.pallas_call(
        kern,
        grid=(n_batch // n_seq, nb),
        in_specs=[
            pl.BlockSpec((rows, RNN_BS), lambda s, c: (rb0 + s, c)),
            pl.BlockSpec((rows, RNN_BS), lambda s, c: (rb0 + s, nb + c)),
            pl.BlockSpec((CONV_W, RNN_BS), lambda s, c: (0, c)),
            pl.BlockSpec((1, RNN_BS), lambda s, c: (0, c)),
            pl.BlockSpec((None, RNN_BS, 4 * RNN_BS), lambda s, c: (c, 0, 0)),
            pl.BlockSpec((None, 1, 4 * RNN_BS), lambda s, c: (c, 0, 0)),
            pl.BlockSpec((2, RNN_BS), lambda s, c: (0, c)),
            pl.BlockSpec((2, n_seq, RNN_BS), lambda s, c: (0, s, c)),
        ],
        out_specs=[
            pl.BlockSpec((rows, RNN_BS), lambda s, c: (s, c)),
            pl.BlockSpec((2, n_seq, RNN_BS), lambda s, c: (0, s, c)),
        ],
        out_shape=[
            jax.ShapeDtypeStruct((n_batch * t_len, D_RNN), BF16),
            jax.ShapeDtypeStruct((2, n_batch, D_RNN), F32),
        ],
        scratch_shapes=[scan_buf, scan_buf, scan_buf, scan_buf],
        compiler_params=_cparams(("arbitrary", "arbitrary")),
        name="rglru",
    )(yx, yx, cw, cb.reshape(1, D_RNN), wg_cat, bg_cat, lam, h0)


def _rope(x, cos, sin_signed, first_of_pair):
    partner = jnp.where(first_of_pair, pltpu.roll(x, LANE - 16, 1), pltpu.roll(x, 16, 1))
    return x * cos + partner * sin_signed


def _attn_kernel(*refs, heads, t_len, s_len, rope, lambda_init):
    if rope:
        (lam_ref, g_ref, q_ref, k_ref, v_ref, ck_ref, cv_ref, cq_ref, sq_ref, ckk_ref, skk_ref,
         o_ref, kk_ref, vv_ref) = refs
    else:
        lam_ref, g_ref, q_ref, k_ref, v_ref, o_ref, kk_ref, vv_ref = refs
    tq = q_ref.shape[0]
    lane = lax.broadcasted_iota(I32, (1, LANE), 1)
    first_of_pair = (lane & 16) == 0
    map0 = lane < HD

    @pl.when(pl.program_id(2) == 0)
    def _():
        for j in range(heads):
            cs = slice(j * LANE, (j + 1) * LANE)
            k = k_ref[:, cs]
            if rope:
                k = _rope(k, ckk_ref[...], skk_ref[...], first_of_pair)
            kk_ref[j, 0:t_len, :] = k.astype(BF16)
            vv_ref[j, 0:t_len, :] = v_ref[:, cs].astype(BF16)
            if s_len > t_len:
                kk_ref[j, t_len:s_len, :] = ck_ref[:, cs].astype(BF16)
                vv_ref[j, t_len:s_len, :] = cv_ref[:, cs].astype(BF16)

    lp = lam_ref[...]
    lam = (jnp.exp(jnp.sum(lp[0:1] * lp[1:2], axis=-1, keepdims=True))
           - jnp.exp(jnp.sum(lp[2:3] * lp[3:4], axis=-1, keepdims=True)) + lambda_init)
    for j in range(heads):
        cs = slice(j * LANE, (j + 1) * LANE)
        q = q_ref[:, cs]
        if rope:
            q = _rope(q, cq_ref[...], sq_ref[...], first_of_pair)
        q = q * (HD ** -0.5)
        q2 = jnp.concatenate([jnp.where(map0, q, 0.0), jnp.where(map0, 0.0, q)], axis=0).astype(BF16)
        s = lax.dot_general(q2, kk_ref[j], (((1,), (1,)), ((), ())), preferred_element_type=F32)
        e = jnp.exp(s - jnp.max(s, axis=-1, keepdims=True))
        p = e * (1.0 / jnp.sum(e, axis=-1, keepdims=True))
        w = p[0:tq] - lam * p[tq:2 * tq]
        o = jnp.dot(w.astype(BF16), vv_ref[j], preferred_element_type=F32)
        o = o * lax.rsqrt(jnp.mean(o * o, axis=-1, keepdims=True) + 1e-6) * g_ref[...]
        o_ref[:, cs] = (o * (1.0 - lambda_init)).astype(BF16)


def _attention(qkv, lam_p, subln_g, lambda_init, *, row0, n_batch, t_len, heads, tq,
               ctx_k=None, ctx_v=None, rope_tabs=None):
    rope = rope_tabs is not None
    s_len = t_len + (PAST if rope else 0)
    w = heads * LANE
    nhb = HEADS // heads
    nq = t_len // tq
    rq0 = row0 // tq
    rk0 = row0 // t_len
    kern = functools.partial(_attn_kernel, heads=heads, t_len=t_len, s_len=s_len, rope=rope,
                             lambda_init=lambda_init)
    in_specs = [
        pl.BlockSpec((4, HD), lambda b, h, q: (0, 0)),
        pl.BlockSpec((1, VD), lambda b, h, q: (0, 0)),
        pl.BlockSpec((tq, w), lambda b, h, q: (rq0 + b * nq + q, h)),
        pl.BlockSpec((t_len, w), lambda b, h, q: (rk0 + b, nhb + h)),
        pl.BlockSpec((t_len, w), lambda b, h, q: (rk0 + b, 2 * nhb + h)),
    ]
    args = [lam_p, subln_g.reshape(1, VD), qkv, qkv, qkv]
    if rope:
        cos_t, sin_t = rope_tabs
        in_specs += [
            pl.BlockSpec((None, PAST, w), lambda b, h, q: (b, 0, h)),
            pl.BlockSpec((None, PAST, w), lambda b, h, q: (b, 0, h)),
            pl.BlockSpec((tq, LANE), lambda b, h, q: (q, 0)),
            pl.BlockSpec((tq, LANE), lambda b, h, q: (q, 0)),
            pl.BlockSpec((t_len, LANE), lambda b, h, q: (0, 0)),
            pl.BlockSpec((t_len, LANE), lambda b, h, q: (0, 0)),
        ]
        args += [ctx_k, ctx_v, cos_t, sin_t, cos_t, sin_t]
    return pl.pallas_call(
        kern,
        grid=(n_batch, nhb, nq),
        in_specs=in_specs,
        out_specs=pl.BlockSpec((tq, w), lambda b, h, q: (b * nq + q, h)),
        out_shape=jax.ShapeDtypeStruct((n_batch * t_len, D), BF16),
        scratch_shapes=[pltpu.VMEM((heads, s_len, LANE), BF16), pltpu.VMEM((heads, s_len, LANE), BF16)],
        compiler_params=_cparams(("arbitrary", "arbitrary", "arbitrary")),
        name="diff_attn",
    )(*args)


def _rope_tables(t_len):
    t = jnp.arange(t_len)
    half = HD // 2
    freqs = ROPE_THETA ** (-jnp.arange(0, half, 2, dtype=F32) / half)
    ang_r = (t // GRID_W).astype(F32)[:, None] * freqs[None]
    ang_c = (t % GRID_W).astype(F32)[:, None] * freqs[None]
    cr, sr, cc, sc = jnp.cos(ang_r), jnp.sin(ang_r), jnp.cos(ang_c), jnp.sin(ang_c)
    cos64 = jnp.concatenate([cr, cr, cc, cc], axis=-1)
    sin64 = jnp.concatenate([-sr, sr, -sc, sc], axis=-1)
    return jnp.concatenate([cos64, cos64], axis=-1), jnp.concatenate([sin64, sin64], axis=-1)


def _swiglu_rows(h_bf16, wgu_ref, wd_ref, h1_ref):
    tn = 256
    for c in range(D_FF // tn):
        gg = jnp.dot(h_bf16, wgu_ref[:, c * tn:(c + 1) * tn], preferred_element_type=F32)
        uu = jnp.dot(h_bf16, wgu_ref[:, D_FF + c * tn:D_FF + (c + 1) * tn], preferred_element_type=F32)
        h1_ref[:, c * tn:(c + 1) * tn] = (jax.nn.silu(gg) * uu).astype(BF16)
    return jnp.dot(h1_ref[...], wd_ref[...], preferred_element_type=F32)


def _ffn_kernel(y_ref, g_ref, mod_ref, wgu_ref, wd_ref, o_ref, h1_ref):
    y = y_ref[...]
    h = _modnorm(y, g_ref[...], mod_ref[3:4, :], mod_ref[4:5, :]).astype(BF16)
    o_ref[...] = y + mod_ref[5:6, :] * _swiglu_rows(h, wgu_ref, wd_ref, h1_ref)


def _ffn(y, g, mod, wgu_bf16, wd_bf16):
    return pl.pallas_call(
        _ffn_kernel,
        grid=(N_TOK // TM,),
        in_specs=[
            pl.BlockSpec((TM, D), lambda i: (i, 0)),
            pl.BlockSpec((1, D), lambda i: (0, 0)),
            pl.BlockSpec((None, 6, D), lambda i: (_cond_of_tile(i, TM), 0, 0)),
            pl.BlockSpec((D, 2 * D_FF), lambda i: (0, 0), pipeline_mode=pl.Buffered(1)),
            pl.BlockSpec((D_FF, D), lambda i: (0, 0), pipeline_mode=pl.Buffered(1)),
        ],
        out_specs=pl.BlockSpec((TM, D), lambda i: (i, 0)),
        out_shape=jax.ShapeDtypeStruct((N_TOK, D), F32),
        scratch_shapes=[pltpu.VMEM((TM, D_FF), BF16)],
        compiler_params=_cparams(("arbitrary",)),
        name="ffn",
    )(y, g.reshape(1, D), mod, wgu_bf16, wd_bf16)


def _router_kernel(y_ref, g_ref, mod_ref, rt_ref, hs_ref, ii_ref, wf_ref, cnt_ref, carry_ref):
    tm = y_ref.shape[0]

    @pl.when(pl.program_id(0) == 0)
    def _():
        carry_ref[...] = jnp.zeros_like(carry_ref)

    h = _modnorm(y_ref[...], g_ref[...], mod_ref[3:4, :], mod_ref[4:5, :])
    for k in range(SLAB):
        hs_ref[:, k, :] = h[:, k * LANE:(k + 1) * LANE]
    logits = lax.dot_general(rt_ref[...], h.astype(BF16), (((1,), (1,)), ((), ())),
                             preferred_element_type=F32)
    eidx = lax.broadcasted_iota(I32, (N_EXP, tm), 0)
    eidx_f = eidx.astype(F32)
    m1 = jnp.max(logits, axis=0, keepdims=True)
    i1 = jnp.min(jnp.where(logits == m1, eidx_f, float(N_EXP)), axis=0, keepdims=True)
    oh1 = eidx_f == i1
    rest = jnp.where(oh1, -jnp.inf, logits)
    m2 = jnp.max(rest, axis=0, keepdims=True)
    i2 = jnp.min(jnp.where(rest == m2, eidx_f, float(N_EXP)), axis=0, keepdims=True)
    oh2 = eidx_f == i2
    i1 = i1.astype(I32)
    i2 = i2.astype(I32)
    e2 = jnp.exp(m2 - m1)
    w1 = 1.0 / (1.0 + e2)
    w2 = e2 / (1.0 + e2)
    sel = jnp.where(oh1 | oh2, 1.0, 0.0)
    before = (lax.broadcasted_iota(I32, (tm, tm), 0) < lax.broadcasted_iota(I32, (tm, tm), 1))
    cum = jnp.dot(sel.astype(BF16), jnp.where(before, 1.0, 0.0).astype(BF16), preferred_element_type=F32)
    cum = cum + carry_ref[:, 0:1]
    r1 = jnp.sum(jnp.where(oh1, cum, 0.0), axis=0, keepdims=True).astype(I32)
    r2 = jnp.sum(jnp.where(oh2, cum, 0.0), axis=0, keepdims=True).astype(I32)
    ii_ref[...] = jnp.where(eidx == 0, i1, jnp.where(eidx == 1, i2, jnp.where(eidx == 2, r1,
                            jnp.where(eidx == 3, r2, 0))))
    wf_ref[...] = jnp.where(eidx == 0, w1, jnp.where(eidx == 1, w2, 0.0))
    carry_ref[...] = carry_ref[...] + jnp.sum(sel, axis=1, keepdims=True)
    cnt_ref[...] = carry_ref[...].astype(I32)


def _router(y, g, mod, router_t_bf16):
    return pl.pallas_call(
        _router_kernel,
        grid=(N_TOK // TM,),
        in_specs=[
            pl.BlockSpec((TM, D), lambda i: (i, 0)),
            pl.BlockSpec((1, D), lambda i: (0, 0)),
            pl.BlockSpec((None, 6, D), lambda i: (_cond_of_tile(i, TM), 0, 0)),
            pl.BlockSpec((N_EXP, D), lambda i: (0, 0)),
        ],
        out_specs=[
            pl.BlockSpec((TM, SLAB, LANE), lambda i: (i, 0, 0)),
            pl.BlockSpec((N_EXP, TM), lambda i: (0, i)),
            pl.BlockSpec((N_EXP, TM), lambda i: (0, i)),
            pl.BlockSpec((N_EXP, LANE), lambda i: (0, 0)),
        ],
        out_shape=[
            jax.ShapeDtypeStruct((N_TOK, SLAB, LANE), F32),
            jax.ShapeDtypeStruct((N_EXP, N_TOK), I32),
            jax.ShapeDtypeStruct((N_EXP, N_TOK), F32),
            jax.ShapeDtypeStruct((N_EXP, LANE), I32),
        ],
        scratch_shapes=[pltpu.VMEM((N_EXP, LANE), F32)],
        compiler_params=_cparams(("arbitrary",)),
        name="router",
    )(y, g.reshape(1, D), mod, router_t_bf16)


def _row_copy(src_hbm, dst_ref, sem, src_row, dst_row):
    return pltpu.make_async_copy(src_hbm.at[src_row], dst_ref.at[dst_row], sem)


def _dispatch_kernel(p1_ref, p2_ref, hs_hbm, xs_in_hbm, xs_hbm, sem):
    del xs_in_hbm
    base = pl.program_id(0) * TM_DMA

    def issue(r, c):
        _row_copy(hs_hbm, xs_hbm, sem, base + r, p1_ref[base + r]).start()
        _row_copy(hs_hbm, xs_hbm, sem, base + r, p2_ref[base + r]).start()
        return c

    lax.fori_loop(0, TM_DMA, issue, 0)

    def drain(r, c):
        _row_copy(hs_hbm, xs_hbm, sem, 0, 0).wait()
        _row_copy(hs_hbm, xs_hbm, sem, 0, 0).wait()
        return c

    lax.fori_loop(0, TM_DMA, drain, 0)


def _dispatch(pos1, pos2, hs, xs_zero):
    return pl.pallas_call(
        _dispatch_kernel,
        grid_spec=pltpu.PrefetchScalarGridSpec(
            num_scalar_prefetch=2,
            grid=(N_TOK // TM_DMA,),
            in_specs=[pl.BlockSpec(memory_space=pl.ANY), pl.BlockSpec(memory_space=pl.ANY)],
            out_specs=pl.BlockSpec(memory_space=pl.ANY),
            scratch_shapes=[pltpu.SemaphoreType.DMA(())],
        ),
        out_shape=jax.ShapeDtypeStruct(xs_zero.shape, F32),
        input_output_aliases={3: 0},
        compiler_params=_cparams(("arbitrary",)),
        name="moe_dispatch",
    )(pos1, pos2, hs, xs_zero)


def _expert_kernel(te_ref, tv_ref, x_ref, wgu_ref, wd_ref, o_ref, xb_ref, h1_ref):
    del te_ref
    valid = tv_ref[pl.program_id(0)] == 1

    @pl.when(valid)
    def _():
        for k in range(SLAB):
            xb_ref[:, k * LANE:(k + 1) * LANE] = x_ref[:, k, :].astype(BF16)
        acc = _swiglu_rows(xb_ref[...], wgu_ref, wd_ref, h1_ref)
        for k in range(SLAB):
            o_ref[:, k, :] = acc[:, k * LANE:(k + 1) * LANE]

    @pl.when(jnp.logical_not(valid))
    def _():
        o_ref[...] = jnp.zeros_like(o_ref)


def _experts(tile_expert, tile_valid, xs, wgu_bf16, wd_bf16):
    return pl.pallas_call(
        _expert_kernel,
        grid_spec=pltpu.PrefetchScalarGridSpec(
            num_scalar_prefetch=2,
            grid=(MAX_TILES,),
            in_specs=[
                pl.BlockSpec((TM_E, SLAB, LANE), lambda i, te, tv: (i, 0, 0)),
                pl.BlockSpec((None, D, 2 * D_FF), lambda i, te, tv: (te[i], 0, 0)),
                pl.BlockSpec((None, D_FF, D), lambda i, te, tv: (te[i], 0, 0)),
            ],
            out_specs=pl.BlockSpec((TM_E, SLAB, LANE), lambda i, te, tv: (i, 0, 0)),
            scratch_shapes=[pltpu.VMEM((TM_E, D), BF16), pltpu.VMEM((TM_E, D_FF), BF16)],
        ),
        out_shape=jax.ShapeDtypeStruct(xs.shape, F32),
        compiler_params=_cparams(("arbitrary",)),
        name="moe_experts",
    )(tile_expert, tile_valid, xs, wgu_bf16, wd_bf16)


def _combine_kernel(p1_ref, p2_ref, y_ref, mod_ref, w_ref, ys_hbm, o_ref, buf1, buf2, sem):
    base = pl.program_id(0) * TM_DMA

    def issue(r, c):
        _row_copy(ys_hbm, buf1, sem, p1_ref[base + r], r).start()
        _row_copy(ys_hbm, buf2, sem, p2_ref[base + r], r).start()
        return c

    lax.fori_loop(0, TM_DMA, issue, 0)

    def drain(r, c):
        _row_copy(ys_hbm, buf1, sem, 0, 0).wait()
        _row_copy(ys_hbm, buf2, sem, 0, 0).wait()
        return c

    lax.fori_loop(0, TM_DMA, drain, 0)
    w1 = w_ref[:, 0:1]
    w2 = w_ref[:, 1:2]
    for k in range(SLAB):
        cs = slice(k * LANE, (k + 1) * LANE)
        f = w1 * buf1[:, k, :] + w2 * buf2[:, k, :]
        o_ref[:, cs] = y_ref[:, cs] + mod_ref[5:6, cs] * f


def _combine(pos1, pos2, y, mod, w_cols, ys):
    return pl.pallas_call(
        _combine_kernel,
        grid_spec=pltpu.PrefetchScalarGridSpec(
            num_scalar_prefetch=2,
            grid=(N_TOK // TM_DMA,),
            in_specs=[
                pl.BlockSpec((TM_DMA, D), lambda i, p1, p2: (i, 0)),
                pl.BlockSpec((None, 6, D), lambda i, p1, p2: (_cond_of_tile(i, TM_DMA), 0, 0)),
                pl.BlockSpec((TM_DMA, 2), lambda i, p1, p2: (i, 0)),
                pl.BlockSpec(memory_space=pl.ANY),
            ],
            out_specs=pl.BlockSpec((TM_DMA, D), lambda i, p1, p2: (i, 0)),
            scratch_shapes=[
                pltpu.VMEM((TM_DMA, SLAB, LANE), F32),
                pltpu.VMEM((TM_DMA, SLAB, LANE), F32),
                pltpu.SemaphoreType.DMA(()),
            ],
        ),
        out_shape=jax.ShapeDtypeStruct((N_TOK, D), F32),
        compiler_params=_cparams(("arbitrary",)),
        name="moe_combine",
    )(pos1, pos2, y, mod, w_cols, ys)


def _moe(y, g, mod, router, w_gu, w_down):
    hs, ii, wf, cnt = _router(y, g, mod, router.T.astype(BF16))
    counts = cnt[:, 0]
    tiles_e = (counts + TM_E - 1) // TM_E
    tile_end = jnp.cumsum(tiles_e)
    offs = (tile_end - tiles_e) * TM_E
    n_tiles = tile_end[-1]
    t = jnp.arange(MAX_TILES, dtype=I32)
    tile_valid = (t < n_tiles).astype(I32)
    tile_expert = jnp.sum((jnp.minimum(t, n_tiles - 1)[:, None] >= tile_end[None, :]).astype(I32), axis=1)
    e_ids = jnp.arange(N_EXP, dtype=I32)[:, None]
    pos1 = jnp.sum(jnp.where(ii[0:1] == e_ids, offs[:, None], 0), axis=0) + ii[2]
    pos2 = jnp.sum(jnp.where(ii[1:2] == e_ids, offs[:, None], 0), axis=0) + ii[3]
    xs = _dispatch(pos1, pos2, hs, jnp.zeros((MAX_TILES * TM_E, SLAB, LANE), F32))
    ys = _experts(tile_expert, tile_valid, xs, w_gu.astype(BF16), w_down.astype(BF16))
    return _combine(pos1, pos2, y, mod, wf[0:2].T, ys)


def _final_norm_kernel(y_ref, g_ref, o_ref):
    y = y_ref[...]
    o_ref[...] = y * lax.rsqrt(jnp.mean(y * y, axis=-1, keepdims=True) + 1e-6) * g_ref[...]


def _final_norm(y, g, *, row0, rows):
    rb0 = row0 // TM
    return pl.pallas_call(
        _final_norm_kernel,
        grid=(rows // TM,),
        in_specs=[
            pl.BlockSpec((TM, D), lambda i: (rb0 + i, 0)),
            pl.BlockSpec((1, D), lambda i: (0, 0)),
        ],
        out_specs=pl.BlockSpec((TM, D), lambda i: (i, 0)),
        out_shape=jax.ShapeDtypeStruct((rows, D), F32),
        compiler_params=_cparams(("arbitrary",)),
        name="final_norm",
    )(y, g.reshape(1, D))


def kernel(x_prompt, x_sample, state_rglru, cache_k, cache_v, c, c_ctx, norm_g, ada_w, ada_b,
           gm_w_in, gm_ln_g, gm_w_s, gm_b_s, gm_w_out, rg_w_in, rg_conv_w, rg_conv_b, rg_w_gate,
           rg_b_gate, rg_lambda, rg_w_out, att_w_qkv, att_lambda, att_subln_g, att_w_out,
           ff_w_gu, ff_w_down, moe_router, moe_w_gu, moe_w_down, final_g):
    y = jnp.concatenate([x_prompt.reshape(N_P, D), x_sample.reshape(N_L, D)], axis=0)
    conds = jnp.concatenate([c_ctx[None], c, jnp.zeros((N_COND - 1 - B_L, D), F32)], axis=0)
    mods = _adaln(conds, ada_w, ada_b)
    nb = D_RNN // RNN_BS
    new_rnn = new_k = new_v = None
    for i in range(DEPTH):
        mod = mods[i]
        kind, slot = i % 3, i // 3
        if kind == 0:
            uv = _mod_matmul(y, norm_g[i, 0], mod, gm_w_in[slot].astype(BF16), tn=512, gelu_cols=2 * D)
            a = _gmlp_spatial(uv, gm_ln_g[slot], gm_w_s[slot].astype(BF16), gm_b_s[slot].T)
            y = _matmul_res(a, gm_w_out[slot].astype(BF16), y, mod)
        elif kind == 1:
            yx = _mod_matmul(y, norm_g[i, 0], mod, rg_w_in[slot].astype(BF16), tn=256, gelu_cols=D_RNN)
            wg_cat = rg_w_gate[slot].transpose(2, 3, 0, 1, 4).reshape(nb, RNN_BS, 4 * RNN_BS).astype(BF16)
            bg_cat = rg_b_gate[slot].reshape(2, 2, nb, RNN_BS).transpose(2, 0, 1, 3).reshape(nb, 1, 4 * RNN_BS)
            rg_args = (yx, rg_conv_w[slot], rg_conv_b[slot], wg_cat, bg_cat, rg_lambda[slot])
            a_p, fin = _rglru(*rg_args, jnp.zeros((2, B_P, D_RNN), F32),
                              row0=0, n_batch=B_P, n_seq=SUBLANE, t_len=T_P)
            a_l, _ = _rglru(*rg_args, state_rglru[:, slot].transpose(1, 0, 2),
                            row0=N_P, n_batch=B_L, n_seq=B_L, t_len=T_L)
            new_rnn = fin.transpose(1, 0, 2)[:, None]
            y = _matmul_res(jnp.concatenate([a_p, a_l], axis=0), rg_w_out[slot].astype(BF16), y, mod)
        else:
            lambda_init = 0.8 - 0.6 * math.exp(-0.3 * i)
            qkv = _mod_matmul(y, norm_g[i, 0], mod, att_w_qkv[slot].astype(BF16), tn=512, gelu_cols=0)
            att = (qkv, att_lambda[slot], att_subln_g[slot], lambda_init)
            a_p = _attention(*att, row0=0, n_batch=B_P, t_len=T_P, heads=HEADS, tq=T_P)
            a_l = _attention(*att, row0=N_P, n_batch=B_L, t_len=T_L, heads=1, tq=256,
                             ctx_k=cache_k[:, slot].reshape(B_L, PAST, D),
                             ctx_v=cache_v[:, slot].reshape(B_L, PAST, D),
                             rope_tabs=_rope_tables(T_L))
            new_k = qkv[:N_P, D:2 * D].reshape(B_P, 1, T_P, HEADS, 2 * HD)
            new_v = qkv[:N_P, 2 * D:].reshape(B_P, 1, T_P, HEADS, VD)
            y = _matmul_res(jnp.concatenate([a_p, a_l], axis=0), att_w_out[slot].astype(BF16), y, mod)
        fslot = i // 2
        if i % 2 == 0:
            y = _ffn(y, norm_g[i, 1], mod, ff_w_gu[fslot].astype(BF16), ff_w_down[fslot].astype(BF16))
        else:
            y = _moe(y, norm_g[i, 1], mod, moe_router[fslot], moe_w_gu[fslot], moe_w_down[fslot])
    y_prompt = _final_norm(y, final_g, row0=0, rows=N_P).reshape(B_P, T_P, D)
    y_sample = _final_norm(y, final_g, row0=N_P, rows=N_L).reshape(B_L, T_L, D)
    return (y_prompt, y_sample, new_rnn, new_k, new_v)
```

```python
import functools
import math

import jax
import jax.numpy as jnp
from jax import lax
from jax.experimental import pallas as pl
from jax.experimental.pallas import tpu as pltpu

F32 = jnp.float32
BF16 = jnp.bfloat16
I32 = jnp.int32

D = 1024
DEPTH = 4
B_P, T_P = 32, 256
B_L, T_L = 4, 2048
PAST = 256
N_P = B_P * T_P
N_L = B_L * T_L
N_TOK = N_P + N_L
N_COND = 8
GRID_W = 64
CHUNK = 128
GROUPS = 8
D_RNN = 1280
RNN_BS = 128
CONV_W = 4
RGLRU_C = 8.0
HD = 64
VD = 128
HEADS = 8
ROPE_THETA = 10000.0
D_FF = 2816
N_EXP = 8

LANE = 128
SUBLANE = 8
VMEM_LIMIT = 56 * 1024 * 1024
TM = 512
TM_E = 256
CH = 32
MAX_TILES = -(-(2 * N_TOK + N_EXP * (CH + TM_E - 1)) // TM_E)
STAGE_ROWS = 2 * TM + N_EXP * CH
SLAB = D // LANE


def _cparams(sem):
    return pltpu.CompilerParams(dimension_semantics=sem, vmem_limit_bytes=VMEM_LIMIT)


def _cond_of_tile(i, tm):
    r0 = i * tm
    return jnp.where(r0 < N_P, 0, 1 + (r0 - N_P) // T_L)


def _modnorm(x, g, shift, scale):
    ms = jnp.mean(x * x, axis=-1, keepdims=True)
    h = x * lax.rsqrt(ms + 1e-6) * g
    return h * (1.0 + scale) + shift


def _adaln_kernel(c_ref, w_ref, b_ref, o_ref):
    s = jax.nn.silu(c_ref[...]).astype(BF16)
    o_ref[...] = jnp.dot(s, w_ref[...].astype(BF16), preferred_element_type=F32) + b_ref[...]


def _adaln(conds, ada_w, ada_b):
    tn = 1536
    n6 = 6 * D
    out = pl.pallas_call(
        _adaln_kernel,
        grid=(DEPTH, n6 // tn),
        in_specs=[
            pl.BlockSpec((N_COND, D), lambda l, j: (0, 0)),
            pl.BlockSpec((None, D, tn), lambda l, j: (l, 0, j)),
            pl.BlockSpec((None, 1, tn), lambda l, j: (l, 0, j)),
        ],
        out_specs=pl.BlockSpec((None, N_COND, tn), lambda l, j: (l, 0, j)),
        out_shape=jax.ShapeDtypeStruct((DEPTH, N_COND, n6), F32),
        compiler_params=_cparams(("arbitrary", "arbitrary")),
        name="adaln",
    )(conds, ada_w, ada_b.reshape(DEPTH, 1, n6))
    return out.reshape(DEPTH, N_COND, 6, D)


def _mod_matmul_kernel(x_ref, g_ref, mod_ref, w_ref, o_ref, *, tn, gelu_cols):
    h = _modnorm(x_ref[...], g_ref[...], mod_ref[0:1, :], mod_ref[1:2, :]).astype(BF16)
    for c in range(o_ref.shape[1] // tn):
        acc = jnp.dot(h, w_ref[:, c * tn:(c + 1) * tn], preferred_element_type=F32)
        if c * tn < gelu_cols:
            acc = jax.nn.gelu(acc)
        o_ref[:, c * tn:(c + 1) * tn] = acc


def _mod_matmul(y, g, mod, w_bf16, slot, *, tn, gelu_cols):
    nout = w_bf16.shape[2]
    return pl.pallas_call(
        functools.partial(_mod_matmul_kernel, tn=tn, gelu_cols=gelu_cols),
        grid=(N_TOK // TM,),
        in_specs=[
            pl.BlockSpec((TM, D), lambda i: (i, 0)),
            pl.BlockSpec((1, D), lambda i: (0, 0)),
            pl.BlockSpec((None, 6, D), lambda i: (_cond_of_tile(i, TM), 0, 0)),
            pl.BlockSpec((None, D, nout), lambda i: (slot, 0, 0)),
        ],
        out_specs=pl.BlockSpec((TM, nout), lambda i: (i, 0)),
        out_shape=jax.ShapeDtypeStruct((N_TOK, nout), F32),
        compiler_params=_cparams(("arbitrary",)),
        name="mod_matmul",
    )(y, g.reshape(1, D), mod, w_bf16)


def _matmul_res_kernel(a_ref, w_ref, y_ref, mod_ref, o_ref):
    acc = jnp.dot(a_ref[...], w_ref[...], preferred_element_type=F32)
    o_ref[...] = y_ref[...] + mod_ref[2:3, :] * acc


def _matmul_res(a_bf16, w_bf16, slot, y, mod):
    k = a_bf16.shape[1]
    return pl.pallas_call(
        _matmul_res_kernel,
        grid=(N_TOK // TM,),
        in_specs=[
            pl.BlockSpec((TM, k), lambda i: (i, 0)),
            pl.BlockSpec((None, k, D), lambda i: (slot, 0, 0)),
            pl.BlockSpec((TM, D), lambda i: (i, 0)),
            pl.BlockSpec((None, 6, D), lambda i: (_cond_of_tile(i, TM), 0, 0)),
        ],
        out_specs=pl.BlockSpec((TM, D), lambda i: (i, 0)),
        out_shape=jax.ShapeDtypeStruct((N_TOK, D), F32),
        compiler_params=_cparams(("arbitrary",)),
        name="matmul_res",
    )(a_bf16, w_bf16, y, mod)


def _gmlp_kernel(uv_ref, lng_ref, ws_ref, bs_ref, o_ref):
    rows = uv_ref.shape[0]
    v = uv_ref[:, D:]
    vc = v - jnp.mean(v, axis=-1, keepdims=True)
    vn = vc * lax.rsqrt(jnp.mean(vc * vc, axis=-1, keepdims=True) + 1e-5) * lng_ref[...]
    vn = vn.astype(BF16)
    for ci in range(rows // CHUNK):
        r0 = ci * CHUNK
        for g in range(GROUPS):
            c0 = g * LANE
            s = jnp.dot(ws_ref[g], vn[r0:r0 + CHUNK, c0:c0 + LANE], preferred_element_type=F32)
            s = s + bs_ref[:, g:g + 1]
            o_ref[r0:r0 + CHUNK, c0:c0 + LANE] = (uv_ref[r0:r0 + CHUNK, c0:c0 + LANE] * s).astype(BF16)


def _gmlp_spatial(uv, ln_g, w_s_bf16, b_s_t):
    rows = 2 * CHUNK
    return pl.pallas_call(
        _gmlp_kernel,
        grid=(N_TOK // rows,),
        in_specs=[
            pl.BlockSpec((rows, 2 * D), lambda i: (i, 0)),
            pl.BlockSpec((1, D), lambda i: (0, 0)),
            pl.BlockSpec((GROUPS, CHUNK, CHUNK), lambda i: (0, 0, 0)),
            pl.BlockSpec((CHUNK, GROUPS), lambda i: (0, 0)),
        ],
        out_specs=pl.BlockSpec((rows, D), lambda i: (i, 0)),
        out_shape=jax.ShapeDtypeStruct((N_TOK, D), BF16),
        compiler_params=_cparams(("arbitrary",)),
        name="gmlp_spatial",
    )(uv, ln_g.reshape(1, D), w_s_bf16, b_s_t)


def _rglru_kernel(yb_ref, xb_ref, cw_ref, cb_ref, wg_ref, bg_ref, lam_ref, h0_ref,
                  o_ref, fin_ref, af_ref, bf_ref, ab_ref, bb_ref, *, n_seq, t_len, row_chunk):
    stride = t_len + SUBLANE
    tix = lax.broadcasted_iota(I32, (t_len, 1), 0)
    sp_f = jax.nn.softplus(-lam_ref[0:1, :])
    sp_b = jax.nn.softplus(-lam_ref[1:2, :])
    for g in range(n_seq):
        x = xb_ref[g * t_len:(g + 1) * t_len, :]
        xm1 = jnp.where(tix >= 1, pltpu.roll(x, 1, 0), 0.0)
        xp1 = jnp.where(tix <= t_len - 2, pltpu.roll(x, t_len - 1, 0), 0.0)
        xp2 = jnp.where(tix <= t_len - 3, pltpu.roll(x, t_len - 2, 0), 0.0)
        xc = cb_ref[...] + xm1 * cw_ref[0:1, :]
        xc = xc + x * cw_ref[1:2, :]
        xc = xc + xp1 * cw_ref[2:3, :]
        xc = xc + xp2 * cw_ref[3:4, :]
        for r0 in range(0, t_len, row_chunk):
            xcc = xc[r0:r0 + row_chunk]
            gates = jnp.dot(xcc.astype(BF16), wg_ref[...], preferred_element_type=F32) + bg_ref[...]
            d0 = g * stride + r0
            for d, (sp, a_ref, b_ref) in enumerate(((sp_f, af_ref, bf_ref), (sp_b, ab_ref, bb_ref))):
                r = jax.nn.sigmoid(gates[:, (2 * d) * LANE:(2 * d + 1) * LANE])
                i = jax.nn.sigmoid(gates[:, (2 * d + 1) * LANE:(2 * d + 2) * LANE])
                log_a = (-RGLRU_C * r) * sp
                a = jnp.exp(log_a)
                one_minus_a2 = jnp.tanh(-log_a) * (a * a + 1.0)
                a_ref[d0:d0 + row_chunk, :] = a
                b_ref[d0:d0 + row_chunk, :] = jnp.sqrt(one_minus_a2) * (i * xcc)

    def step(s, carry):
        hf, hb = carry
        tf = pl.ds(s, n_seq, stride=stride)
        tb = pl.ds(t_len - 1 - s, n_seq, stride=stride)
        hf = af_ref[tf, :] * hf + bf_ref[tf, :]
        hb = ab_ref[tb, :] * hb + bb_ref[tb, :]
        bf_ref[tf, :] = hf
        bb_ref[tb, :] = hb
        return hf, hb

    hf, hb = lax.fori_loop(0, t_len, step, (h0_ref[0], h0_ref[1]))
    fin_ref[0] = hf
    fin_ref[1] = hb
    for g in range(n_seq):
        rs = slice(g * stride, g * stride + t_len)
        hsum = bf_ref[rs, :] + bb_ref[rs, :]
        o_ref[g * t_len:(g + 1) * t_len, :] = (hsum * yb_ref[g * t_len:(g + 1) * t_len, :]).astype(BF16)


def _rglru(yx, cw, cb, wg_cat, bg_cat, lam, h0, *, row0, n_batch, n_seq, t_len):
    rows = n_seq * t_len
    nb = D_RNN // RNN_BS
    rb0 = row0 // rows
    kern = functools.partial(_rglru_kernel, n_seq=n_seq, t_len=t_len, row_chunk=min(t_len, 512))
    scan_buf = pltpu.VMEM((n_seq * (t_len + SUBLANE), RNN_BS), F32)
    return pl.pallas_call(
        kern,
        grid=(n_batch // n_seq, nb),
        in_specs=[
            pl.BlockSpec((rows, RNN_BS), lambda s, c: (rb0 + s, c)),
            pl.BlockSpec((rows, RNN_BS), lambda s, c: (rb0 + s, nb + c)),
            pl.BlockSpec((CONV_W, RNN_BS), lambda s, c: (0, c)),
            pl.BlockSpec((1, RNN_BS), lambda s, c: (0, c)),
            pl.BlockSpec((None, RNN_BS, 4 * RNN_BS), lambda s, c: (c, 0, 0)),
            pl.BlockSpec((None, 1, 4 * RNN_BS), lambda s, c: (c, 0, 0)),
            pl.BlockSpec((2, RNN_BS), lambda s, c: (0, c)),
            pl.BlockSpec((2, n_seq, RNN_BS), lambda s, c: (0, s, c)),
        ],
        out_specs=[
            pl.BlockSpec((rows, RNN_BS), lambda s, c: (s, c)),
            pl.BlockSpec((2, n_seq, RNN_BS), lambda s, c: (0, s, c)),
        ],
        out_shape=[
            jax.ShapeDtypeStruct((n_batch * t_len, D_RNN), BF16),
            jax.ShapeDtypeStruct((2, n_batch, D_RNN), F32),
        ],
        scratch_shapes=[scan_buf, scan_buf, scan_buf, scan_buf],
        compiler_params=_cparams(("arbitrary", "arbitrary")),
        name="rglru",
    )(yx, yx, cw, cb.reshape(1, D_RNN), wg_cat, bg_cat, lam, h0)


def _rope(x, cos, sin_signed, first_of_pair):
    partner = jnp.where(first_of_pair, pltpu.roll(x, LANE - 16, 1), pltpu.roll(x, 16, 1))
    return x * cos + partner * sin_signed


def _attn_kernel(*refs, heads, t_len, s_len, rope, lambda_init):
    if rope:
        (lam_ref, g_ref, q_ref, k_ref, v_ref, ck_ref, cv_ref, cq_ref, sq_ref, ckk_ref, skk_ref,
         o_ref, kk_ref, vv_ref) = refs
    else:
        lam_ref, g_ref, q_ref, k_ref, v_ref, o_ref, kk_ref, vv_ref = refs
    tq = q_ref.shape[0]
    lane = lax.broadcasted_iota(I32, (1, LANE), 1)
    first_of_pair = (lane & 16) == 0
    map0 = lane < HD

    @pl.when(pl.program_id(2) == 0)
    def _():
        for j in range(heads):
            cs = slice(j * LANE, (j + 1) * LANE)
            k = k_ref[:, cs]
            if rope:
                k = _rope(k, ckk_ref[...], skk_ref[...], first_of_pair)
            kk_ref[j, 0:t_len, :] = k.astype(BF16)
            vv_ref[j, 0:t_len, :] = v_ref[:, cs].astype(BF16)
            if s_len > t_len:
                kk_ref[j, t_len:s_len, :] = ck_ref[:, cs].astype(BF16)
                vv_ref[j, t_len:s_len, :] = cv_ref[:, cs].astype(BF16)

    lp = lam_ref[...]
    lam = (jnp.exp(jnp.sum(lp[0:1] * lp[1:2], axis=-1, keepdims=True))
           - jnp.exp(jnp.sum(lp[2:3] * lp[3:4], axis=-1, keepdims=True)) + lambda_init)
    for j in range(heads):
        cs = slice(j * LANE, (j + 1) * LANE)
        q = q_ref[:, cs]
        if rope:
            q = _rope(q, cq_ref[...], sq_ref[...], first_of_pair)
        q = q * (HD ** -0.5)
        q2 = jnp.concatenate([jnp.where(map0, q, 0.0), jnp.where(map0, 0.0, q)], axis=0).astype(BF16)
        s = lax.dot_general(q2, kk_ref[j], (((1,), (1,)), ((), ())), preferred_element_type=F32)
        e = jnp.exp(s - jnp.max(s, axis=-1, keepdims=True))
        p = e * (1.0 / jnp.sum(e, axis=-1, keepdims=True))
        w = p[0:tq] - lam * p[tq:2 * tq]
        o = jnp.dot(w.astype(BF16), vv_ref[j], preferred_element_type=F32)
        o = o * lax.rsqrt(jnp.mean(o * o, axis=-1, keepdims=True) + 1e-6) * g_ref[...]
        o_ref[:, cs] = (o * (1.0 - lambda_init)).astype(BF16)


def _attention(qkv, lam_p, subln_g, lambda_init, *, row0, n_batch, t_len, heads, tq,
               ctx_k=None, ctx_v=None, rope_tabs=None):
    rope = rope_tabs is not None
    s_len = t_len + (PAST if rope else 0)
    w = heads * LANE
    nhb = HEADS // heads
    nq = t_len // tq
    rq0 = row0 // tq
    rk0 = row0 // t_len
    kern = functools.partial(_attn_kernel, heads=heads, t_len=t_len, s_len=s_len, rope=rope,
                             lambda_init=lambda_init)
    in_specs = [
        pl.BlockSpec((4, HD), lambda b, h, q: (0, 0)),
        pl.BlockSpec((1, VD), lambda b, h, q: (0, 0)),
        pl.BlockSpec((tq, w), lambda b, h, q: (rq0 + b * nq + q, h)),
        pl.BlockSpec((t_len, w), lambda b, h, q: (rk0 + b, nhb + h)),
        pl.BlockSpec((t_len, w), lambda b, h, q: (rk0 + b, 2 * nhb + h)),
    ]
    args = [lam_p, subln_g.reshape(1, VD), qkv, qkv, qkv]
    if rope:
        cos_t, sin_t = rope_tabs
        in_specs += [
            pl.BlockSpec((None, PAST, w), lambda b, h, q: (b, 0, h)),
            pl.BlockSpec((None, PAST, w), lambda b, h, q: (b, 0, h)),
            pl.BlockSpec((tq, LANE), lambda b, h, q: (q, 0)),
            pl.BlockSpec((tq, LANE), lambda b, h, q: (q, 0)),
            pl.BlockSpec((t_len, LANE), lambda b, h, q: (0, 0)),
            pl.BlockSpec((t_len, LANE), lambda b, h, q: (0, 0)),
        ]
        args += [ctx_k, ctx_v, cos_t, sin_t, cos_t, sin_t]
    return pl.pallas_call(
        kern,
        grid=(n_batch, nhb, nq),
        in_specs=in_specs,
        out_specs=pl.BlockSpec((tq, w), lambda b, h, q: (b * nq + q, h)),
        out_shape=jax.ShapeDtypeStruct((n_batch * t_len, D), BF16),
        scratch_shapes=[pltpu.VMEM((heads, s_len, LANE), BF16), pltpu.VMEM((heads, s_len, LANE), BF16)],
        compiler_params=_cparams(("arbitrary", "arbitrary", "arbitrary")),
        name="diff_attn",
    )(*args)


def _rope_tables(t_len):
    t = jnp.arange(t_len)
    half = HD // 2
    freqs = ROPE_THETA ** (-jnp.arange(0, half, 2, dtype=F32) / half)
    ang_r = (t // GRID_W).astype(F32)[:, None] * freqs[None]
    ang_c = (t % GRID_W).astype(F32)[:, None] * freqs[None]
    cr, sr, cc, sc = jnp.cos(ang_r), jnp.sin(ang_r), jnp.cos(ang_c), jnp.sin(ang_c)
    cos64 = jnp.concatenate([cr, cr, cc, cc], axis=-1)
    sin64 = jnp.concatenate([-sr, sr, -sc, sc], axis=-1)
    return jnp.concatenate([cos64, cos64], axis=-1), jnp.concatenate([sin64, sin64], axis=-1)


def _swiglu_rows(h_bf16, wgu_ref, wd_ref, h1_ref):
    tn = 256
    for c in range(D_FF // tn):
        gg = jnp.dot(h_bf16, wgu_ref[:, c * tn:(c + 1) * tn], preferred_element_type=F32)
        uu = jnp.dot(h_bf16, wgu_ref[:, D_FF + c * tn:D_FF + (c + 1) * tn], preferred_element_type=F32)
        h1_ref[:, c * tn:(c + 1) * tn] = (jax.nn.silu(gg) * uu).astype(BF16)
    return jnp.dot(h1_ref[...], wd_ref[...], preferred_element_type=F32)


def _ffn_kernel(y_ref, g_ref, mod_ref, wgu_ref, wd_ref, o_ref, h1_ref):
    y = y_ref[...]
    h = _modnorm(y, g_ref[...], mod_ref[3:4, :], mod_ref[4:5, :]).astype(BF16)
    o_ref[...] = y + mod_ref[5:6, :] * _swiglu_rows(h, wgu_ref, wd_ref, h1_ref)


def _ffn(y, g, mod, wgu_bf16, wd_bf16, slot):
    return pl.pallas_call(
        _ffn_kernel,
        grid=(N_TOK // TM,),
        in_specs=[
            pl.BlockSpec((TM, D), lambda i: (i, 0)),
            pl.BlockSpec((1, D), lambda i: (0, 0)),
            pl.BlockSpec((None, 6, D), lambda i: (_cond_of_tile(i, TM), 0, 0)),
            pl.BlockSpec((None, D, 2 * D_FF), lambda i: (slot, 0, 0), pipeline_mode=pl.Buffered(1)),
            pl.BlockSpec((None, D_FF, D), lambda i: (slot, 0, 0), pipeline_mode=pl.Buffered(1)),
        ],
        out_specs=pl.BlockSpec((TM, D), lambda i: (i, 0)),
        out_shape=jax.ShapeDtypeStruct((N_TOK, D), F32),
        scratch_shapes=[pltpu.VMEM((TM, D_FF), BF16)],
        compiler_params=_cparams(("arbitrary",)),
        name="ffn",
    )(y, g.reshape(1, D), mod, wgu_bf16, wd_bf16)


def _router_kernel(y_ref, g_ref, mod_ref, rt_ref, hs_ref, ii_ref, wf_ref, cnt_ref, tb_ref, carry_ref):
    tm = y_ref.shape[0]

    @pl.when(pl.program_id(0) == 0)
    def _():
        carry_ref[...] = jnp.zeros_like(carry_ref)

    tb_ref[...] = carry_ref[...].astype(I32)

    h = _modnorm(y_ref[...], g_ref[...], mod_ref[3:4, :], mod_ref[4:5, :])
    for k in range(SLAB):
        hs_ref[:, k, :] = h[:, k * LANE:(k + 1) * LANE]
    logits = lax.dot_general(rt_ref[...], h.astype(BF16), (((1,), (1,)), ((), ())),
                             preferred_element_type=F32)
    eidx = lax.broadcasted_iota(I32, (N_EXP, tm), 0)
    eidx_f = eidx.astype(F32)
    m1 = jnp.max(logits, axis=0, keepdims=True)
    i1 = jnp.min(jnp.where(logits == m1, eidx_f, float(N_EXP)), axis=0, keepdims=True)
    oh1 = eidx_f == i1
    rest = jnp.where(oh1, -jnp.inf, logits)
    m2 = jnp.max(rest, axis=0, keepdims=True)
    i2 = jnp.min(jnp.where(rest == m2, eidx_f, float(N_EXP)), axis=0, keepdims=True)
    oh2 = eidx_f == i2
    i1 = i1.astype(I32)
    i2 = i2.astype(I32)
    e2 = jnp.exp(m2 - m1)
    w1 = 1.0 / (1.0 + e2)
    w2 = e2 / (1.0 + e2)
    sel = jnp.where(oh1 | oh2, 1.0, 0.0)
    before = (lax.broadcasted_iota(I32, (tm, tm), 0) < lax.broadcasted_iota(I32, (tm, tm), 1))
    cum = jnp.dot(sel.astype(BF16), jnp.where(before, 1.0, 0.0).astype(BF16), preferred_element_type=F32)
    cum = cum + carry_ref[:, 0:1]
    r1 = jnp.sum(jnp.where(oh1, cum, 0.0), axis=0, keepdims=True).astype(I32)
    r2 = jnp.sum(jnp.where(oh2, cum, 0.0), axis=0, keepdims=True).astype(I32)
    ii_ref[...] = jnp.where(eidx == 0, i1, jnp.where(eidx == 1, i2, jnp.where(eidx == 2, r1,
                            jnp.where(eidx == 3, r2, 0))))
    wf_ref[...] = jnp.where(eidx == 0, w1, jnp.where(eidx == 1, w2, 0.0))
    carry_ref[...] = carry_ref[...] + jnp.sum(sel, axis=1, keepdims=True)
    cnt_ref[...] = carry_ref[...].astype(I32)


def _router(y, g, mod, router_t_bf16):
    return pl.pallas_call(
        _router_kernel,
        grid=(N_TOK // TM,),
        in_specs=[
            pl.BlockSpec((TM, D), lambda i: (i, 0)),
            pl.BlockSpec((1, D), lambda i: (0, 0)),
            pl.BlockSpec((None, 6, D), lambda i: (_cond_of_tile(i, TM), 0, 0)),
            pl.BlockSpec((N_EXP, D), lambda i: (0, 0)),
        ],
        out_specs=[
            pl.BlockSpec((TM, SLAB, LANE), lambda i: (i, 0, 0)),
            pl.BlockSpec((N_EXP, TM), lambda i: (0, i)),
            pl.BlockSpec((N_EXP, TM), lambda i: (0, i)),
            pl.BlockSpec((N_EXP, LANE), lambda i: (0, 0)),
            pl.BlockSpec((N_EXP, LANE), lambda i: (i, 0)),
        ],
        out_shape=[
            jax.ShapeDtypeStruct((N_TOK, SLAB, LANE), F32),
            jax.ShapeDtypeStruct((N_EXP, N_TOK), I32),
            jax.ShapeDtypeStruct((N_EXP, N_TOK), F32),
            jax.ShapeDtypeStruct((N_EXP, LANE), I32),
            jax.ShapeDtypeStruct((N_TOK // TM * N_EXP, LANE), I32),
        ],
        scratch_shapes=[pltpu.VMEM((N_EXP, LANE), F32)],
        compiler_params=_cparams(("arbitrary",)),
        name="router",
    )(y, g.reshape(1, D), mod, router_t_bf16)


def _segment_copies(i, seg_ref, dst_ref, nch_ref, stage_ref, hbm_ref, sem, *, to_hbm):
    def chunk(s_row, d_row):
        st = stage_ref.at[pl.ds(s_row, CH)]
        hb = hbm_ref.at[pl.ds(d_row, CH)]
        return pltpu.make_async_copy(st, hb, sem) if to_hbm else pltpu.make_async_copy(hb, st, sem)

    total = 0
    for e in range(N_EXP):
        k = i * N_EXP + e
        s0, d0, n = seg_ref[k], dst_ref[k], nch_ref[k]

        def start(c, carry, s0=s0, d0=d0):
            chunk(s0 + c * CH, d0 + c * CH).start()
            return carry

        lax.fori_loop(0, n, start, 0)
        total = total + n

    def wait(c, carry):
        chunk(0, 0).wait()
        return carry

    lax.fori_loop(0, total, wait, 0)


def _dispatch_kernel(l1_ref, l2_ref, seg_ref, dst_ref, nch_ref, hs_ref, xs_in_hbm, xs_hbm, stage_ref, sem):
    del xs_in_hbm
    i = pl.program_id(0)

    @pl.when(i == 0)
    def _():
        stage_ref[...] = jnp.zeros_like(stage_ref)

    base = i * TM

    def place(t8, carry):
        for u in range(SUBLANE):
            t = t8 * SUBLANE + u
            row = hs_ref[t]
            stage_ref[l1_ref[base + t]] = row
            stage_ref[l2_ref[base + t]] = row
        return carry

    lax.fori_loop(0, TM // SUBLANE, place, 0)
    _segment_copies(i, seg_ref, dst_ref, nch_ref, stage_ref, xs_hbm, sem, to_hbm=True)


def _dispatch(tabs, hs, xs_zero):
    return pl.pallas_call(
        _dispatch_kernel,
        grid_spec=pltpu.PrefetchScalarGridSpec(
            num_scalar_prefetch=5,
            grid=(N_TOK // TM,),
            in_specs=[
                pl.BlockSpec((TM, SLAB, LANE), lambda i, *_: (i, 0, 0)),
                pl.BlockSpec(memory_space=pl.ANY),
            ],
            out_specs=pl.BlockSpec(memory_space=pl.ANY),
            scratch_shapes=[pltpu.VMEM((STAGE_ROWS, SLAB, LANE), F32), pltpu.SemaphoreType.DMA(())],
        ),
        out_shape=jax.ShapeDtypeStruct(xs_zero.shape, F32),
        input_output_aliases={6: 0},
        compiler_params=_cparams(("arbitrary",)),
        name="moe_dispatch",
    )(*tabs, hs, xs_zero)


def _expert_kernel(te_ref, tv_ref, x_ref, wgu_ref, wd_ref, o_ref, xb_ref, h1_ref):
    del te_ref
    valid = tv_ref[pl.program_id(0)] == 1

    @pl.when(valid)
    def _():
        for k in range(SLAB):
            xb_ref[:, k * LANE:(k + 1) * LANE] = x_ref[:, k, :].astype(BF16)
        acc = _swiglu_rows(xb_ref[...], wgu_ref, wd_ref, h1_ref)
        for k in range(SLAB):
            o_ref[:, k, :] = acc[:, k * LANE:(k + 1) * LANE]

    @pl.when(jnp.logical_not(valid))
    def _():
        o_ref[...] = jnp.zeros_like(o_ref)


def _experts(tile_expert, tile_valid, xs, wgu_bf16, wd_bf16, slot):
    return pl.pallas_call(
        _expert_kernel,
        grid_spec=pltpu.PrefetchScalarGridSpec(
            num_scalar_prefetch=2,
            grid=(MAX_TILES,),
            in_specs=[
                pl.BlockSpec((TM_E, SLAB, LANE), lambda i, te, tv: (i, 0, 0)),
                pl.BlockSpec((None, None, D, 2 * D_FF), lambda i, te, tv: (slot, te[i], 0, 0)),
                pl.BlockSpec((None, None, D_FF, D), lambda i, te, tv: (slot, te[i], 0, 0)),
            ],
            out_specs=pl.BlockSpec((TM_E, SLAB, LANE), lambda i, te, tv: (i, 0, 0)),
            scratch_shapes=[pltpu.VMEM((TM_E, D), BF16), pltpu.VMEM((TM_E, D_FF), BF16)],
        ),
        out_shape=jax.ShapeDtypeStruct(xs.shape, F32),
        compiler_params=_cparams(("arbitrary",)),
        name="moe_experts",
    )(tile_expert, tile_valid, xs, wgu_bf16, wd_bf16)


def _combine_kernel(l1_ref, l2_ref, seg_ref, dst_ref, nch_ref, y_ref, mod_ref, w_ref, ys_hbm, o_ref,
                    stage_ref, g1_ref, g2_ref, sem):
    i = pl.program_id(0)
    _segment_copies(i, seg_ref, dst_ref, nch_ref, stage_ref, ys_hbm, sem, to_hbm=False)
    base = i * TM

    def pick(t8, carry):
        for u in range(SUBLANE):
            t = t8 * SUBLANE + u
            g1_ref[t] = stage_ref[l1_ref[base + t]]
            g2_ref[t] = stage_ref[l2_ref[base + t]]
        return carry

    lax.fori_loop(0, TM // SUBLANE, pick, 0)
    w1 = w_ref[:, 0:1]
    w2 = w_ref[:, 1:2]
    for k in range(SLAB):
        cs = slice(k * LANE, (k + 1) * LANE)
        f = w1 * g1_ref[:, k, :] + w2 * g2_ref[:, k, :]
        o_ref[:, cs] = y_ref[:, cs] + mod_ref[5:6, cs] * f


def _combine(tabs, y, mod, w_cols, ys):
    return pl.pallas_call(
        _combine_kernel,
        grid_spec=pltpu.PrefetchScalarGridSpec(
            num_scalar_prefetch=5,
            grid=(N_TOK // TM,),
            in_specs=[
                pl.BlockSpec((TM, D), lambda i, *_: (i, 0)),
                pl.BlockSpec((None, 6, D), lambda i, *_: (_cond_of_tile(i, TM), 0, 0)),
                pl.BlockSpec((TM, 2), lambda i, *_: (i, 0)),
                pl.BlockSpec(memory_space=pl.ANY),
            ],
            out_specs=pl.BlockSpec((TM, D), lambda i, *_: (i, 0)),
            scratch_shapes=[
                pltpu.VMEM((STAGE_ROWS, SLAB, LANE), F32),
                pltpu.VMEM((TM, SLAB, LANE), F32),
                pltpu.VMEM((TM, SLAB, LANE), F32),
                pltpu.SemaphoreType.DMA(()),
            ],
        ),
        out_shape=jax.ShapeDtypeStruct((N_TOK, D), F32),
        compiler_params=_cparams(("arbitrary",)),
        name="moe_combine",
    )(*tabs, y, mod, w_cols, ys)


def _moe(y, g, mod, router, wgu_bf16, wd_bf16, slot):
    hs, ii, wf, cnt, tbase = _router(y, g, mod, router.T.astype(BF16))
    n_tt = N_TOK // TM
    counts = cnt[:, 0]
    before = tbase[:, 0].reshape(n_tt, N_EXP)
    n_seg = jnp.concatenate([before[1:], counts[None]], axis=0) - before
    region = (counts + CH + TM_E - 1) // TM_E * TM_E
    region_end = jnp.cumsum(region)
    offs = region_end - region
    seg_pad = (n_seg + CH - 1) // CH * CH
    seg = jnp.cumsum(seg_pad, axis=1) - seg_pad
    dst = offs[None, :] + before
    e_ids = jnp.arange(N_EXP, dtype=I32)[:, None]
    shift = jnp.repeat((seg - before).T, TM, axis=1)
    l1 = jnp.sum(jnp.where(ii[0:1] == e_ids, shift, 0), axis=0) + ii[2]
    l2 = jnp.sum(jnp.where(ii[1:2] == e_ids, shift, 0), axis=0) + ii[3]
    tabs = (l1, l2, seg.reshape(-1), dst.reshape(-1), (seg_pad // CH).reshape(-1))
    row0 = jnp.arange(MAX_TILES, dtype=I32) * TM_E
    tile_expert = jnp.minimum(jnp.sum((row0[:, None] >= region_end[None, :]).astype(I32), axis=1), N_EXP - 1)
    tile_valid = (row0 < (offs + counts)[tile_expert]).astype(I32)
    xs = _dispatch(tabs, hs, jnp.zeros((MAX_TILES * TM_E, SLAB, LANE), F32))
    ys = _experts(tile_expert, tile_valid, xs, wgu_bf16, wd_bf16, slot)
    return _combine(tabs, y, mod, wf[0:2].T, ys)


def _final_norm_kernel(y_ref, g_ref, o_ref):
    y = y_ref[...]
    o_ref[...] = y * lax.rsqrt(jnp.mean(y * y, axis=-1, keepdims=True) + 1e-6) * g_ref[...]


def _final_norm(y, g, *, row0, rows):
    rb0 = row0 // TM
    return pl.pallas_call(
        _final_norm_kernel,
        grid=(rows // TM,),
        in_specs=[
            pl.BlockSpec((TM, D), lambda i: (rb0 + i, 0)),
            pl.BlockSpec((1, D), lambda i: (0, 0)),
        ],
        out_specs=pl.BlockSpec((TM, D), lambda i: (i, 0)),
        out_shape=jax.ShapeDtypeStruct((rows, D), F32),
        compiler_params=_cparams(("arbitrary",)),
        name="final_norm",
    )(y, g.reshape(1, D))


def kernel(x_prompt, x_sample, state_rglru, cache_k, cache_v, c, c_ctx, norm_g, ada_w, ada_b,
           gm_w_in, gm_ln_g, gm_w_s, gm_b_s, gm_w_out, rg_w_in, rg_conv_w, rg_conv_b, rg_w_gate,
           rg_b_gate, rg_lambda, rg_w_out, att_w_qkv, att_lambda, att_subln_g, att_w_out,
           ff_w_gu, ff_w_down, moe_router, moe_w_gu, moe_w_down, final_g):
    y = jnp.concatenate([x_prompt.reshape(N_P, D), x_sample.reshape(N_L, D)], axis=0)
    conds = jnp.concatenate([c_ctx[None], c, jnp.zeros((N_COND - 1 - B_L, D), F32)], axis=0)
    mods = _adaln(conds, ada_w, ada_b)
    nb = D_RNN // RNN_BS
    gm_w_in, gm_w_out, rg_w_in, rg_w_out, att_w_qkv, att_w_out, ff_w_gu, ff_w_down, moe_w_gu, moe_w_down = (
        w.astype(BF16) for w in (gm_w_in, gm_w_out, rg_w_in, rg_w_out, att_w_qkv, att_w_out,
                                 ff_w_gu, ff_w_down, moe_w_gu, moe_w_down))
    new_rnn = new_k = new_v = None
    for i in range(DEPTH):
        mod = mods[i]
        kind, slot = i % 3, i // 3
        if kind == 0:
            uv = _mod_matmul(y, norm_g[i, 0], mod, gm_w_in, slot, tn=512, gelu_cols=2 * D)
            a = _gmlp_spatial(uv, gm_ln_g[slot], gm_w_s[slot].astype(BF16), gm_b_s[slot].T)
            y = _matmul_res(a, gm_w_out, slot, y, mod)
        elif kind == 1:
            yx = _mod_matmul(y, norm_g[i, 0], mod, rg_w_in, slot, tn=256, gelu_cols=D_RNN)
            wg_cat = rg_w_gate[slot].transpose(2, 3, 0, 1, 4).reshape(nb, RNN_BS, 4 * RNN_BS).astype(BF16)
            bg_cat = rg_b_gate[slot].reshape(2, 2, nb, RNN_BS).transpose(2, 0, 1, 3).reshape(nb, 1, 4 * RNN_BS)
            rg_args = (yx, rg_conv_w[slot], rg_conv_b[slot], wg_cat, bg_cat, rg_lambda[slot])
            a_p, fin = _rglru(*rg_args, jnp.zeros((2, B_P, D_RNN), F32),
                              row0=0, n_batch=B_P, n_seq=SUBLANE, t_len=T_P)
            a_l, _ = _rglru(*rg_args, state_rglru[:, slot].transpose(1, 0, 2),
                            row0=N_P, n_batch=B_L, n_seq=B_L, t_len=T_L)
            new_rnn = fin.transpose(1, 0, 2)[:, None]
            y = _matmul_res(jnp.concatenate([a_p, a_l], axis=0), rg_w_out, slot, y, mod)
        else:
            lambda_init = 0.8 - 0.6 * math.exp(-0.3 * i)
            qkv = _mod_matmul(y, norm_g[i, 0], mod, att_w_qkv, slot, tn=512, gelu_cols=0)
            att = (qkv, att_lambda[slot], att_subln_g[slot], lambda_init)
            a_p = _attention(*att, row0=0, n_batch=B_P, t_len=T_P, heads=HEADS, tq=T_P)
            a_l = _attention(*att, row0=N_P, n_batch=B_L, t_len=T_L, heads=1, tq=256,
                             ctx_k=cache_k[:, slot].reshape(B_L, PAST, D),
                             ctx_v=cache_v[:, slot].reshape(B_L, PAST, D),
                             rope_tabs=_rope_tables(T_L))
            new_k = qkv[:N_P, D:2 * D].reshape(B_P, 1, T_P, HEADS, 2 * HD)
            new_v = qkv[:N_P, 2 * D:].reshape(B_P, 1, T_P, HEADS, VD)
            y = _matmul_res(jnp.concatenate([a_p, a_l], axis=0), att_w_out, slot, y, mod)
        fslot = i // 2
        if i % 2 == 0:
            y = _ffn(y, norm_g[i, 1], mod, ff_w_gu, ff_w_down, fslot)
        else:
            y = _moe(y, norm_g[i, 1], mod, moe_router[fslot], moe_w_gu, moe_w_down, fslot)
    y_prompt = _final_norm(y, final_g, row0=0, rows=N_P).reshape(B_P, T_P, D)
    y_sample = _final_norm(y, final_g, row0=N_P, rows=N_L).reshape(B_L, T_L, D)
    return (y_prompt, y_sample, new_rnn, new_k, new_v)
```

```python
import functools
import math

import jax
import jax.numpy as jnp
from jax import lax
from jax.experimental import pallas as pl
from jax.experimental.pallas import tpu as pltpu

F32 = jnp.float32
BF16 = jnp.bfloat16
I32 = jnp.int32

D = 1024
DEPTH = 4
B_P, T_P = 32, 256
B_L, T_L = 4, 2048
PAST = 256
N_P = B_P * T_P
N_L = B_L * T_L
N_TOK = N_P + N_L
N_COND = 8
GRID_W = 64
CHUNK = 128
GROUPS = 8
D_RNN = 1280
RNN_BS = 128
CONV_W = 4
RGLRU_C = 8.0
HD = 64
VD = 128
HEADS = 8
ROPE_THETA = 10000.0
LOG2_E = 1.4426950408889634
D_FF = 2816
N_EXP = 8

LANE = 128
SUBLANE = 8
VMEM_LIMIT = 56 * 1024 * 1024
TM = 512
TM_E = 256
CH = 32
MAX_TILES = -(-(2 * N_TOK + N_EXP * (CH + TM_E - 1)) // TM_E)
STAGE_ROWS = 2 * TM + N_EXP * CH
SLAB = D // LANE


def _cparams(sem):
    return pltpu.CompilerParams(dimension_semantics=sem, vmem_limit_bytes=VMEM_LIMIT)


def _cond_of_tile(i, tm):
    r0 = i * tm
    return jnp.where(r0 < N_P, 0, 1 + (r0 - N_P) // T_L)


def _modnorm(x, g, shift, scale):
    ms = jnp.mean(x * x, axis=-1, keepdims=True)
    h = x * lax.rsqrt(ms + 1e-6) * g
    return h * (1.0 + scale) + shift


def _adaln_kernel(c_ref, w_ref, b_ref, o_ref):
    s = jax.nn.silu(c_ref[...]).astype(BF16)
    o_ref[...] = jnp.dot(s, w_ref[...].astype(BF16), preferred_element_type=F32) + b_ref[...]


def _adaln(conds, ada_w, ada_b):
    tn = 1536
    n6 = 6 * D
    out = pl.pallas_call(
        _adaln_kernel,
        grid=(DEPTH, n6 // tn),
        in_specs=[
            pl.BlockSpec((N_COND, D), lambda l, j: (0, 0)),
            pl.BlockSpec((None, D, tn), lambda l, j: (l, 0, j)),
            pl.BlockSpec((None, 1, tn), lambda l, j: (l, 0, j)),
        ],
        out_specs=pl.BlockSpec((None, N_COND, tn), lambda l, j: (l, 0, j)),
        out_shape=jax.ShapeDtypeStruct((DEPTH, N_COND, n6), F32),
        compiler_params=_cparams(("arbitrary", "arbitrary")),
        name="adaln",
    )(conds, ada_w, ada_b.reshape(DEPTH, 1, n6))
    return out.reshape(DEPTH, N_COND, 6, D)


def _mod_matmul_kernel(x_ref, g_ref, mod_ref, w_ref, o_ref, *, tn, gelu_cols):
    h = _modnorm(x_ref[...], g_ref[...], mod_ref[0:1, :], mod_ref[1:2, :]).astype(BF16)
    for c in range(o_ref.shape[1] // tn):
        acc = jnp.dot(h, w_ref[:, c * tn:(c + 1) * tn], preferred_element_type=F32)
        if c * tn < gelu_cols:
            acc = jax.nn.gelu(acc)
        o_ref[:, c * tn:(c + 1) * tn] = acc


def _mod_matmul(y, g, mod, w_bf16, slot, *, tn, gelu_cols):
    nout = w_bf16.shape[2]
    return pl.pallas_call(
        functools.partial(_mod_matmul_kernel, tn=tn, gelu_cols=gelu_cols),
        grid=(N_TOK // TM,),
        in_specs=[
            pl.BlockSpec((TM, D), lambda i: (i, 0)),
            pl.BlockSpec((1, D), lambda i: (0, 0)),
            pl.BlockSpec((None, 6, D), lambda i: (_cond_of_tile(i, TM), 0, 0)),
            pl.BlockSpec((None, D, nout), lambda i: (slot, 0, 0)),
        ],
        out_specs=pl.BlockSpec((TM, nout), lambda i: (i, 0)),
        out_shape=jax.ShapeDtypeStruct((N_TOK, nout), F32),
        compiler_params=_cparams(("arbitrary",)),
        name="mod_matmul",
    )(y, g.reshape(1, D), mod, w_bf16)


def _matmul_res_kernel(*refs, n_prompt_tiles):
    *a_refs, w_ref, y_ref, mod_ref, o_ref = refs

    def run(a_ref):
        acc = jnp.dot(a_ref[...], w_ref[...], preferred_element_type=F32)
        o_ref[...] = y_ref[...] + mod_ref[2:3, :] * acc

    if len(a_refs) == 1:
        run(a_refs[0])
    else:
        pl.when(pl.program_id(0) < n_prompt_tiles)(lambda: run(a_refs[0]))
        pl.when(pl.program_id(0) >= n_prompt_tiles)(lambda: run(a_refs[1]))


def _matmul_res(a_parts, w_bf16, slot, y, mod):
    k = a_parts[0].shape[1]
    npt = N_P // TM
    if len(a_parts) == 1:
        a_specs = [pl.BlockSpec((TM, k), lambda i: (i, 0))]
    else:
        a_specs = [pl.BlockSpec((TM, k), lambda i: (jnp.minimum(i, npt - 1), 0)),
                   pl.BlockSpec((TM, k), lambda i: (jnp.maximum(i - npt, 0), 0))]
    return pl.pallas_call(
        functools.partial(_matmul_res_kernel, n_prompt_tiles=npt),
        grid=(N_TOK // TM,),
        in_specs=a_specs + [
            pl.BlockSpec((None, k, D), lambda i: (slot, 0, 0)),
            pl.BlockSpec((TM, D), lambda i: (i, 0)),
            pl.BlockSpec((None, 6, D), lambda i: (_cond_of_tile(i, TM), 0, 0)),
        ],
        out_specs=pl.BlockSpec((TM, D), lambda i: (i, 0)),
        out_shape=jax.ShapeDtypeStruct((N_TOK, D), F32),
        compiler_params=_cparams(("arbitrary",)),
        name="matmul_res",
    )(*a_parts, w_bf16, y, mod)


def _gmlp_kernel(uv_ref, lng_ref, ws_ref, bs_ref, o_ref):
    rows = uv_ref.shape[0]
    v = uv_ref[:, D:]
    vc = v - jnp.mean(v, axis=-1, keepdims=True)
    vn = vc * lax.rsqrt(jnp.mean(vc * vc, axis=-1, keepdims=True) + 1e-5) * lng_ref[...]
    vn = vn.astype(BF16)
    for ci in range(rows // CHUNK):
        r0 = ci * CHUNK
        for g in range(GROUPS):
            c0 = g * LANE
            s = jnp.dot(ws_ref[g], vn[r0:r0 + CHUNK, c0:c0 + LANE], preferred_element_type=F32)
            s = s + bs_ref[:, g:g + 1]
            o_ref[r0:r0 + CHUNK, c0:c0 + LANE] = (uv_ref[r0:r0 + CHUNK, c0:c0 + LANE] * s).astype(BF16)


def _gmlp_spatial(uv, ln_g, w_s_bf16, b_s_t):
    rows = 2 * CHUNK
    return pl.pallas_call(
        _gmlp_kernel,
        grid=(N_TOK // rows,),
        in_specs=[
            pl.BlockSpec((rows, 2 * D), lambda i: (i, 0)),
            pl.BlockSpec((1, D), lambda i: (0, 0)),
            pl.BlockSpec((GROUPS, CHUNK, CHUNK), lambda i: (0, 0, 0)),
            pl.BlockSpec((CHUNK, GROUPS), lambda i: (0, 0)),
        ],
        out_specs=pl.BlockSpec((rows, D), lambda i: (i, 0)),
        out_shape=jax.ShapeDtypeStruct((N_TOK, D), BF16),
        compiler_params=_cparams(("arbitrary",)),
        name="gmlp_spatial",
    )(uv, ln_g.reshape(1, D), w_s_bf16, b_s_t)


def _rglru_kernel(yb_ref, xb_ref, cw_ref, cb_ref, wg_ref, bg_ref, lam_ref, h0_ref,
                  o_ref, fin_ref, af_ref, bf_ref, ab_ref, bb_ref, *, n_seq, t_len, row_chunk):
    stride = t_len + SUBLANE
    tix = lax.broadcasted_iota(I32, (t_len, 1), 0)
    sp_f = jax.nn.softplus(-lam_ref[0:1, :])
    sp_b = jax.nn.softplus(-lam_ref[1:2, :])
    for g in range(n_seq):
        x = xb_ref[g * t_len:(g + 1) * t_len, :]
        xm1 = jnp.where(tix >= 1, pltpu.roll(x, 1, 0), 0.0)
        xp1 = jnp.where(tix <= t_len - 2, pltpu.roll(x, t_len - 1, 0), 0.0)
        xp2 = jnp.where(tix <= t_len - 3, pltpu.roll(x, t_len - 2, 0), 0.0)
        xc = cb_ref[...] + xm1 * cw_ref[0:1, :]
        xc = xc + x * cw_ref[1:2, :]
        xc = xc + xp1 * cw_ref[2:3, :]
        xc = xc + xp2 * cw_ref[3:4, :]
        for r0 in range(0, t_len, row_chunk):
            xcc = xc[r0:r0 + row_chunk]
            gates = jnp.dot(xcc.astype(BF16), wg_ref[...], preferred_element_type=F32) + bg_ref[...]
            d0 = g * stride + r0
            for d, (sp, a_ref, b_ref) in enumerate(((sp_f, af_ref, bf_ref), (sp_b, ab_ref, bb_ref))):
                r = jax.nn.sigmoid(gates[:, (2 * d) * LANE:(2 * d + 1) * LANE])
                i = jax.nn.sigmoid(gates[:, (2 * d + 1) * LANE:(2 * d + 2) * LANE])
                log_a = (-RGLRU_C * r) * sp
                a = jnp.exp(log_a)
                one_minus_a2 = jnp.tanh(-log_a) * (a * a + 1.0)
                a_ref[d0:d0 + row_chunk, :] = a
                b_ref[d0:d0 + row_chunk, :] = jnp.sqrt(one_minus_a2) * (i * xcc)

    def step(s, carry):
        hf, hb = carry
        tf = pl.ds(s, n_seq, stride=stride)
        tb = pl.ds(t_len - 1 - s, n_seq, stride=stride)
        hf = af_ref[tf, :] * hf + bf_ref[tf, :]
        hb = ab_ref[tb, :] * hb + bb_ref[tb, :]
        bf_ref[tf, :] = hf
        bb_ref[tb, :] = hb
        return hf, hb

    hf, hb = lax.fori_loop(0, t_len, step, (h0_ref[0], h0_ref[1]))
    fin_ref[0] = hf
    fin_ref[1] = hb
    for g in range(n_seq):
        rs = slice(g * stride, g * stride + t_len)
        hsum = bf_ref[rs, :] + bb_ref[rs, :]
        o_ref[g * t_len:(g + 1) * t_len, :] = (hsum * yb_ref[g * t_len:(g + 1) * t_len, :]).astype(BF16)


def _rglru(yx, cw, cb, wg_cat, bg_cat, lam, h0, *, row0, n_batch, n_seq, t_len):
    rows = n_seq * t_len
    nb = D_RNN // RNN_BS
    rb0 = row0 // rows
    kern = functools.partial(_rglru_kernel, n_seq=n_seq, t_len=t_len, row_chunk=min(t_len, 512))
    scan_buf = pltpu.VMEM((n_seq * (t_len + SUBLANE), RNN_BS), F32)
    return pl.pallas_call(
        kern,
        grid=(n_batch // n_seq, nb),
        in_specs=[
            pl.BlockSpec((rows, RNN_BS), lambda s, c: (rb0 + s, c)),
            pl.BlockSpec((rows, RNN_BS), lambda s, c: (rb0 + s, nb + c)),
            pl.BlockSpec((CONV_W, RNN_BS), lambda s, c: (0, c)),
            pl.BlockSpec((1, RNN_BS), lambda s, c: (0, c)),
            pl.BlockSpec((None, RNN_BS, 4 * RNN_BS), lambda s, c: (c, 0, 0)),
            pl.BlockSpec((None, 1, 4 * RNN_BS), lambda s, c: (c, 0, 0)),
            pl.BlockSpec((2, RNN_BS), lambda s, c: (0, c)),
            pl.BlockSpec((2, n_seq, RNN_BS), lambda s, c: (0, s, c)),
        ],
        out_specs=[
            pl.BlockSpec((rows, RNN_BS), lambda s, c: (s, c)),
            pl.BlockSpec((2, n_seq, RNN_BS), lambda s, c: (0, s, c)),
        ],
        out_shape=[
            jax.ShapeDtypeStruct((n_batch * t_len, D_RNN), BF16),
            jax.ShapeDtypeStruct((2, n_batch, D_RNN), F32),
        ],
        scratch_shapes=[scan_buf, scan_buf, scan_buf, scan_buf],
        compiler_params=_cparams(("arbitrary", "arbitrary")),
        name="rglru",
    )(yx, yx, cw, cb.reshape(1, D_RNN), wg_cat, bg_cat, lam, h0)


def _rope(x, cos, sin_signed, first_of_pair):
    partner = jnp.where(first_of_pair, pltpu.roll(x, LANE - 16, 1), pltpu.roll(x, 16, 1))
    return x * cos + partner * sin_signed


def _attn_kernel(*refs, heads, t_len, s_len, rope, lambda_init):
    if rope:
        (lam_ref, g_ref, q_ref, k_ref, v_ref, ck_ref, cv_ref, cq_ref, sq_ref, ckk_ref, skk_ref,
         o_ref, kk_ref, vv_ref) = refs
    else:
        lam_ref, g_ref, q_ref, k_ref, v_ref, o_ref, kk_ref, vv_ref = refs
    tq = q_ref.shape[0]
    lane = lax.broadcasted_iota(I32, (1, LANE), 1)
    first_of_pair = (lane & 16) == 0
    map0 = lane < HD

    @pl.when(pl.program_id(2) == 0)
    def _():
        for j in range(heads):
            cs = slice(j * LANE, (j + 1) * LANE)
            k = k_ref[:, cs]
            if rope:
                k = _rope(k, ckk_ref[...], skk_ref[...], first_of_pair)
            kk_ref[j, 0:t_len, :] = k.astype(BF16)
            vv_ref[j, 0:t_len, 0:LANE] = v_ref[:, cs].astype(BF16)
            if s_len > t_len:
                kk_ref[j, t_len:s_len, :] = ck_ref[:, cs].astype(BF16)
                vv_ref[j, t_len:s_len, 0:LANE] = cv_ref[:, cs].astype(BF16)
            vv_ref[j, :, LANE:2 * LANE] = jnp.ones((s_len, LANE), BF16)

    lp = lam_ref[...]
    lam = (jnp.exp(jnp.sum(lp[0:1] * lp[1:2], axis=-1, keepdims=True))
           - jnp.exp(jnp.sum(lp[2:3] * lp[3:4], axis=-1, keepdims=True)) + lambda_init)
    for j in range(heads):
        cs = slice(j * LANE, (j + 1) * LANE)
        q = q_ref[:, cs]
        if rope:
            q = _rope(q, cq_ref[...], sq_ref[...], first_of_pair)
        q = q * (HD ** -0.5 * LOG2_E)
        q2 = jnp.concatenate([jnp.where(map0, q, 0.0), jnp.where(map0, 0.0, q)], axis=0).astype(BF16)
        s = lax.dot_general(q2, kk_ref[j], (((1,), (1,)), ((), ())), preferred_element_type=F32)
        e = jnp.exp2(s - jnp.max(s, axis=-1, keepdims=True)).astype(BF16)
        nd = jnp.dot(e, vv_ref[j], preferred_element_type=F32)
        av = nd[:, 0:LANE] * (1.0 / nd[:, LANE:LANE + 1])
        o = av[0:tq] - lam * av[tq:2 * tq]
        o = o * lax.rsqrt(jnp.mean(o * o, axis=-1, keepdims=True) + 1e-6) * g_ref[...]
        o_ref[:, cs] = (o * (1.0 - lambda_init)).astype(BF16)


def _attention(qkv, lam_p, subln_g, lambda_init, *, row0, n_batch, t_len, heads, tq,
               ctx_k=None, ctx_v=None, rope_tabs=None):
    rope = rope_tabs is not None
    s_len = t_len + (PAST if rope else 0)
    w = heads * LANE
    nhb = HEADS // heads
    nq = t_len // tq
    rq0 = row0 // tq
    rk0 = row0 // t_len
    kern = functools.partial(_attn_kernel, heads=heads, t_len=t_len, s_len=s_len, rope=rope,
                             lambda_init=lambda_init)
    in_specs = [
        pl.BlockSpec((4, HD), lambda b, h, q: (0, 0)),
        pl.BlockSpec((1, VD), lambda b, h, q: (0, 0)),
        pl.BlockSpec((tq, w), lambda b, h, q: (rq0 + b * nq + q, h)),
        pl.BlockSpec((t_len, w), lambda b, h, q: (rk0 + b, nhb + h)),
        pl.BlockSpec((t_len, w), lambda b, h, q: (rk0 + b, 2 * nhb + h)),
    ]
    args = [lam_p, subln_g.reshape(1, VD), qkv, qkv, qkv]
    if rope:
        cos_t, sin_t = rope_tabs
        in_specs += [
            pl.BlockSpec((None, PAST, w), lambda b, h, q: (b, 0, h)),
            pl.BlockSpec((None, PAST, w), lambda b, h, q: (b, 0, h)),
            pl.BlockSpec((tq, LANE), lambda b, h, q: (q, 0)),
            pl.BlockSpec((tq, LANE), lambda b, h, q: (q, 0)),
            pl.BlockSpec((t_len, LANE), lambda b, h, q: (0, 0)),
            pl.BlockSpec((t_len, LANE), lambda b, h, q: (0, 0)),
        ]
        args += [ctx_k, ctx_v, cos_t, sin_t, cos_t, sin_t]
    return pl.pallas_call(
        kern,
        grid=(n_batch, nhb, nq),
        in_specs=in_specs,
        out_specs=pl.BlockSpec((tq, w), lambda b, h, q: (b * nq + q, h)),
        out_shape=jax.ShapeDtypeStruct((n_batch * t_len, D), BF16),
        scratch_shapes=[pltpu.VMEM((heads, s_len, LANE), BF16), pltpu.VMEM((heads, s_len, 2 * LANE), BF16)],
        compiler_params=_cparams(("arbitrary", "arbitrary", "arbitrary")),
        name="diff_attn",
    )(*args)


def _rope_tables(t_len):
    t = jnp.arange(t_len)
    half = HD // 2
    freqs = ROPE_THETA ** (-jnp.arange(0, half, 2, dtype=F32) / half)
    ang_r = (t // GRID_W).astype(F32)[:, None] * freqs[None]
    ang_c = (t % GRID_W).astype(F32)[:, None] * freqs[None]
    cr, sr, cc, sc = jnp.cos(ang_r), jnp.sin(ang_r), jnp.cos(ang_c), jnp.sin(ang_c)
    cos64 = jnp.concatenate([cr, cr, cc, cc], axis=-1)
    sin64 = jnp.concatenate([-sr, sr, -sc, sc], axis=-1)
    return jnp.concatenate([cos64, cos64], axis=-1), jnp.concatenate([sin64, sin64], axis=-1)


def _swiglu_rows(h_bf16, wgu_ref, wd_ref, h1_ref):
    tn = 256
    for c in range(D_FF // tn):
        gg = jnp.dot(h_bf16, wgu_ref[:, c * tn:(c + 1) * tn], preferred_element_type=F32)
        uu = jnp.dot(h_bf16, wgu_ref[:, D_FF + c * tn:D_FF + (c + 1) * tn], preferred_element_type=F32)
        h1_ref[:, c * tn:(c + 1) * tn] = (jax.nn.silu(gg) * uu).astype(BF16)
    return jnp.dot(h1_ref[...], wd_ref[...], preferred_element_type=F32)


def _ffn_kernel(y_ref, g_ref, mod_ref, wgu_ref, wd_ref, o_ref, h1_ref):
    y = y_ref[...]
    h = _modnorm(y, g_ref[...], mod_ref[3:4, :], mod_ref[4:5, :]).astype(BF16)
    o_ref[...] = y + mod_ref[5:6, :] * _swiglu_rows(h, wgu_ref, wd_ref, h1_ref)


def _ffn(y, g, mod, wgu_bf16, wd_bf16, slot):
    return pl.pallas_call(
        _ffn_kernel,
        grid=(N_TOK // TM,),
        in_specs=[
            pl.BlockSpec((TM, D), lambda i: (i, 0)),
            pl.BlockSpec((1, D), lambda i: (0, 0)),
            pl.BlockSpec((None, 6, D), lambda i: (_cond_of_tile(i, TM), 0, 0)),
            pl.BlockSpec((None, D, 2 * D_FF), lambda i: (slot, 0, 0), pipeline_mode=pl.Buffered(1)),
            pl.BlockSpec((None, D_FF, D), lambda i: (slot, 0, 0), pipeline_mode=pl.Buffered(1)),
        ],
        out_specs=pl.BlockSpec((TM, D), lambda i: (i, 0)),
        out_shape=jax.ShapeDtypeStruct((N_TOK, D), F32),
        scratch_shapes=[pltpu.VMEM((TM, D_FF), BF16)],
        compiler_params=_cparams(("arbitrary",)),
        name="ffn",
    )(y, g.reshape(1, D), mod, wgu_bf16, wd_bf16)


def _router_kernel(y_ref, g_ref, mod_ref, rt_ref, hs_ref, ii_ref, wf_ref, cnt_ref, tb_ref, carry_ref):
    tm = y_ref.shape[0]

    @pl.when(pl.program_id(0) == 0)
    def _():
        carry_ref[...] = jnp.zeros_like(carry_ref)

    tb_ref[...] = carry_ref[...].astype(I32)

    h = _modnorm(y_ref[...], g_ref[...], mod_ref[3:4, :], mod_ref[4:5, :])
    for k in range(SLAB):
        hs_ref[:, k, :] = h[:, k * LANE:(k + 1) * LANE]
    logits = lax.dot_general(rt_ref[...], h.astype(BF16), (((1,), (1,)), ((), ())),
                             preferred_element_type=F32)
    eidx = lax.broadcasted_iota(I32, (N_EXP, tm), 0)
    eidx_f = eidx.astype(F32)
    m1 = jnp.max(logits, axis=0, keepdims=True)
    i1 = jnp.min(jnp.where(logits == m1, eidx_f, float(N_EXP)), axis=0, keepdims=True)
    oh1 = eidx_f == i1
    rest = jnp.where(oh1, -jnp.inf, logits)
    m2 = jnp.max(rest, axis=0, keepdims=True)
    i2 = jnp.min(jnp.where(rest == m2, eidx_f, float(N_EXP)), axis=0, keepdims=True)
    oh2 = eidx_f == i2
    i1 = i1.astype(I32)
    i2 = i2.astype(I32)
    e2 = jnp.exp(m2 - m1)
    w1 = 1.0 / (1.0 + e2)
    w2 = e2 / (1.0 + e2)
    sel = jnp.where(oh1 | oh2, 1.0, 0.0)
    before = (lax.broadcasted_iota(I32, (tm, tm), 0) < lax.broadcasted_iota(I32, (tm, tm), 1))
    cum = jnp.dot(sel.astype(BF16), jnp.where(before, 1.0, 0.0).astype(BF16), preferred_element_type=F32)
    cum = cum + carry_ref[:, 0:1]
    r1 = jnp.sum(jnp.where(oh1, cum, 0.0), axis=0, keepdims=True).astype(I32)
    r2 = jnp.sum(jnp.where(oh2, cum, 0.0), axis=0, keepdims=True).astype(I32)
    ii_ref[...] = jnp.where(eidx == 0, i1, jnp.where(eidx == 1, i2, jnp.where(eidx == 2, r1,
                            jnp.where(eidx == 3, r2, 0))))
    wf_ref[...] = jnp.where(eidx == 0, w1, jnp.where(eidx == 1, w2, 0.0))
    carry_ref[...] = carry_ref[...] + jnp.sum(sel, axis=1, keepdims=True)
    cnt_ref[...] = carry_ref[...].astype(I32)


def _router(y, g, mod, router_t_bf16):
    return pl.pallas_call(
        _router_kernel,
        grid=(N_TOK // TM,),
        in_specs=[
            pl.BlockSpec((TM, D), lambda i: (i, 0)),
            pl.BlockSpec((1, D), lambda i: (0, 0)),
            pl.BlockSpec((None, 6, D), lambda i: (_cond_of_tile(i, TM), 0, 0)),
            pl.BlockSpec((N_EXP, D), lambda i: (0, 0)),
        ],
        out_specs=[
            pl.BlockSpec((TM, SLAB, LANE), lambda i: (i, 0, 0)),
            pl.BlockSpec((N_EXP, TM), lambda i: (0, i)),
            pl.BlockSpec((N_EXP, TM), lambda i: (0, i)),
            pl.BlockSpec((N_EXP, LANE), lambda i: (0, 0)),
            pl.BlockSpec((N_EXP, LANE), lambda i: (i, 0)),
        ],
        out_shape=[
            jax.ShapeDtypeStruct((N_TOK, SLAB, LANE), F32),
            jax.ShapeDtypeStruct((N_EXP, N_TOK), I32),
            jax.ShapeDtypeStruct((N_EXP, N_TOK), F32),
            jax.ShapeDtypeStruct((N_EXP, LANE), I32),
            jax.ShapeDtypeStruct((N_TOK // TM * N_EXP, LANE), I32),
        ],
        scratch_shapes=[pltpu.VMEM((N_EXP, LANE), F32)],
        compiler_params=_cparams(("arbitrary",)),
        name="router",
    )(y, g.reshape(1, D), mod, router_t_bf16)


def _segment_copies(i, seg_ref, dst_ref, nch_ref, stage_ref, hbm_ref, sem, *, to_hbm):
    def chunk(s_row, d_row):
        st = stage_ref.at[pl.ds(s_row, CH)]
        hb = hbm_ref.at[pl.ds(d_row, CH)]
        return pltpu.make_async_copy(st, hb, sem) if to_hbm else pltpu.make_async_copy(hb, st, sem)

    total = 0
    for e in range(N_EXP):
        k = i * N_EXP + e
        s0, d0, n = seg_ref[k], dst_ref[k], nch_ref[k]

        def start(c, carry, s0=s0, d0=d0):
            chunk(s0 + c * CH, d0 + c * CH).start()
            return carry

        lax.fori_loop(0, n, start, 0)
        total = total + n

    def wait(c, carry):
        chunk(0, 0).wait()
        return carry

    lax.fori_loop(0, total, wait, 0)


def _dispatch_kernel(l1_ref, l2_ref, seg_ref, dst_ref, nch_ref, hs_ref, xs_in_hbm, xs_hbm, stage_ref, sem):
    del xs_in_hbm
    i = pl.program_id(0)

    @pl.when(i == 0)
    def _():
        stage_ref[...] = jnp.zeros_like(stage_ref)

    base = i * TM

    def place(t8, carry):
        for u in range(SUBLANE):
            t = t8 * SUBLANE + u
            row = hs_ref[t]
            stage_ref[l1_ref[base + t]] = row
            stage_ref[l2_ref[base + t]] = row
        return carry

    lax.fori_loop(0, TM // SUBLANE, place, 0)
    _segment_copies(i, seg_ref, dst_ref, nch_ref, stage_ref, xs_hbm, sem, to_hbm=True)


def _dispatch(tabs, hs, xs_zero):
    return pl.pallas_call(
        _dispatch_kernel,
        grid_spec=pltpu.PrefetchScalarGridSpec(
            num_scalar_prefetch=5,
            grid=(N_TOK // TM,),
            in_specs=[
                pl.BlockSpec((TM, SLAB, LANE), lambda i, *_: (i, 0, 0)),
                pl.BlockSpec(memory_space=pl.ANY),
            ],
            out_specs=pl.BlockSpec(memory_space=pl.ANY),
            scratch_shapes=[pltpu.VMEM((STAGE_ROWS, SLAB, LANE), F32), pltpu.SemaphoreType.DMA(())],
        ),
        out_shape=jax.ShapeDtypeStruct(xs_zero.shape, F32),
        input_output_aliases={6: 0},
        compiler_params=_cparams(("arbitrary",)),
        name="moe_dispatch",
    )(*tabs, hs, xs_zero)


def _expert_kernel(te_ref, tv_ref, x_ref, wgu_ref, wd_ref, o_ref, xb_ref, h1_ref):
    del te_ref
    valid = tv_ref[pl.program_id(0)] == 1

    @pl.when(valid)
    def _():
        for k in range(SLAB):
            xb_ref[:, k * LANE:(k + 1) * LANE] = x_ref[pl.ds(k, TM_E, stride=SLAB), :].astype(BF16)
        acc = _swiglu_rows(xb_ref[...], wgu_ref, wd_ref, h1_ref)
        for k in range(SLAB):
            o_ref[:, k, :] = acc[:, k * LANE:(k + 1) * LANE]

    @pl.when(jnp.logical_not(valid))
    def _():
        o_ref[...] = jnp.zeros_like(o_ref)


def _experts(tile_expert, tile_valid, xs, wgu_bf16, wd_bf16, slot):
    return pl.pallas_call(
        _expert_kernel,
        grid_spec=pltpu.PrefetchScalarGridSpec(
            num_scalar_prefetch=2,
            grid=(MAX_TILES,),
            in_specs=[
                pl.BlockSpec((TM_E * SLAB, LANE), lambda i, te, tv: (i, 0)),
                pl.BlockSpec((None, None, D, 2 * D_FF), lambda i, te, tv: (slot, te[i], 0, 0)),
                pl.BlockSpec((None, None, D_FF, D), lambda i, te, tv: (slot, te[i], 0, 0)),
            ],
            out_specs=pl.BlockSpec((TM_E, SLAB, LANE), lambda i, te, tv: (i, 0, 0)),
            scratch_shapes=[pltpu.VMEM((TM_E, D), BF16), pltpu.VMEM((TM_E, D_FF), BF16)],
        ),
        out_shape=jax.ShapeDtypeStruct(xs.shape, F32),
        compiler_params=_cparams(("arbitrary",)),
        name="moe_experts",
    )(tile_expert, tile_valid, xs.reshape(-1, LANE), wgu_bf16, wd_bf16)


def _combine_kernel(l1_ref, l2_ref, seg_ref, dst_ref, nch_ref, y_ref, mod_ref, w_ref, ys_hbm, o_ref,
                    stage_ref, g1_ref, g2_ref, sem):
    i = pl.program_id(0)
    _segment_copies(i, seg_ref, dst_ref, nch_ref, stage_ref, ys_hbm, sem, to_hbm=False)
    base = i * TM

    def pick(t8, carry):
        for u in range(SUBLANE):
            t = t8 * SUBLANE + u
            rows = pl.ds(pl.multiple_of(t * SLAB, SLAB), SLAB)
            g1_ref[rows, :] = stage_ref[l1_ref[base + t]]
            g2_ref[rows, :] = stage_ref[l2_ref[base + t]]
        return carry

    lax.fori_loop(0, TM // SUBLANE, pick, 0)
    w1 = w_ref[:, 0:1]
    w2 = w_ref[:, 1:2]
    for k in range(SLAB):
        cs = slice(k * LANE, (k + 1) * LANE)
        chunk_k = pl.ds(k, TM, stride=SLAB)
        f = w1 * g1_ref[chunk_k, :] + w2 * g2_ref[chunk_k, :]
        o_ref[:, cs] = y_ref[:, cs] + mod_ref[5:6, cs] * f


def _combine(tabs, y, mod, w_cols, ys):
    return pl.pallas_call(
        _combine_kernel,
        grid_spec=pltpu.PrefetchScalarGridSpec(
            num_scalar_prefetch=5,
            grid=(N_TOK // TM,),
            in_specs=[
                pl.BlockSpec((TM, D), lambda i, *_: (i, 0)),
                pl.BlockSpec((None, 6, D), lambda i, *_: (_cond_of_tile(i, TM), 0, 0)),
                pl.BlockSpec((TM, 2), lambda i, *_: (i, 0)),
                pl.BlockSpec(memory_space=pl.ANY),
            ],
            out_specs=pl.BlockSpec((TM, D), lambda i, *_: (i, 0)),
            scratch_shapes=[
                pltpu.VMEM((STAGE_ROWS, SLAB, LANE), F32),
                pltpu.VMEM((TM * SLAB, LANE), F32),
                pltpu.VMEM((TM * SLAB, LANE), F32),
                pltpu.SemaphoreType.DMA(()),
            ],
        ),
        out_shape=jax.ShapeDtypeStruct((N_TOK, D), F32),
        compiler_params=_cparams(("arbitrary",)),
        name="moe_combine",
    )(*tabs, y, mod, w_cols, ys)


def _moe(y, g, mod, router, wgu_bf16, wd_bf16, slot):
    hs, ii, wf, cnt, tbase = _router(y, g, mod, router.T.astype(BF16))
    n_tt = N_TOK // TM
    counts = cnt[:, 0]
    before = tbase[:, 0].reshape(n_tt, N_EXP)
    n_seg = jnp.concatenate([before[1:], counts[None]], axis=0) - before
    region = (counts + CH + TM_E - 1) // TM_E * TM_E
    region_end = jnp.cumsum(region)
    offs = region_end - region
    seg_pad = (n_seg + CH - 1) // CH * CH
    seg = jnp.cumsum(seg_pad, axis=1) - seg_pad
    dst = offs[None, :] + before
    e_ids = jnp.arange(N_EXP, dtype=I32)[:, None]
    shift = jnp.repeat((seg - before).T, TM, axis=1)
    l1 = jnp.sum(jnp.where(ii[0:1] == e_ids, shift, 0), axis=0) + ii[2]
    l2 = jnp.sum(jnp.where(ii[1:2] == e_ids, shift, 0), axis=0) + ii[3]
    tabs = (l1, l2, seg.reshape(-1), dst.reshape(-1), (seg_pad // CH).reshape(-1))
    row0 = jnp.arange(MAX_TILES, dtype=I32) * TM_E
    tile_expert = jnp.minimum(jnp.sum((row0[:, None] >= region_end[None, :]).astype(I32), axis=1), N_EXP - 1)
    tile_valid = (row0 < (offs + counts)[tile_expert]).astype(I32)
    xs = _dispatch(tabs, hs, jnp.zeros((MAX_TILES * TM_E, SLAB, LANE), F32))
    ys = _experts(tile_expert, tile_valid, xs, wgu_bf16, wd_bf16, slot)
    return _combine(tabs, y, mod, wf[0:2].T, ys)


def _final_norm_kernel(y_ref, g_ref, o_ref):
    y = y_ref[...]
    o_ref[...] = y * lax.rsqrt(jnp.mean(y * y, axis=-1, keepdims=True) + 1e-6) * g_ref[...]


def _final_norm(y, g, *, row0, rows):
    rb0 = row0 // TM
    return pl.pallas_call(
        _final_norm_kernel,
        grid=(rows // TM,),
        in_specs=[
            pl.BlockSpec((TM, D), lambda i: (rb0 + i, 0)),
            pl.BlockSpec((1, D), lambda i: (0, 0)),
        ],
        out_specs=pl.BlockSpec((TM, D), lambda i: (i, 0)),
        out_shape=jax.ShapeDtypeStruct((rows, D), F32),
        compiler_params=_cparams(("arbitrary",)),
        name="final_norm",
    )(y, g.reshape(1, D))


def kernel(x_prompt, x_sample, state_rglru, cache_k, cache_v, c, c_ctx, norm_g, ada_w, ada_b,
           gm_w_in, gm_ln_g, gm_w_s, gm_b_s, gm_w_out, rg_w_in, rg_conv_w, rg_conv_b, rg_w_gate,
           rg_b_gate, rg_lambda, rg_w_out, att_w_qkv, att_lambda, att_subln_g, att_w_out,
           ff_w_gu, ff_w_down, moe_router, moe_w_gu, moe_w_down, final_g):
    y = jnp.concatenate([x_prompt.reshape(N_P, D), x_sample.reshape(N_L, D)], axis=0)
    conds = jnp.concatenate([c_ctx[None], c, jnp.zeros((N_COND - 1 - B_L, D), F32)], axis=0)
    mods = _adaln(conds, ada_w, ada_b)
    nb = D_RNN // RNN_BS
    gm_w_in, gm_w_out, rg_w_in, rg_w_out, att_w_qkv, att_w_out, ff_w_gu, ff_w_down, moe_w_gu, moe_w_down = (
        w.astype(BF16) for w in (gm_w_in, gm_w_out, rg_w_in, rg_w_out, att_w_qkv, att_w_out,
                                 ff_w_gu, ff_w_down, moe_w_gu, moe_w_down))
    new_rnn = new_k = new_v = None
    for i in range(DEPTH):
        mod = mods[i]
        kind, slot = i % 3, i // 3
        if kind == 0:
            uv = _mod_matmul(y, norm_g[i, 0], mod, gm_w_in, slot, tn=512, gelu_cols=2 * D)
            a = _gmlp_spatial(uv, gm_ln_g[slot], gm_w_s[slot].astype(BF16), gm_b_s[slot].T)
            y = _matmul_res((a,), gm_w_out, slot, y, mod)
        elif kind == 1:
            yx = _mod_matmul(y, norm_g[i, 0], mod, rg_w_in, slot, tn=256, gelu_cols=D_RNN)
            wg_cat = rg_w_gate[slot].transpose(2, 3, 0, 1, 4).reshape(nb, RNN_BS, 4 * RNN_BS).astype(BF16)
            bg_cat = rg_b_gate[slot].reshape(2, 2, nb, RNN_BS).transpose(2, 0, 1, 3).reshape(nb, 1, 4 * RNN_BS)
            rg_args = (yx, rg_conv_w[slot], rg_conv_b[slot], wg_cat, bg_cat, rg_lambda[slot])
            a_p, fin = _rglru(*rg_args, jnp.zeros((2, B_P, D_RNN), F32),
                              row0=0, n_batch=B_P, n_seq=SUBLANE, t_len=T_P)
            a_l, _ = _rglru(*rg_args, state_rglru[:, slot].transpose(1, 0, 2),
                            row0=N_P, n_batch=B_L, n_seq=B_L, t_len=T_L)
            new_rnn = fin.transpose(1, 0, 2)[:, None]
            y = _matmul_res((a_p, a_l), rg_w_out, slot, y, mod)
        else:
            lambda_init = 0.8 - 0.6 * math.exp(-0.3 * i)
            qkv = _mod_matmul(y, norm_g[i, 0], mod, att_w_qkv, slot, tn=512, gelu_cols=0)
            att = (qkv, att_lambda[slot], att_subln_g[slot], lambda_init)
            a_p = _attention(*att, row0=0, n_batch=B_P, t_len=T_P, heads=HEADS, tq=T_P)
            a_l = _attention(*att, row0=N_P, n_batch=B_L, t_len=T_L, heads=4, tq=256,
                             ctx_k=cache_k[:, slot].reshape(B_L, PAST, D),
                             ctx_v=cache_v[:, slot].reshape(B_L, PAST, D),
                             rope_tabs=_rope_tables(T_L))
            new_k = qkv[:N_P, D:2 * D].reshape(B_P, 1, T_P, HEADS, 2 * HD)
            new_v = qkv[:N_P, 2 * D:].reshape(B_P, 1, T_P, HEADS, VD)
            y = _matmul_res((a_p, a_l), att_w_out, slot, y, mod)
        fslot = i // 2
        if i % 2 == 0:
            y = _ffn(y, norm_g[i, 1], mod, ff_w_gu, ff_w_down, fslot)
        else:
            y = _moe(y, norm_g[i, 1], mod, moe_router[fslot], moe_w_gu, moe_w_down, fslot)
    y_prompt = _final_norm(y, final_g, row0=0, rows=N_P).reshape(B_P, T_P, D)
    y_sample = _final_norm(y, final_g, row0=N_P, rows=N_L).reshape(B_L, T_L, D)
    return (y_prompt, y_sample, new_rnn, new_k, new_v)
```

```python
import functools
import math

import jax
import jax.numpy as jnp
from jax import lax
from jax.experimental import pallas as pl
from jax.experimental.pallas import tpu as pltpu

F32 = jnp.float32
BF16 = jnp.bfloat16
I32 = jnp.int32

D = 1024
DEPTH = 4
B_P, T_P = 32, 256
B_L, T_L = 4, 2048
PAST = 256
N_P = B_P * T_P
N_L = B_L * T_L
N_TOK = N_P + N_L
N_COND = 8
GRID_W = 64
CHUNK = 128
GROUPS = 8
D_RNN = 1280
RNN_BS = 128
CONV_W = 4
RGLRU_C = 8.0
HD = 64
VD = 128
HEADS = 8
ROPE_THETA = 10000.0
LOG2_E = 1.4426950408889634
D_FF = 2816
N_EXP = 8

LANE = 128
SUBLANE = 8
VMEM_LIMIT = 56 * 1024 * 1024
TM = 512
TM_E = 256
CH = 32
MAX_TILES = -(-(2 * N_TOK + N_EXP * (CH + TM_E - 1)) // TM_E)
STAGE_ROWS = 2 * TM + N_EXP * CH
SLAB = D // LANE


def _cparams(sem):
    return pltpu.CompilerParams(dimension_semantics=sem, vmem_limit_bytes=VMEM_LIMIT)


def _cond_of_tile(i, tm):
    r0 = i * tm
    return jnp.where(r0 < N_P, 0, 1 + (r0 - N_P) // T_L)


def _modnorm(x, g, shift, scale):
    ms = jnp.mean(x * x, axis=-1, keepdims=True)
    h = x * lax.rsqrt(ms + 1e-6) * g
    return h * (1.0 + scale) + shift


def _adaln_kernel(c_ref, w_ref, b_ref, o_ref):
    s = jax.nn.silu(c_ref[...]).astype(BF16)
    o_ref[...] = jnp.dot(s, w_ref[...].astype(BF16), preferred_element_type=F32) + b_ref[...]


def _adaln(conds, ada_w, ada_b):
    tn = 1536
    n6 = 6 * D
    out = pl.pallas_call(
        _adaln_kernel,
        grid=(DEPTH, n6 // tn),
        in_specs=[
            pl.BlockSpec((N_COND, D), lambda l, j: (0, 0)),
            pl.BlockSpec((None, D, tn), lambda l, j: (l, 0, j)),
            pl.BlockSpec((None, 1, tn), lambda l, j: (l, 0, j)),
        ],
        out_specs=pl.BlockSpec((None, N_COND, tn), lambda l, j: (l, 0, j)),
        out_shape=jax.ShapeDtypeStruct((DEPTH, N_COND, n6), F32),
        compiler_params=_cparams(("arbitrary", "arbitrary")),
        name="adaln",
    )(conds, ada_w, ada_b.reshape(DEPTH, 1, n6))
    return out.reshape(DEPTH, N_COND, 6, D)


def _mod_matmul_kernel(x_ref, g_ref, mod_ref, w_ref, o_ref, *, tn, gelu_cols):
    h = _modnorm(x_ref[...], g_ref[...], mod_ref[0:1, :], mod_ref[1:2, :]).astype(BF16)
    for c in range(o_ref.shape[1] // tn):
        acc = jnp.dot(h, w_ref[:, c * tn:(c + 1) * tn], preferred_element_type=F32)
        if c * tn < gelu_cols:
            acc = jax.nn.gelu(acc)
        o_ref[:, c * tn:(c + 1) * tn] = acc


def _mod_matmul(y, g, mod, w_bf16, slot, *, tn, gelu_cols):
    nout = w_bf16.shape[2]
    return pl.pallas_call(
        functools.partial(_mod_matmul_kernel, tn=tn, gelu_cols=gelu_cols),
        grid=(N_TOK // TM,),
        in_specs=[
            pl.BlockSpec((TM, D), lambda i: (i, 0)),
            pl.BlockSpec((1, D), lambda i: (0, 0)),
            pl.BlockSpec((None, 6, D), lambda i: (_cond_of_tile(i, TM), 0, 0)),
            pl.BlockSpec((None, D, nout), lambda i: (slot, 0, 0)),
        ],
        out_specs=pl.BlockSpec((TM, nout), lambda i: (i, 0)),
        out_shape=jax.ShapeDtypeStruct((N_TOK, nout), F32),
        compiler_params=_cparams(("arbitrary",)),
        name="mod_matmul",
    )(y, g.reshape(1, D), mod, w_bf16)


N_PROMPT_TILES = N_P // TM


def _mixer_out_specs(a_parts, w_out_bf16, slot):
    k = a_parts[0].shape[1]
    return [
        pl.BlockSpec((TM, k), lambda i, *_: (jnp.minimum(i, N_PROMPT_TILES - 1), 0)),
        pl.BlockSpec((TM, k), lambda i, *_: (jnp.maximum(i - N_PROMPT_TILES, 0), 0)),
        pl.BlockSpec((None, k, D), lambda i, *_: (slot, 0, 0)),
    ]


def _for_stream_of_tile(body):
    pl.when(pl.program_id(0) < N_PROMPT_TILES)(lambda: body(0))
    pl.when(pl.program_id(0) >= N_PROMPT_TILES)(lambda: body(1))


def _gmlp_kernel(y_ref, g_ref, mod_ref, win_ref, lng_ref, ws_ref, bs_ref, wout_ref, o_ref, uv_ref, a_ref):
    y = y_ref[...]
    h = _modnorm(y, g_ref[...], mod_ref[0:1, :], mod_ref[1:2, :]).astype(BF16)
    tn = 512
    for c in range(2 * D // tn):
        uv = jnp.dot(h, win_ref[:, c * tn:(c + 1) * tn], preferred_element_type=F32)
        uv_ref[:, c * tn:(c + 1) * tn] = jax.nn.gelu(uv)
    v = uv_ref[:, D:]
    vc = v - jnp.mean(v, axis=-1, keepdims=True)
    vn = vc * lax.rsqrt(jnp.mean(vc * vc, axis=-1, keepdims=True) + 1e-5) * lng_ref[...]
    vn = vn.astype(BF16)
    for ci in range(TM // CHUNK):
        r0 = ci * CHUNK
        for g in range(GROUPS):
            c0 = g * LANE
            s = jnp.dot(ws_ref[g], vn[r0:r0 + CHUNK, c0:c0 + LANE], preferred_element_type=F32)
            s = s + bs_ref[:, g:g + 1]
            a_ref[r0:r0 + CHUNK, c0:c0 + LANE] = (uv_ref[r0:r0 + CHUNK, c0:c0 + LANE] * s).astype(BF16)
    o_ref[...] = y + mod_ref[2:3, :] * jnp.dot(a_ref[...], wout_ref[...], preferred_element_type=F32)


def _gmlp_mixer(y, g, mod, w_in_bf16, ln_g, w_s_bf16, b_s_t, w_out_bf16, slot):
    return pl.pallas_call(
        _gmlp_kernel,
        grid=(N_TOK // TM,),
        in_specs=[
            pl.BlockSpec((TM, D), lambda i: (i, 0)),
            pl.BlockSpec((1, D), lambda i: (0, 0)),
            pl.BlockSpec((None, 6, D), lambda i: (_cond_of_tile(i, TM), 0, 0)),
            pl.BlockSpec((None, D, 2 * D), lambda i: (slot, 0, 0)),
            pl.BlockSpec((1, D), lambda i: (0, 0)),
            pl.BlockSpec((GROUPS, CHUNK, CHUNK), lambda i: (0, 0, 0)),
            pl.BlockSpec((CHUNK, GROUPS), lambda i: (0, 0)),
            pl.BlockSpec((None, D, D), lambda i: (slot, 0, 0)),
        ],
        out_specs=pl.BlockSpec((TM, D), lambda i: (i, 0)),
        out_shape=jax.ShapeDtypeStruct((N_TOK, D), F32),
        scratch_shapes=[pltpu.VMEM((TM, 2 * D), F32), pltpu.VMEM((TM, D), BF16)],
        compiler_params=_cparams(("arbitrary",)),
        name="gmlp_mixer",
    )(y, g.reshape(1, D), mod, w_in_bf16, ln_g.reshape(1, D), w_s_bf16, b_s_t, w_out_bf16)


SCAN_UNROLL = 8


def _rglru_kernel(yb_ref, xb_ref, cw_ref, cb_ref, wg_ref, bg_ref, lam_ref, h0_ref,
                  o_ref, fin_ref, af_ref, bf_ref, ab_ref, bb_ref, *, n_seq, t_len, row_chunk):
    stride = t_len + SUBLANE
    tix = lax.broadcasted_iota(I32, (t_len, 1), 0)
    sp_f = jax.nn.softplus(-lam_ref[0:1, :])
    sp_b = jax.nn.softplus(-lam_ref[1:2, :])
    for g in range(n_seq):
        x = xb_ref[g * t_len:(g + 1) * t_len, :]
        xm1 = jnp.where(tix >= 1, pltpu.roll(x, 1, 0), 0.0)
        xp1 = jnp.where(tix <= t_len - 2, pltpu.roll(x, t_len - 1, 0), 0.0)
        xp2 = jnp.where(tix <= t_len - 3, pltpu.roll(x, t_len - 2, 0), 0.0)
        xc = cb_ref[...] + xm1 * cw_ref[0:1, :]
        xc = xc + x * cw_ref[1:2, :]
        xc = xc + xp1 * cw_ref[2:3, :]
        xc = xc + xp2 * cw_ref[3:4, :]
        for r0 in range(0, t_len, row_chunk):
            xcc = xc[r0:r0 + row_chunk]
            gates = jnp.dot(xcc.astype(BF16), wg_ref[...], preferred_element_type=F32) + bg_ref[...]
            d0 = g * stride + r0
            for d, (sp, a_ref, b_ref) in enumerate(((sp_f, af_ref, bf_ref), (sp_b, ab_ref, bb_ref))):
                r = jax.nn.sigmoid(gates[:, (2 * d) * LANE:(2 * d + 1) * LANE])
                i = jax.nn.sigmoid(gates[:, (2 * d + 1) * LANE:(2 * d + 2) * LANE])
                log_a = (-RGLRU_C * r) * sp
                a = jnp.exp(log_a)
                one_minus_a2 = jnp.tanh(-log_a) * (a * a + 1.0)
                a_ref[d0:d0 + row_chunk, :] = a
                b_ref[d0:d0 + row_chunk, :] = jnp.sqrt(one_minus_a2) * (i * xcc)

    def steps(s8, carry):
        hf, hb = carry
        for u in range(SCAN_UNROLL):
            s = s8 * SCAN_UNROLL + u
            tf = pl.ds(s, n_seq, stride=stride)
            tb = pl.ds(t_len - 1 - s, n_seq, stride=stride)
            hf = af_ref[tf, :] * hf + bf_ref[tf, :]
            hb = ab_ref[tb, :] * hb + bb_ref[tb, :]
            bf_ref[tf, :] = hf
            bb_ref[tb, :] = hb
        return hf, hb

    hf, hb = lax.fori_loop(0, t_len // SCAN_UNROLL, steps, (h0_ref[0], h0_ref[1]))
    fin_ref[0] = hf
    fin_ref[1] = hb
    for g in range(n_seq):
        rs = slice(g * stride, g * stride + t_len)
        hsum = bf_ref[rs, :] + bb_ref[rs, :]
        o_ref[g * t_len:(g + 1) * t_len, :] = (hsum * yb_ref[g * t_len:(g + 1) * t_len, :]).astype(BF16)


def _rglru(yx, cw, cb, wg_cat, bg_cat, lam, h0, *, row0, n_batch, n_seq, t_len):
    rows = n_seq * t_len
    nb = D_RNN // RNN_BS
    rb0 = row0 // rows
    kern = functools.partial(_rglru_kernel, n_seq=n_seq, t_len=t_len, row_chunk=min(t_len, 512))
    scan_buf = pltpu.VMEM((n_seq * (t_len + SUBLANE), RNN_BS), F32)
    return pl.pallas_call(
        kern,
        grid=(n_batch // n_seq, nb),
        in_specs=[
            pl.BlockSpec((rows, RNN_BS), lambda s, c: (rb0 + s, c)),
            pl.BlockSpec((rows, RNN_BS), lambda s, c: (rb0 + s, nb + c)),
            pl.BlockSpec((CONV_W, RNN_BS), lambda s, c: (0, c)),
            pl.BlockSpec((1, RNN_BS), lambda s, c: (0, c)),
            pl.BlockSpec((None, RNN_BS, 4 * RNN_BS), lambda s, c: (c, 0, 0)),
            pl.BlockSpec((None, 1, 4 * RNN_BS), lambda s, c: (c, 0, 0)),
            pl.BlockSpec((2, RNN_BS), lambda s, c: (0, c)),
            pl.BlockSpec((2, n_seq, RNN_BS), lambda s, c: (0, s, c)),
        ],
        out_specs=[
            pl.BlockSpec((rows, RNN_BS), lambda s, c: (s, c)),
            pl.BlockSpec((2, n_seq, RNN_BS), lambda s, c: (0, s, c)),
        ],
        out_shape=[
            jax.ShapeDtypeStruct((n_batch * t_len, D_RNN), BF16),
            jax.ShapeDtypeStruct((2, n_batch, D_RNN), F32),
        ],
        scratch_shapes=[scan_buf, scan_buf, scan_buf, scan_buf],
        compiler_params=_cparams(("arbitrary", "arbitrary")),
        name="rglru",
    )(yx, yx, cw, cb.reshape(1, D_RNN), wg_cat, bg_cat, lam, h0)


def _rope(x, cos, sin_signed, first_of_pair):
    partner = jnp.where(first_of_pair, pltpu.roll(x, LANE - 16, 1), pltpu.roll(x, 16, 1))
    return x * cos + partner * sin_signed


def _attn_kernel(*refs, heads, t_len, s_len, rope, lambda_init, mxu_denominator):
    if rope:
        (lam_ref, g_ref, q_ref, k_ref, v_ref, ck_ref, cv_ref, cq_ref, sq_ref, ckk_ref, skk_ref,
         o_ref, kk_ref, vv_ref) = refs
    else:
        lam_ref, g_ref, q_ref, k_ref, v_ref, o_ref, kk_ref, vv_ref = refs
    tq = q_ref.shape[0]
    lane = lax.broadcasted_iota(I32, (1, LANE), 1)
    first_of_pair = (lane & 16) == 0
    map0 = lane < HD

    @pl.when(pl.program_id(2) == 0)
    def _():
        for j in range(heads):
            cs = slice(j * LANE, (j + 1) * LANE)
            k = k_ref[:, cs]
            if rope:
                k = _rope(k, ckk_ref[...], skk_ref[...], first_of_pair)
            kk_ref[j, 0:t_len, :] = k.astype(BF16)
            vv_ref[j, 0:t_len, 0:LANE] = v_ref[:, cs].astype(BF16)
            if s_len > t_len:
                kk_ref[j, t_len:s_len, :] = ck_ref[:, cs].astype(BF16)
                vv_ref[j, t_len:s_len, 0:LANE] = cv_ref[:, cs].astype(BF16)
            if mxu_denominator:
                vv_ref[j, :, LANE:2 * LANE] = jnp.ones((s_len, LANE), BF16)

    lp = lam_ref[...]
    lam = (jnp.exp(jnp.sum(lp[0:1] * lp[1:2], axis=-1, keepdims=True))
           - jnp.exp(jnp.sum(lp[2:3] * lp[3:4], axis=-1, keepdims=True)) + lambda_init)
    for j in range(heads):
        cs = slice(j * LANE, (j + 1) * LANE)
        q = q_ref[:, cs]
        if rope:
            q = _rope(q, cq_ref[...], sq_ref[...], first_of_pair)
        q = q * (HD ** -0.5 * LOG2_E)
        q2 = jnp.concatenate([jnp.where(map0, q, 0.0), jnp.where(map0, 0.0, q)], axis=0).astype(BF16)
        s = lax.dot_general(q2, kk_ref[j], (((1,), (1,)), ((), ())), preferred_element_type=F32)
        e = jnp.exp2(s - jnp.max(s, axis=-1, keepdims=True))
        if mxu_denominator:
            nd = jnp.dot(e.astype(BF16), vv_ref[j], preferred_element_type=F32)
            av = nd[:, 0:LANE] * (1.0 / nd[:, LANE:LANE + 1])
            o = av[0:tq] - lam * av[tq:2 * tq]
        else:
            p = e * (1.0 / jnp.sum(e, axis=-1, keepdims=True))
            w = p[0:tq] - lam * p[tq:2 * tq]
            o = jnp.dot(w.astype(BF16), vv_ref[j], preferred_element_type=F32)
        o = o * lax.rsqrt(jnp.mean(o * o, axis=-1, keepdims=True) + 1e-6) * g_ref[...]
        o_ref[:, cs] = (o * (1.0 - lambda_init)).astype(BF16)


def _attention(qkv, lam_p, subln_g, lambda_init, *, row0, n_batch, t_len, heads, tq,
               ctx_k=None, ctx_v=None, rope_tabs=None):
    rope = rope_tabs is not None
    s_len = t_len + (PAST if rope else 0)
    w = heads * LANE
    nhb = HEADS // heads
    nq = t_len // tq
    rq0 = row0 // tq
    rk0 = row0 // t_len
    mxu_denominator = s_len >= 1024
    kern = functools.partial(_attn_kernel, heads=heads, t_len=t_len, s_len=s_len, rope=rope,
                             lambda_init=lambda_init, mxu_denominator=mxu_denominator)
    v_width = 2 * LANE if mxu_denominator else LANE
    in_specs = [
        pl.BlockSpec((4, HD), lambda b, h, q: (0, 0)),
        pl.BlockSpec((1, VD), lambda b, h, q: (0, 0)),
        pl.BlockSpec((tq, w), lambda b, h, q: (rq0 + b * nq + q, h)),
        pl.BlockSpec((t_len, w), lambda b, h, q: (rk0 + b, nhb + h)),
        pl.BlockSpec((t_len, w), lambda b, h, q: (rk0 + b, 2 * nhb + h)),
    ]
    args = [lam_p, subln_g.reshape(1, VD), qkv, qkv, qkv]
    if rope:
        cos_t, sin_t = rope_tabs
        in_specs += [
            pl.BlockSpec((None, PAST, w), lambda b, h, q: (b, 0, h)),
            pl.BlockSpec((None, PAST, w), lambda b, h, q: (b, 0, h)),
            pl.BlockSpec((tq, LANE), lambda b, h, q: (q, 0)),
            pl.BlockSpec((tq, LANE), lambda b, h, q: (q, 0)),
            pl.BlockSpec((t_len, LANE), lambda b, h, q: (0, 0)),
            pl.BlockSpec((t_len, LANE), lambda b, h, q: (0, 0)),
        ]
        args += [ctx_k, ctx_v, cos_t, sin_t, cos_t, sin_t]
    return pl.pallas_call(
        kern,
        grid=(n_batch, nhb, nq),
        in_specs=in_specs,
        out_specs=pl.BlockSpec((tq, w), lambda b, h, q: (b * nq + q, h)),
        out_shape=jax.ShapeDtypeStruct((n_batch * t_len, D), BF16),
        scratch_shapes=[pltpu.VMEM((heads, s_len, LANE), BF16), pltpu.VMEM((heads, s_len, v_width), BF16)],
        compiler_params=_cparams(("arbitrary", "arbitrary", "arbitrary")),
        name="diff_attn",
    )(*args)


def _rope_tables(t_len):
    t = jnp.arange(t_len)
    half = HD // 2
    freqs = ROPE_THETA ** (-jnp.arange(0, half, 2, dtype=F32) / half)
    ang_r = (t // GRID_W).astype(F32)[:, None] * freqs[None]
    ang_c = (t % GRID_W).astype(F32)[:, None] * freqs[None]
    cr, sr, cc, sc = jnp.cos(ang_r), jnp.sin(ang_r), jnp.cos(ang_c), jnp.sin(ang_c)
    cos64 = jnp.concatenate([cr, cr, cc, cc], axis=-1)
    sin64 = jnp.concatenate([-sr, sr, -sc, sc], axis=-1)
    return jnp.concatenate([cos64, cos64], axis=-1), jnp.concatenate([sin64, sin64], axis=-1)


def _swiglu_rows(h_bf16, wgu_ref, wd_ref, h1_ref):
    tn = 256
    for c in range(D_FF // tn):
        gg = jnp.dot(h_bf16, wgu_ref[:, c * tn:(c + 1) * tn], preferred_element_type=F32)
        uu = jnp.dot(h_bf16, wgu_ref[:, D_FF + c * tn:D_FF + (c + 1) * tn], preferred_element_type=F32)
        h1_ref[:, c * tn:(c + 1) * tn] = (jax.nn.silu(gg) * uu).astype(BF16)
    return jnp.dot(h1_ref[...], wd_ref[...], preferred_element_type=F32)


def _ffn_kernel(*refs, mixer_out):
    if mixer_out:
        ap_ref, al_ref, wout_ref, y_ref, g_ref, mod_ref, wgu_ref, wd_ref, o_ref, h1_ref = refs
    else:
        y_ref, g_ref, mod_ref, wgu_ref, wd_ref, o_ref, h1_ref = refs

    def body(a_ref):
        y = y_ref[...]
        if a_ref is not None:
            y = y + mod_ref[2:3, :] * jnp.dot(a_ref[...], wout_ref[...], preferred_element_type=F32)
        h = _modnorm(y, g_ref[...], mod_ref[3:4, :], mod_ref[4:5, :]).astype(BF16)
        o_ref[...] = y + mod_ref[5:6, :] * _swiglu_rows(h, wgu_ref, wd_ref, h1_ref)

    if mixer_out:
        _for_stream_of_tile(lambda part: body((ap_ref, al_ref)[part]))
    else:
        body(None)


def _ffn(y, g, mod, wgu_bf16, wd_bf16, slot, mixer_out=None):
    pre_specs, pre_args = [], []
    if mixer_out is not None:
        pre_specs = _mixer_out_specs(*mixer_out)
        pre_args = [*mixer_out[0], mixer_out[1]]
    return pl.pallas_call(
        functools.partial(_ffn_kernel, mixer_out=mixer_out is not None),
        grid=(N_TOK // TM,),
        in_specs=pre_specs + [
            pl.BlockSpec((TM, D), lambda i: (i, 0)),
            pl.BlockSpec((1, D), lambda i: (0, 0)),
            pl.BlockSpec((None, 6, D), lambda i: (_cond_of_tile(i, TM), 0, 0)),
            pl.BlockSpec((None, D, 2 * D_FF), lambda i: (slot, 0, 0), pipeline_mode=pl.Buffered(1)),
            pl.BlockSpec((None, D_FF, D), lambda i: (slot, 0, 0), pipeline_mode=pl.Buffered(1)),
        ],
        out_specs=pl.BlockSpec((TM, D), lambda i: (i, 0)),
        out_shape=jax.ShapeDtypeStruct((N_TOK, D), F32),
        scratch_shapes=[pltpu.VMEM((TM, D_FF), BF16)],
        compiler_params=_cparams(("arbitrary",)),
        name="ffn",
    )(*pre_args, y, g.reshape(1, D), mod, wgu_bf16, wd_bf16)


def _router_kernel(*refs, mixer_out):
    if mixer_out:
        (ap_ref, al_ref, wout_ref, y_ref, g_ref, mod_ref, rt_ref,
         ym_ref, hs_ref, ii_ref, wf_ref, cnt_ref, tb_ref, carry_ref) = refs

        def add_mixer_out(part):
            acc = jnp.dot((ap_ref, al_ref)[part][...], wout_ref[...], preferred_element_type=F32)
            ym_ref[...] = y_ref[...] + mod_ref[2:3, :] * acc

        _for_stream_of_tile(add_mixer_out)
        y = ym_ref[...]
    else:
        y_ref, g_ref, mod_ref, rt_ref, hs_ref, ii_ref, wf_ref, cnt_ref, tb_ref, carry_ref = refs
        y = y_ref[...]
    tm = y_ref.shape[0]

    @pl.when(pl.program_id(0) == 0)
    def _():
        carry_ref[...] = jnp.zeros_like(carry_ref)

    tb_ref[...] = carry_ref[...].astype(I32)

    h = _modnorm(y, g_ref[...], mod_ref[3:4, :], mod_ref[4:5, :])
    for k in range(SLAB):
        hs_ref[:, k, :] = h[:, k * LANE:(k + 1) * LANE]
    logits = lax.dot_general(rt_ref[...], h.astype(BF16), (((1,), (1,)), ((), ())),
                             preferred_element_type=F32)
    eidx = lax.broadcasted_iota(I32, (N_EXP, tm), 0)
    eidx_f = eidx.astype(F32)
    m1 = jnp.max(logits, axis=0, keepdims=True)
    i1 = jnp.min(jnp.where(logits == m1, eidx_f, float(N_EXP)), axis=0, keepdims=True)
    oh1 = eidx_f == i1
    rest = jnp.where(oh1, -jnp.inf, logits)
    m2 = jnp.max(rest, axis=0, keepdims=True)
    i2 = jnp.min(jnp.where(rest == m2, eidx_f, float(N_EXP)), axis=0, keepdims=True)
    oh2 = eidx_f == i2
    i1 = i1.astype(I32)
    i2 = i2.astype(I32)
    e2 = jnp.exp(m2 - m1)
    w1 = 1.0 / (1.0 + e2)
    w2 = e2 / (1.0 + e2)
    sel = jnp.where(oh1 | oh2, 1.0, 0.0)
    before = (lax.broadcasted_iota(I32, (tm, tm), 0) < lax.broadcasted_iota(I32, (tm, tm), 1))
    cum = jnp.dot(sel.astype(BF16), jnp.where(before, 1.0, 0.0).astype(BF16), preferred_element_type=F32)
    cum = cum + carry_ref[:, 0:1]
    r1 = jnp.sum(jnp.where(oh1, cum, 0.0), axis=0, keepdims=True).astype(I32)
    r2 = jnp.sum(jnp.where(oh2, cum, 0.0), axis=0, keepdims=True).astype(I32)
    ii_ref[...] = jnp.where(eidx == 0, i1, jnp.where(eidx == 1, i2, jnp.where(eidx == 2, r1,
                            jnp.where(eidx == 3, r2, 0))))
    wf_ref[...] = jnp.where(eidx == 0, w1, jnp.where(eidx == 1, w2, 0.0))
    carry_ref[...] = carry_ref[...] + jnp.sum(sel, axis=1, keepdims=True)
    cnt_ref[...] = carry_ref[...].astype(I32)


def _router(y, g, mod, router_t_bf16, mixer_out=None):
    pre_specs, pre_args, pre_out_specs, pre_out_shape = [], [], [], []
    if mixer_out is not None:
        pre_specs = _mixer_out_specs(*mixer_out)
        pre_args = [*mixer_out[0], mixer_out[1]]
        pre_out_specs = [pl.BlockSpec((TM, D), lambda i: (i, 0))]
        pre_out_shape = [jax.ShapeDtypeStruct((N_TOK, D), F32)]
    outs = pl.pallas_call(
        functools.partial(_router_kernel, mixer_out=mixer_out is not None),
        grid=(N_TOK // TM,),
        in_specs=pre_specs + [
            pl.BlockSpec((TM, D), lambda i: (i, 0)),
            pl.BlockSpec((1, D), lambda i: (0, 0)),
            pl.BlockSpec((None, 6, D), lambda i: (_cond_of_tile(i, TM), 0, 0)),
            pl.BlockSpec((N_EXP, D), lambda i: (0, 0)),
        ],
        out_specs=pre_out_specs + [
            pl.BlockSpec((TM, SLAB, LANE), lambda i: (i, 0, 0)),
            pl.BlockSpec((N_EXP, TM), lambda i: (0, i)),
            pl.BlockSpec((N_EXP, TM), lambda i: (0, i)),
            pl.BlockSpec((N_EXP, LANE), lambda i: (0, 0)),
            pl.BlockSpec((N_EXP, LANE), lambda i: (i, 0)),
        ],
        out_shape=pre_out_shape + [
            jax.ShapeDtypeStruct((N_TOK, SLAB, LANE), F32),
            jax.ShapeDtypeStruct((N_EXP, N_TOK), I32),
            jax.ShapeDtypeStruct((N_EXP, N_TOK), F32),
            jax.ShapeDtypeStruct((N_EXP, LANE), I32),
            jax.ShapeDtypeStruct((N_TOK // TM * N_EXP, LANE), I32),
        ],
        scratch_shapes=[pltpu.VMEM((N_EXP, LANE), F32)],
        compiler_params=_cparams(("arbitrary",)),
        name="router",
    )(*pre_args, y, g.reshape(1, D), mod, router_t_bf16)
    return tuple(outs) if mixer_out is not None else (y, *outs)


def _segment_copies(i, seg_ref, dst_ref, nch_ref, stage_ref, hbm_ref, sem, *, to_hbm):
    def chunk(s_row, d_row):
        st = stage_ref.at[pl.ds(s_row, CH)]
        hb = hbm_ref.at[pl.ds(d_row, CH)]
        return pltpu.make_async_copy(st, hb, sem) if to_hbm else pltpu.make_async_copy(hb, st, sem)

    total = 0
    for e in range(N_EXP):
        k = i * N_EXP + e
        s0, d0, n = seg_ref[k], dst_ref[k], nch_ref[k]

        def start(c, carry, s0=s0, d0=d0):
            chunk(s0 + c * CH, d0 + c * CH).start()
            return carry

        lax.fori_loop(0, n, start, 0)
        total = total + n

    def wait(c, carry):
        chunk(0, 0).wait()
        return carry

    lax.fori_loop(0, total, wait, 0)


def _dispatch_kernel(l1_ref, l2_ref, seg_ref, dst_ref, nch_ref, zs_ref, zn_ref, hs_ref, xs_hbm, stage_ref, sem):
    i = pl.program_id(0)

    @pl.when(i == 0)
    def _():
        stage_ref[...] = jnp.zeros_like(stage_ref)

        def zero_chunk(row):
            return pltpu.make_async_copy(stage_ref.at[pl.ds(0, CH)], xs_hbm.at[pl.ds(row, CH)], sem)

        total = 0
        for z in range(N_EXP + 1):
            z0, n = zs_ref[z], zn_ref[z]

            def start(c, carry, z0=z0):
                zero_chunk(z0 + c * CH).start()
                return carry

            lax.fori_loop(0, n, start, 0)
            total = total + n

        def wait(c, carry):
            zero_chunk(0).wait()
            return carry

        lax.fori_loop(0, total, wait, 0)

    base = i * TM

    def place(t8, carry):
        for u in range(SUBLANE):
            t = t8 * SUBLANE + u
            row = hs_ref[t]
            stage_ref[l1_ref[base + t]] = row
            stage_ref[l2_ref[base + t]] = row
        return carry

    lax.fori_loop(0, TM // SUBLANE, place, 0)
    _segment_copies(i, seg_ref, dst_ref, nch_ref, stage_ref, xs_hbm, sem, to_hbm=True)


def _dispatch(tabs, zero_tabs, hs):
    return pl.pallas_call(
        _dispatch_kernel,
        grid_spec=pltpu.PrefetchScalarGridSpec(
            num_scalar_prefetch=7,
            grid=(N_TOK // TM,),
            in_specs=[pl.BlockSpec((TM, SLAB, LANE), lambda i, *_: (i, 0, 0))],
            out_specs=pl.BlockSpec(memory_space=pl.ANY),
            scratch_shapes=[pltpu.VMEM((STAGE_ROWS, SLAB, LANE), F32), pltpu.SemaphoreType.DMA(())],
        ),
        out_shape=jax.ShapeDtypeStruct((MAX_TILES * TM_E, SLAB, LANE), F32),
        compiler_params=_cparams(("arbitrary",)),
        name="moe_dispatch",
    )(*tabs, *zero_tabs, hs)


def _expert_kernel(te_ref, tv_ref, x_ref, wgu_ref, wd_ref, o_ref, xb_ref, h1_ref):
    del te_ref
    valid = tv_ref[pl.program_id(0)] == 1

    @pl.when(valid)
    def _():
        for k in range(SLAB):
            xb_ref[:, k * LANE:(k + 1) * LANE] = x_ref[pl.ds(k, TM_E, stride=SLAB), :].astype(BF16)
        acc = _swiglu_rows(xb_ref[...], wgu_ref, wd_ref, h1_ref)
        for k in range(SLAB):
            o_ref[:, k, :] = acc[:, k * LANE:(k + 1) * LANE]

    @pl.when(jnp.logical_not(valid))
    def _():
        o_ref[...] = jnp.zeros_like(o_ref)


def _experts(tile_expert, tile_valid, xs, wgu_bf16, wd_bf16, slot):
    return pl.pallas_call(
        _expert_kernel,
        grid_spec=pltpu.PrefetchScalarGridSpec(
            num_scalar_prefetch=2,
            grid=(MAX_TILES,),
            in_specs=[
                pl.BlockSpec((TM_E * SLAB, LANE), lambda i, te, tv: (i, 0)),
                pl.BlockSpec((None, None, D, 2 * D_FF), lambda i, te, tv: (slot, te[i], 0, 0)),
                pl.BlockSpec((None, None, D_FF, D), lambda i, te, tv: (slot, te[i], 0, 0)),
            ],
            out_specs=pl.BlockSpec((TM_E, SLAB, LANE), lambda i, te, tv: (i, 0, 0)),
            scratch_shapes=[pltpu.VMEM((TM_E, D), BF16), pltpu.VMEM((TM_E, D_FF), BF16)],
        ),
        out_shape=jax.ShapeDtypeStruct(xs.shape, F32),
        compiler_params=_cparams(("arbitrary",)),
        name="moe_experts",
    )(tile_expert, tile_valid, xs.reshape(-1, LANE), wgu_bf16, wd_bf16)


def _combine_kernel(l1_ref, l2_ref, seg_ref, dst_ref, nch_ref, *refs, final):
    if final:
        y_ref, mod_ref, w_ref, fg_ref, ys_hbm, op_ref, ol_ref, stage_ref, g1_ref, g2_ref, sem, o_ref = refs
    else:
        y_ref, mod_ref, w_ref, ys_hbm, o_ref, stage_ref, g1_ref, g2_ref, sem = refs
    i = pl.program_id(0)
    _segment_copies(i, seg_ref, dst_ref, nch_ref, stage_ref, ys_hbm, sem, to_hbm=False)
    base = i * TM

    def pick(t8, carry):
        for u in range(SUBLANE):
            t = t8 * SUBLANE + u
            rows = pl.ds(pl.multiple_of(t * SLAB, SLAB), SLAB)
            g1_ref[rows, :] = stage_ref[l1_ref[base + t]]
            g2_ref[rows, :] = stage_ref[l2_ref[base + t]]
        return carry

    lax.fori_loop(0, TM // SUBLANE, pick, 0)
    w1 = w_ref[:, 0:1]
    w2 = w_ref[:, 1:2]
    for k in range(SLAB):
        cs = slice(k * LANE, (k + 1) * LANE)
        chunk_k = pl.ds(k, TM, stride=SLAB)
        f = w1 * g1_ref[chunk_k, :] + w2 * g2_ref[chunk_k, :]
        o_ref[:, cs] = y_ref[:, cs] + mod_ref[5:6, cs] * f
    if final:
        yn = o_ref[...]
        out = yn * lax.rsqrt(jnp.mean(yn * yn, axis=-1, keepdims=True) + 1e-6) * fg_ref[...]

        def write(part):
            (op_ref, ol_ref)[part][...] = out

        _for_stream_of_tile(write)


def _combine(tabs, y, mod, w_cols, ys, final_g=None):
    final = final_g is not None
    row_spec = pl.BlockSpec((TM, D), lambda i, *_: (i, 0))
    scratch = [
        pltpu.VMEM((STAGE_ROWS, SLAB, LANE), F32),
        pltpu.VMEM((TM * SLAB, LANE), F32),
        pltpu.VMEM((TM * SLAB, LANE), F32),
        pltpu.SemaphoreType.DMA(()),
    ]
    if final:
        out_specs = [pl.BlockSpec((TM, D), lambda i, *_: (jnp.minimum(i, N_PROMPT_TILES - 1), 0)),
                     pl.BlockSpec((TM, D), lambda i, *_: (jnp.maximum(i - N_PROMPT_TILES, 0), 0))]
        out_shape = [jax.ShapeDtypeStruct((N_P, D), F32), jax.ShapeDtypeStruct((N_L, D), F32)]
        scratch.append(pltpu.VMEM((TM, D), F32))
        extra_specs, extra_args = [pl.BlockSpec((1, D), lambda i, *_: (0, 0))], [final_g.reshape(1, D)]
    else:
        out_specs, out_shape, extra_specs, extra_args = row_spec, jax.ShapeDtypeStruct((N_TOK, D), F32), [], []
    return pl.pallas_call(
        functools.partial(_combine_kernel, final=final),
        grid_spec=pltpu.PrefetchScalarGridSpec(
            num_scalar_prefetch=5,
            grid=(N_TOK // TM,),
            in_specs=[
                row_spec,
                pl.BlockSpec((None, 6, D), lambda i, *_: (_cond_of_tile(i, TM), 0, 0)),
                pl.BlockSpec((TM, 2), lambda i, *_: (i, 0)),
                *extra_specs,
                pl.BlockSpec(memory_space=pl.ANY),
            ],
            out_specs=out_specs,
            scratch_shapes=scratch,
        ),
        out_shape=out_shape,
        compiler_params=_cparams(("arbitrary",)),
        name="moe_combine",
    )(*tabs, y, mod, w_cols, *extra_args, ys)


def _moe(y, g, mod, router, wgu_bf16, wd_bf16, slot, mixer_out=None, final_g=None):
    y, hs, ii, wf, cnt, tbase = _router(y, g, mod, router.T.astype(BF16), mixer_out)
    n_tt = N_TOK // TM
    counts = cnt[:, 0]
    before = tbase[:, 0].reshape(n_tt, N_EXP)
    n_seg = jnp.concatenate([before[1:], counts[None]], axis=0) - before
    region = (counts + CH + TM_E - 1) // TM_E * TM_E
    region_end = jnp.cumsum(region)
    offs = region_end - region
    seg_pad = (n_seg + CH - 1) // CH * CH
    seg = jnp.cumsum(seg_pad, axis=1) - seg_pad
    dst = offs[None, :] + before
    e_ids = jnp.arange(N_EXP, dtype=I32)[:, None]
    shift = jnp.repeat((seg - before).T, TM, axis=1)
    l1 = jnp.sum(jnp.where(ii[0:1] == e_ids, shift, 0), axis=0) + ii[2]
    l2 = jnp.sum(jnp.where(ii[1:2] == e_ids, shift, 0), axis=0) + ii[3]
    tabs = (l1, l2, seg.reshape(-1), dst.reshape(-1), (seg_pad // CH).reshape(-1))
    row0 = jnp.arange(MAX_TILES, dtype=I32) * TM_E
    tile_expert = jnp.minimum(jnp.sum((row0[:, None] >= region_end[None, :]).astype(I32), axis=1), N_EXP - 1)
    tile_valid = (row0 < (offs + counts)[tile_expert]).astype(I32)
    zero_start = offs + counts // CH * CH
    zero_tabs = (jnp.concatenate([zero_start, region_end[-1:]]),
                 jnp.concatenate([region_end - zero_start, MAX_TILES * TM_E - region_end[-1:]]) // CH)
    xs = _dispatch(tabs, zero_tabs, hs)
    ys = _experts(tile_expert, tile_valid, xs, wgu_bf16, wd_bf16, slot)
    return _combine(tabs, y, mod, wf[0:2].T, ys, final_g)


def kernel(x_prompt, x_sample, state_rglru, cache_k, cache_v, c, c_ctx, norm_g, ada_w, ada_b,
           gm_w_in, gm_ln_g, gm_w_s, gm_b_s, gm_w_out, rg_w_in, rg_conv_w, rg_conv_b, rg_w_gate,
           rg_b_gate, rg_lambda, rg_w_out, att_w_qkv, att_lambda, att_subln_g, att_w_out,
           ff_w_gu, ff_w_down, moe_router, moe_w_gu, moe_w_down, final_g):
    y = jnp.concatenate([x_prompt.reshape(N_P, D), x_sample.reshape(N_L, D)], axis=0)
    conds = jnp.concatenate([c_ctx[None], c, jnp.zeros((N_COND - 1 - B_L, D), F32)], axis=0)
    mods = _adaln(conds, ada_w, ada_b)
    nb = D_RNN // RNN_BS
    gm_w_in, gm_w_out, rg_w_in, rg_w_out, att_w_qkv, att_w_out, ff_w_gu, ff_w_down, moe_w_gu, moe_w_down = (
        w.astype(BF16) for w in (gm_w_in, gm_w_out, rg_w_in, rg_w_out, att_w_qkv, att_w_out,
                                 ff_w_gu, ff_w_down, moe_w_gu, moe_w_down))
    new_rnn = new_k = new_v = None
    for i in range(DEPTH):
        mod = mods[i]
        kind, slot = i % 3, i // 3
        mixer_out = None
        if kind == 0:
            y = _gmlp_mixer(y, norm_g[i, 0], mod, gm_w_in, gm_ln_g[slot], gm_w_s[slot].astype(BF16),
                            gm_b_s[slot].T, gm_w_out, slot)
        elif kind == 1:
            yx = _mod_matmul(y, norm_g[i, 0], mod, rg_w_in, slot, tn=256, gelu_cols=D_RNN)
            wg_cat = rg_w_gate[slot].transpose(2, 3, 0, 1, 4).reshape(nb, RNN_BS, 4 * RNN_BS).astype(BF16)
            bg_cat = rg_b_gate[slot].reshape(2, 2, nb, RNN_BS).transpose(2, 0, 1, 3).reshape(nb, 1, 4 * RNN_BS)
            rg_args = (yx, rg_conv_w[slot], rg_conv_b[slot], wg_cat, bg_cat, rg_lambda[slot])
            a_p, fin = _rglru(*rg_args, jnp.zeros((2, B_P, D_RNN), F32),
                              row0=0, n_batch=B_P, n_seq=SUBLANE, t_len=T_P)
            a_l, _ = _rglru(*rg_args, state_rglru[:, slot].transpose(1, 0, 2),
                            row0=N_P, n_batch=B_L, n_seq=B_L, t_len=T_L)
            new_rnn = fin.transpose(1, 0, 2)[:, None]
            mixer_out = ((a_p, a_l), rg_w_out, slot)
        else:
            lambda_init = 0.8 - 0.6 * math.exp(-0.3 * i)
            qkv = _mod_matmul(y, norm_g[i, 0], mod, att_w_qkv, slot, tn=512, gelu_cols=0)
            att = (qkv, att_lambda[slot], att_subln_g[slot], lambda_init)
            a_p = _attention(*att, row0=0, n_batch=B_P, t_len=T_P, heads=HEADS, tq=T_P)
            a_l = _attention(*att, row0=N_P, n_batch=B_L, t_len=T_L, heads=4, tq=256,
                             ctx_k=cache_k[:, slot].reshape(B_L, PAST, D),
                             ctx_v=cache_v[:, slot].reshape(B_L, PAST, D),
                             rope_tabs=_rope_tables(T_L))
            new_k = qkv[:N_P, D:2 * D].reshape(B_P, 1, T_P, HEADS, 2 * HD)
            new_v = qkv[:N_P, 2 * D:].reshape(B_P, 1, T_P, HEADS, VD)
            mixer_out = ((a_p, a_l), att_w_out, slot)
        fslot = i // 2
        if i % 2 == 0:
            y = _ffn(y, norm_g[i, 1], mod, ff_w_gu, ff_w_down, fslot, mixer_out)
        else:
            y = _moe(y, norm_g[i, 1], mod, moe_router[fslot], moe_w_gu, moe_w_down, fslot, mixer_out,
                     final_g if i == DEPTH - 1 else None)
    y_prompt, y_sample = y
    return (y_prompt.reshape(B_P, T_P, D), y_sample.reshape(B_L, T_L, D), new_rnn, new_k, new_v)
```

```python
import functools
import math

import jax
import jax.numpy as jnp
from jax import lax
from jax.experimental import pallas as pl
from jax.experimental.pallas import tpu as pltpu

F32 = jnp.float32
BF16 = jnp.bfloat16
I32 = jnp.int32

D = 1024
DEPTH = 4
B_P, T_P = 32, 256
B_L, T_L = 4, 2048
PAST = 256
N_P = B_P * T_P
N_L = B_L * T_L
N_TOK = N_P + N_L
N_COND = 8
GRID_W = 64
CHUNK = 128
GROUPS = 8
D_RNN = 1280
RNN_BS = 128
CONV_W = 4
RGLRU_C = 8.0
HD = 64
VD = 128
HEADS = 8
ROPE_THETA = 10000.0
LOG2_E = 1.4426950408889634
D_FF = 2816
N_EXP = 8

LANE = 128
SUBLANE = 8
VMEM_LIMIT = 56 * 1024 * 1024
TM = 512
TM_E = 256
CH = 32
MAX_TILES = -(-(2 * N_TOK + N_EXP * (CH + TM_E - 1)) // TM_E)
STAGE_ROWS = 2 * TM + N_EXP * CH
SLAB = D // LANE


def _cparams(sem):
    return pltpu.CompilerParams(dimension_semantics=sem, vmem_limit_bytes=VMEM_LIMIT)


def _cond_of_tile(i, tm):
    r0 = i * tm
    return jnp.where(r0 < N_P, 0, 1 + (r0 - N_P) // T_L)


def _modnorm(x, g, shift, scale):
    ms = jnp.mean(x * x, axis=-1, keepdims=True)
    h = x * lax.rsqrt(ms + 1e-6) * g
    return h * (1.0 + scale) + shift


def _adaln_kernel(c_ref, w_ref, b_ref, o_ref):
    s = jax.nn.silu(c_ref[...]).astype(BF16)
    o_ref[...] = jnp.dot(s, w_ref[...].astype(BF16), preferred_element_type=F32) + b_ref[...]


def _adaln(conds, ada_w, ada_b):
    tn = 1536
    n6 = 6 * D
    out = pl.pallas_call(
        _adaln_kernel,
        grid=(DEPTH, n6 // tn),
        in_specs=[
            pl.BlockSpec((N_COND, D), lambda l, j: (0, 0)),
            pl.BlockSpec((None, D, tn), lambda l, j: (l, 0, j)),
            pl.BlockSpec((None, 1, tn), lambda l, j: (l, 0, j)),
        ],
        out_specs=pl.BlockSpec((None, N_COND, tn), lambda l, j: (l, 0, j)),
        out_shape=jax.ShapeDtypeStruct((DEPTH, N_COND, n6), F32),
        compiler_params=_cparams(("arbitrary", "arbitrary")),
        name="adaln",
    )(conds, ada_w, ada_b.reshape(DEPTH, 1, n6))
    return out.reshape(DEPTH, N_COND, 6, D)


def _mod_matmul_kernel(x_ref, g_ref, mod_ref, w_ref, o_ref, *, tn, gelu_cols):
    h = _modnorm(x_ref[...], g_ref[...], mod_ref[0:1, :], mod_ref[1:2, :]).astype(BF16)
    for c in range(o_ref.shape[1] // tn):
        acc = jnp.dot(h, w_ref[:, c * tn:(c + 1) * tn], preferred_element_type=F32)
        if c * tn < gelu_cols:
            acc = jax.nn.gelu(acc)
        o_ref[:, c * tn:(c + 1) * tn] = acc


def _mod_matmul(y, g, mod, w_bf16, slot, *, tn, gelu_cols):
    nout = w_bf16.shape[2]
    return pl.pallas_call(
        functools.partial(_mod_matmul_kernel, tn=tn, gelu_cols=gelu_cols),
        grid=(N_TOK // TM,),
        in_specs=[
            pl.BlockSpec((TM, D), lambda i: (i, 0)),
            pl.BlockSpec((1, D), lambda i: (0, 0)),
            pl.BlockSpec((None, 6, D), lambda i: (_cond_of_tile(i, TM), 0, 0)),
            pl.BlockSpec((None, D, nout), lambda i: (slot, 0, 0)),
        ],
        out_specs=pl.BlockSpec((TM, nout), lambda i: (i, 0)),
        out_shape=jax.ShapeDtypeStruct((N_TOK, nout), F32),
        compiler_params=_cparams(("arbitrary",)),
        name="mod_matmul",
    )(y, g.reshape(1, D), mod, w_bf16)


N_PROMPT_TILES = N_P // TM


def _mixer_out_specs(a_parts, w_out_bf16, slot):
    k = a_parts[0].shape[1]
    return [
        pl.BlockSpec((TM, k), lambda i, *_: (jnp.minimum(i, N_PROMPT_TILES - 1), 0)),
        pl.BlockSpec((TM, k), lambda i, *_: (jnp.maximum(i - N_PROMPT_TILES, 0), 0)),
        pl.BlockSpec((None, k, D), lambda i, *_: (slot, 0, 0)),
    ]


def _for_stream_of_tile(body):
    pl.when(pl.program_id(0) < N_PROMPT_TILES)(lambda: body(0))
    pl.when(pl.program_id(0) >= N_PROMPT_TILES)(lambda: body(1))


def _gmlp_kernel(y_ref, g_ref, mod_ref, win_ref, lng_ref, ws_ref, bs_ref, wout_ref, o_ref, uv_ref, a_ref):
    y = y_ref[...]
    h = _modnorm(y, g_ref[...], mod_ref[0:1, :], mod_ref[1:2, :]).astype(BF16)
    tn = 512
    for c in range(2 * D // tn):
        uv = jnp.dot(h, win_ref[:, c * tn:(c + 1) * tn], preferred_element_type=F32)
        uv_ref[:, c * tn:(c + 1) * tn] = jax.nn.gelu(uv)
    v = uv_ref[:, D:]
    vc = v - jnp.mean(v, axis=-1, keepdims=True)
    vn = vc * lax.rsqrt(jnp.mean(vc * vc, axis=-1, keepdims=True) + 1e-5) * lng_ref[...]
    vn = vn.astype(BF16)
    for ci in range(TM // CHUNK):
        r0 = ci * CHUNK
        for g in range(GROUPS):
            c0 = g * LANE
            s = jnp.dot(ws_ref[g], vn[r0:r0 + CHUNK, c0:c0 + LANE], preferred_element_type=F32)
            s = s + bs_ref[:, g:g + 1]
            a_ref[r0:r0 + CHUNK, c0:c0 + LANE] = (uv_ref[r0:r0 + CHUNK, c0:c0 + LANE] * s).astype(BF16)
    o_ref[...] = y + mod_ref[2:3, :] * jnp.dot(a_ref[...], wout_ref[...], preferred_element_type=F32)


def _gmlp_mixer(y, g, mod, w_in_bf16, ln_g, w_s_bf16, b_s_t, w_out_bf16, slot):
    return pl.pallas_call(
        _gmlp_kernel,
        grid=(N_TOK // TM,),
        in_specs=[
            pl.BlockSpec((TM, D), lambda i: (i, 0)),
            pl.BlockSpec((1, D), lambda i: (0, 0)),
            pl.BlockSpec((None, 6, D), lambda i: (_cond_of_tile(i, TM), 0, 0)),
            pl.BlockSpec((None, D, 2 * D), lambda i: (slot, 0, 0)),
            pl.BlockSpec((1, D), lambda i: (0, 0)),
            pl.BlockSpec((GROUPS, CHUNK, CHUNK), lambda i: (0, 0, 0)),
            pl.BlockSpec((CHUNK, GROUPS), lambda i: (0, 0)),
            pl.BlockSpec((None, D, D), lambda i: (slot, 0, 0)),
        ],
        out_specs=pl.BlockSpec((TM, D), lambda i: (i, 0)),
        out_shape=jax.ShapeDtypeStruct((N_TOK, D), F32),
        scratch_shapes=[pltpu.VMEM((TM, 2 * D), F32), pltpu.VMEM((TM, D), BF16)],
        compiler_params=_cparams(("arbitrary",)),
        name="gmlp_mixer",
    )(y, g.reshape(1, D), mod, w_in_bf16, ln_g.reshape(1, D), w_s_bf16, b_s_t, w_out_bf16)


SCAN_UNROLL = 8


def _rglru_kernel(yb_ref, xb_ref, cw_ref, cb_ref, wg_ref, bg_ref, lam_ref, h0_ref,
                  o_ref, fin_ref, af_ref, bf_ref, ab_ref, bb_ref, *, n_seq, t_len, row_chunk):
    stride = t_len + SUBLANE
    tix = lax.broadcasted_iota(I32, (t_len, 1), 0)
    sp_f = jax.nn.softplus(-lam_ref[0:1, :])
    sp_b = jax.nn.softplus(-lam_ref[1:2, :])
    for g in range(n_seq):
        x = xb_ref[g * t_len:(g + 1) * t_len, :]
        xm1 = jnp.where(tix >= 1, pltpu.roll(x, 1, 0), 0.0)
        xp1 = jnp.where(tix <= t_len - 2, pltpu.roll(x, t_len - 1, 0), 0.0)
        xp2 = jnp.where(tix <= t_len - 3, pltpu.roll(x, t_len - 2, 0), 0.0)
        xc = cb_ref[...] + xm1 * cw_ref[0:1, :]
        xc = xc + x * cw_ref[1:2, :]
        xc = xc + xp1 * cw_ref[2:3, :]
        xc = xc + xp2 * cw_ref[3:4, :]
        for r0 in range(0, t_len, row_chunk):
            xcc = xc[r0:r0 + row_chunk]
            gates = jnp.dot(xcc.astype(BF16), wg_ref[...], preferred_element_type=F32) + bg_ref[...]
            d0 = g * stride + r0
            for d, (sp, a_ref, b_ref) in enumerate(((sp_f, af_ref, bf_ref), (sp_b, ab_ref, bb_ref))):
                r = jax.nn.sigmoid(gates[:, (2 * d) * LANE:(2 * d + 1) * LANE])
                i = jax.nn.sigmoid(gates[:, (2 * d + 1) * LANE:(2 * d + 2) * LANE])
                log_a = (-RGLRU_C * r) * sp
                a = jnp.exp(log_a)
                one_minus_a2 = jnp.tanh(-log_a) * (a * a + 1.0)
                a_ref[d0:d0 + row_chunk, :] = a
                b_ref[d0:d0 + row_chunk, :] = jnp.sqrt(one_minus_a2) * (i * xcc)

    def steps(s8, carry):
        hf, hb = carry
        for u in range(SCAN_UNROLL):
            s = s8 * SCAN_UNROLL + u
            tf = pl.ds(s, n_seq, stride=stride)
            tb = pl.ds(t_len - 1 - s, n_seq, stride=stride)
            hf = af_ref[tf, :] * hf + bf_ref[tf, :]
            hb = ab_ref[tb, :] * hb + bb_ref[tb, :]
            bf_ref[tf, :] = hf
            bb_ref[tb, :] = hb
        return hf, hb

    hf, hb = lax.fori_loop(0, t_len // SCAN_UNROLL, steps, (h0_ref[0], h0_ref[1]))
    fin_ref[0] = hf
    fin_ref[1] = hb
    for g in range(n_seq):
        rs = slice(g * stride, g * stride + t_len)
        hsum = bf_ref[rs, :] + bb_ref[rs, :]
        o_ref[g * t_len:(g + 1) * t_len, :] = (hsum * yb_ref[g * t_len:(g + 1) * t_len, :]).astype(BF16)


def _rglru(yx, cw, cb, wg_cat, bg_cat, lam, h0, *, row0, n_batch, n_seq, t_len):
    rows = n_seq * t_len
    nb = D_RNN // RNN_BS
    rb0 = row0 // rows
    kern = functools.partial(_rglru_kernel, n_seq=n_seq, t_len=t_len, row_chunk=min(t_len, 512))
    scan_buf = pltpu.VMEM((n_seq * (t_len + SUBLANE), RNN_BS), F32)
    return pl.pallas_call(
        kern,
        grid=(n_batch // n_seq, nb),
        in_specs=[
            pl.BlockSpec((rows, RNN_BS), lambda s, c: (rb0 + s, c)),
            pl.BlockSpec((rows, RNN_BS), lambda s, c: (rb0 + s, nb + c)),
            pl.BlockSpec((CONV_W, RNN_BS), lambda s, c: (0, c)),
            pl.BlockSpec((1, RNN_BS), lambda s, c: (0, c)),
            pl.BlockSpec((None, RNN_BS, 4 * RNN_BS), lambda s, c: (c, 0, 0)),
            pl.BlockSpec((None, 1, 4 * RNN_BS), lambda s, c: (c, 0, 0)),
            pl.BlockSpec((2, RNN_BS), lambda s, c: (0, c)),
            pl.BlockSpec((2, n_seq, RNN_BS), lambda s, c: (0, s, c)),
        ],
        out_specs=[
            pl.BlockSpec((rows, RNN_BS), lambda s, c: (s, c)),
            pl.BlockSpec((2, n_seq, RNN_BS), lambda s, c: (0, s, c)),
        ],
        out_shape=[
            jax.ShapeDtypeStruct((n_batch * t_len, D_RNN), BF16),
            jax.ShapeDtypeStruct((2, n_batch, D_RNN), F32),
        ],
        scratch_shapes=[scan_buf, scan_buf, scan_buf, scan_buf],
        compiler_params=_cparams(("arbitrary", "arbitrary")),
        name="rglru",
    )(yx, yx, cw, cb.reshape(1, D_RNN), wg_cat, bg_cat, lam, h0)


def _rope(x, cos, sin_signed, first_of_pair):
    partner = jnp.where(first_of_pair, pltpu.roll(x, LANE - 16, 1), pltpu.roll(x, 16, 1))
    return x * cos + partner * sin_signed


def _attn_kernel(*refs, heads, t_len, s_len, rope, lambda_init, mxu_denominator):
    if rope:
        (lam_ref, g_ref, q_ref, k_ref, v_ref, ck_ref, cv_ref, cq_ref, sq_ref, ckk_ref, skk_ref,
         o_ref, kk_ref, vv_ref) = refs
    else:
        lam_ref, g_ref, q_ref, k_ref, v_ref, o_ref, kk_ref, vv_ref = refs
    tq = q_ref.shape[0]
    lane = lax.broadcasted_iota(I32, (1, LANE), 1)
    first_of_pair = (lane & 16) == 0
    map0 = lane < HD

    @pl.when(pl.program_id(2) == 0)
    def _():
        for j in range(heads):
            cs = slice(j * LANE, (j + 1) * LANE)
            k = k_ref[:, cs]
            if rope:
                k = _rope(k, ckk_ref[...], skk_ref[...], first_of_pair)
            kk_ref[j, 0:t_len, :] = k.astype(BF16)
            vv_ref[j, 0:t_len, 0:LANE] = v_ref[:, cs].astype(BF16)
            if s_len > t_len:
                kk_ref[j, t_len:s_len, :] = ck_ref[:, cs].astype(BF16)
                vv_ref[j, t_len:s_len, 0:LANE] = cv_ref[:, cs].astype(BF16)
            if mxu_denominator:
                vv_ref[j, :, LANE:2 * LANE] = jnp.ones((s_len, LANE), BF16)

    lp = lam_ref[...]
    lam = (jnp.exp(jnp.sum(lp[0:1] * lp[1:2], axis=-1, keepdims=True))
           - jnp.exp(jnp.sum(lp[2:3] * lp[3:4], axis=-1, keepdims=True)) + lambda_init)
    for j in range(heads):
        cs = slice(j * LANE, (j + 1) * LANE)
        q = q_ref[:, cs]
        if rope:
            q = _rope(q, cq_ref[...], sq_ref[...], first_of_pair)
        q = q * (HD ** -0.5 * LOG2_E)
        q2 = jnp.concatenate([jnp.where(map0, q, 0.0), jnp.where(map0, 0.0, q)], axis=0).astype(BF16)
        s = lax.dot_general(q2, kk_ref[j], (((1,), (1,)), ((), ())), preferred_element_type=F32)
        e = jnp.exp2(s - jnp.max(s, axis=-1, keepdims=True))
        if mxu_denominator:
            nd = jnp.dot(e.astype(BF16), vv_ref[j], preferred_element_type=F32)
            av = nd[:, 0:LANE] * (1.0 / nd[:, LANE:LANE + 1])
            o = av[0:tq] - lam * av[tq:2 * tq]
        else:
            p = e * (1.0 / jnp.sum(e, axis=-1, keepdims=True))
            w = p[0:tq] - lam * p[tq:2 * tq]
            o = jnp.dot(w.astype(BF16), vv_ref[j], preferred_element_type=F32)
        o = o * lax.rsqrt(jnp.mean(o * o, axis=-1, keepdims=True) + 1e-6) * g_ref[...]
        o_ref[:, cs] = (o * (1.0 - lambda_init)).astype(BF16)


def _attention(qkv, lam_p, subln_g, lambda_init, *, row0, n_batch, t_len, heads, tq,
               ctx_k=None, ctx_v=None, rope_tabs=None):
    rope = rope_tabs is not None
    s_len = t_len + (PAST if rope else 0)
    w = heads * LANE
    nhb = HEADS // heads
    nq = t_len // tq
    rq0 = row0 // tq
    rk0 = row0 // t_len
    mxu_denominator = s_len >= 1024
    kern = functools.partial(_attn_kernel, heads=heads, t_len=t_len, s_len=s_len, rope=rope,
                             lambda_init=lambda_init, mxu_denominator=mxu_denominator)
    v_width = 2 * LANE if mxu_denominator else LANE
    in_specs = [
        pl.BlockSpec((4, HD), lambda b, h, q: (0, 0)),
        pl.BlockSpec((1, VD), lambda b, h, q: (0, 0)),
        pl.BlockSpec((tq, w), lambda b, h, q: (rq0 + b * nq + q, h)),
        pl.BlockSpec((t_len, w), lambda b, h, q: (rk0 + b, nhb + h)),
        pl.BlockSpec((t_len, w), lambda b, h, q: (rk0 + b, 2 * nhb + h)),
    ]
    args = [lam_p, subln_g.reshape(1, VD), qkv, qkv, qkv]
    if rope:
        cos_t, sin_t = rope_tabs
        in_specs += [
            pl.BlockSpec((None, PAST, w), lambda b, h, q: (b, 0, h)),
            pl.BlockSpec((None, PAST, w), lambda b, h, q: (b, 0, h)),
            pl.BlockSpec((tq, LANE), lambda b, h, q: (q, 0)),
            pl.BlockSpec((tq, LANE), lambda b, h, q: (q, 0)),
            pl.BlockSpec((t_len, LANE), lambda b, h, q: (0, 0)),
            pl.BlockSpec((t_len, LANE), lambda b, h, q: (0, 0)),
        ]
        args += [ctx_k, ctx_v, cos_t, sin_t, cos_t, sin_t]
    return pl.pallas_call(
        kern,
        grid=(n_batch, nhb, nq),
        in_specs=in_specs,
        out_specs=pl.BlockSpec((tq, w), lambda b, h, q: (b * nq + q, h)),
        out_shape=jax.ShapeDtypeStruct((n_batch * t_len, D), BF16),
        scratch_shapes=[pltpu.VMEM((heads, s_len, LANE), BF16), pltpu.VMEM((heads, s_len, v_width), BF16)],
        compiler_params=_cparams(("arbitrary", "arbitrary", "arbitrary")),
        name="diff_attn",
    )(*args)


def _rope_tables(t_len):
    t = jnp.arange(t_len)
    half = HD // 2
    freqs = ROPE_THETA ** (-jnp.arange(0, half, 2, dtype=F32) / half)
    ang_r = (t // GRID_W).astype(F32)[:, None] * freqs[None]
    ang_c = (t % GRID_W).astype(F32)[:, None] * freqs[None]
    cr, sr, cc, sc = jnp.cos(ang_r), jnp.sin(ang_r), jnp.cos(ang_c), jnp.sin(ang_c)
    cos64 = jnp.concatenate([cr, cr, cc, cc], axis=-1)
    sin64 = jnp.concatenate([-sr, sr, -sc, sc], axis=-1)
    return jnp.concatenate([cos64, cos64], axis=-1), jnp.concatenate([sin64, sin64], axis=-1)


def _swiglu_rows(h_bf16, wgu_ref, wd_ref, h1_ref):
    tn = 256
    for c in range(D_FF // tn):
        gg = jnp.dot(h_bf16, wgu_ref[:, c * tn:(c + 1) * tn], preferred_element_type=F32)
        uu = jnp.dot(h_bf16, wgu_ref[:, D_FF + c * tn:D_FF + (c + 1) * tn], preferred_element_type=F32)
        h1_ref[:, c * tn:(c + 1) * tn] = (jax.nn.silu(gg) * uu).astype(BF16)
    return jnp.dot(h1_ref[...], wd_ref[...], preferred_element_type=F32)


def _ffn_kernel(*refs, mixer_out):
    if mixer_out:
        ap_ref, al_ref, wout_ref, y_ref, g_ref, mod_ref, wgu_ref, wd_ref, o_ref, h1_ref = refs
    else:
        y_ref, g_ref, mod_ref, wgu_ref, wd_ref, o_ref, h1_ref = refs

    def body(a_ref):
        y = y_ref[...]
        if a_ref is not None:
            y = y + mod_ref[2:3, :] * jnp.dot(a_ref[...], wout_ref[...], preferred_element_type=F32)
        h = _modnorm(y, g_ref[...], mod_ref[3:4, :], mod_ref[4:5, :]).astype(BF16)
        o_ref[...] = y + mod_ref[5:6, :] * _swiglu_rows(h, wgu_ref, wd_ref, h1_ref)

    if mixer_out:
        _for_stream_of_tile(lambda part: body((ap_ref, al_ref)[part]))
    else:
        body(None)


def _ffn(y, g, mod, wgu_bf16, wd_bf16, slot, mixer_out=None):
    pre_specs, pre_args = [], []
    if mixer_out is not None:
        pre_specs = _mixer_out_specs(*mixer_out)
        pre_args = [*mixer_out[0], mixer_out[1]]
    return pl.pallas_call(
        functools.partial(_ffn_kernel, mixer_out=mixer_out is not None),
        grid=(N_TOK // TM,),
        in_specs=pre_specs + [
            pl.BlockSpec((TM, D), lambda i: (i, 0)),
            pl.BlockSpec((1, D), lambda i: (0, 0)),
            pl.BlockSpec((None, 6, D), lambda i: (_cond_of_tile(i, TM), 0, 0)),
            pl.BlockSpec((None, D, 2 * D_FF), lambda i: (slot, 0, 0), pipeline_mode=pl.Buffered(1)),
            pl.BlockSpec((None, D_FF, D), lambda i: (slot, 0, 0), pipeline_mode=pl.Buffered(1)),
        ],
        out_specs=pl.BlockSpec((TM, D), lambda i: (i, 0)),
        out_shape=jax.ShapeDtypeStruct((N_TOK, D), F32),
        scratch_shapes=[pltpu.VMEM((TM, D_FF), BF16)],
        compiler_params=_cparams(("arbitrary",)),
        name="ffn",
    )(*pre_args, y, g.reshape(1, D), mod, wgu_bf16, wd_bf16)


def _router_kernel(*refs, mixer_out):
    if mixer_out:
        (ap_ref, al_ref, wout_ref, y_ref, g_ref, mod_ref, rt_ref,
         ym_ref, hs_ref, ii_ref, wf_ref, cnt_ref, tb_ref, carry_ref) = refs

        def add_mixer_out(part):
            acc = jnp.dot((ap_ref, al_ref)[part][...], wout_ref[...], preferred_element_type=F32)
            ym_ref[...] = y_ref[...] + mod_ref[2:3, :] * acc

        _for_stream_of_tile(add_mixer_out)
        y = ym_ref[...]
    else:
        y_ref, g_ref, mod_ref, rt_ref, hs_ref, ii_ref, wf_ref, cnt_ref, tb_ref, carry_ref = refs
        y = y_ref[...]
    tm = y_ref.shape[0]

    @pl.when(pl.program_id(0) == 0)
    def _():
        carry_ref[...] = jnp.zeros_like(carry_ref)

    tb_ref[...] = carry_ref[...].astype(I32)

    h = _modnorm(y, g_ref[...], mod_ref[3:4, :], mod_ref[4:5, :])
    for k in range(SLAB):
        hs_ref[:, k, :] = h[:, k * LANE:(k + 1) * LANE]
    logits = lax.dot_general(rt_ref[...], h.astype(BF16), (((1,), (1,)), ((), ())),
                             preferred_element_type=F32)
    eidx = lax.broadcasted_iota(I32, (N_EXP, tm), 0)
    eidx_f = eidx.astype(F32)
    m1 = jnp.max(logits, axis=0, keepdims=True)
    i1 = jnp.min(jnp.where(logits == m1, eidx_f, float(N_EXP)), axis=0, keepdims=True)
    oh1 = eidx_f == i1
    rest = jnp.where(oh1, -jnp.inf, logits)
    m2 = jnp.max(rest, axis=0, keepdims=True)
    i2 = jnp.min(jnp.where(rest == m2, eidx_f, float(N_EXP)), axis=0, keepdims=True)
    oh2 = eidx_f == i2
    i1 = i1.astype(I32)
    i2 = i2.astype(I32)
    e2 = jnp.exp(m2 - m1)
    w1 = 1.0 / (1.0 + e2)
    w2 = e2 / (1.0 + e2)
    sel = jnp.where(oh1 | oh2, 1.0, 0.0)
    before = (lax.broadcasted_iota(I32, (tm, tm), 0) < lax.broadcasted_iota(I32, (tm, tm), 1))
    cum = jnp.dot(sel.astype(BF16), jnp.where(before, 1.0, 0.0).astype(BF16), preferred_element_type=F32)
    cum = cum + carry_ref[:, 0:1]
    r1 = jnp.sum(jnp.where(oh1, cum, 0.0), axis=0, keepdims=True).astype(I32)
    r2 = jnp.sum(jnp.where(oh2, cum, 0.0), axis=0, keepdims=True).astype(I32)
    ii_ref[...] = jnp.where(eidx == 0, i1, jnp.where(eidx == 1, i2, jnp.where(eidx == 2, r1,
                            jnp.where(eidx == 3, r2, 0))))
    wf_ref[...] = jnp.where(eidx == 0, w1, jnp.where(eidx == 1, w2, 0.0))
    carry_ref[...] = carry_ref[...] + jnp.sum(sel, axis=1, keepdims=True)
    cnt_ref[...] = carry_ref[...].astype(I32)


def _router(y, g, mod, router_t_bf16, mixer_out=None):
    pre_specs, pre_args, pre_out_specs, pre_out_shape = [], [], [], []
    if mixer_out is not None:
        pre_specs = _mixer_out_specs(*mixer_out)
        pre_args = [*mixer_out[0], mixer_out[1]]
        pre_out_specs = [pl.BlockSpec((TM, D), lambda i: (i, 0))]
        pre_out_shape = [jax.ShapeDtypeStruct((N_TOK, D), F32)]
    outs = pl.pallas_call(
        functools.partial(_router_kernel, mixer_out=mixer_out is not None),
        grid=(N_TOK // TM,),
        in_specs=pre_specs + [
            pl.BlockSpec((TM, D), lambda i: (i, 0)),
            pl.BlockSpec((1, D), lambda i: (0, 0)),
            pl.BlockSpec((None, 6, D), lambda i: (_cond_of_tile(i, TM), 0, 0)),
            pl.BlockSpec((N_EXP, D), lambda i: (0, 0)),
        ],
        out_specs=pre_out_specs + [
            pl.BlockSpec((TM, SLAB, LANE), lambda i: (i, 0, 0)),
            pl.BlockSpec((N_EXP, TM), lambda i: (0, i)),
            pl.BlockSpec((N_EXP, TM), lambda i: (0, i)),
            pl.BlockSpec((N_EXP, LANE), lambda i: (0, 0)),
            pl.BlockSpec((N_EXP, LANE), lambda i: (i, 0)),
        ],
        out_shape=pre_out_shape + [
            jax.ShapeDtypeStruct((N_TOK, SLAB, LANE), F32),
            jax.ShapeDtypeStruct((N_EXP, N_TOK), I32),
            jax.ShapeDtypeStruct((N_EXP, N_TOK), F32),
            jax.ShapeDtypeStruct((N_EXP, LANE), I32),
            jax.ShapeDtypeStruct((N_TOK // TM * N_EXP, LANE), I32),
        ],
        scratch_shapes=[pltpu.VMEM((N_EXP, LANE), F32)],
        compiler_params=_cparams(("arbitrary",)),
        name="router",
    )(*pre_args, y, g.reshape(1, D), mod, router_t_bf16)
    return tuple(outs) if mixer_out is not None else (y, *outs)


class _SegmentCopies:
    def __init__(self, seg_ref, dst_ref, nch_ref, stage_ref, hbm_ref, sem, *, to_hbm):
        self.tabs = (seg_ref, dst_ref, nch_ref)
        self.stage_ref, self.hbm_ref, self.sem, self.to_hbm = stage_ref, hbm_ref, sem, to_hbm

    def _chunk(self, slot, s_row, d_row):
        st = self.stage_ref.at[slot, pl.ds(s_row, CH)]
        hb = self.hbm_ref.at[pl.ds(d_row, CH)]
        sem = self.sem.at[slot]
        return pltpu.make_async_copy(st, hb, sem) if self.to_hbm else pltpu.make_async_copy(hb, st, sem)

    def start(self, tile):
        seg_ref, dst_ref, nch_ref = self.tabs
        slot = tile % 2
        for e in range(N_EXP):
            k = tile * N_EXP + e
            s0, d0 = seg_ref[k], dst_ref[k]

            def start_chunk(c, carry, s0=s0, d0=d0):
                self._chunk(slot, s0 + c * CH, d0 + c * CH).start()
                return carry

            lax.fori_loop(0, nch_ref[k], start_chunk, 0)

    def wait(self, tile):
        nch_ref = self.tabs[2]
        slot = tile % 2
        total = 0
        for e in range(N_EXP):
            total = total + nch_ref[tile * N_EXP + e]

        def wait_chunk(c, carry):
            self._chunk(slot, 0, 0).wait()
            return carry

        lax.fori_loop(0, total, wait_chunk, 0)


def _dispatch_kernel(l1_ref, l2_ref, seg_ref, dst_ref, nch_ref, zs_ref, zn_ref, hs_ref, xs_hbm, stage_ref, sem):
    i = pl.program_id(0)
    n_tiles = pl.num_programs(0)
    copies = _SegmentCopies(seg_ref, dst_ref, nch_ref, stage_ref, xs_hbm, sem, to_hbm=True)

    @pl.when(i == 0)
    def _():
        stage_ref[...] = jnp.zeros_like(stage_ref)

        def zero_chunk(row):
            return pltpu.make_async_copy(stage_ref.at[0, pl.ds(0, CH)], xs_hbm.at[pl.ds(row, CH)], sem.at[0])

        total = 0
        for z in range(N_EXP + 1):
            z0, n = zs_ref[z], zn_ref[z]

            def start(c, carry, z0=z0):
                zero_chunk(z0 + c * CH).start()
                return carry

            lax.fori_loop(0, n, start, 0)
            total = total + n

        def wait(c, carry):
            zero_chunk(0).wait()
            return carry

        lax.fori_loop(0, total, wait, 0)

    slot = i % 2
    base = i * TM

    def place(t8, carry):
        for u in range(SUBLANE):
            t = t8 * SUBLANE + u
            row = hs_ref[t]
            stage_ref[slot, l1_ref[base + t]] = row
            stage_ref[slot, l2_ref[base + t]] = row
        return carry

    lax.fori_loop(0, TM // SUBLANE, place, 0)
    pl.when(i >= 1)(lambda: copies.wait(i - 1))
    copies.start(i)
    pl.when(i == n_tiles - 1)(lambda: copies.wait(i))


def _dispatch(tabs, zero_tabs, hs):
    return pl.pallas_call(
        _dispatch_kernel,
        grid_spec=pltpu.PrefetchScalarGridSpec(
            num_scalar_prefetch=7,
            grid=(N_TOK // TM,),
            in_specs=[pl.BlockSpec((TM, SLAB, LANE), lambda i, *_: (i, 0, 0))],
            out_specs=pl.BlockSpec(memory_space=pl.ANY),
            scratch_shapes=[pltpu.VMEM((2, STAGE_ROWS, SLAB, LANE), F32), pltpu.SemaphoreType.DMA((2,))],
        ),
        out_shape=jax.ShapeDtypeStruct((MAX_TILES * TM_E, SLAB, LANE), F32),
        compiler_params=_cparams(("arbitrary",)),
        name="moe_dispatch",
    )(*tabs, *zero_tabs, hs)


FF_CHUNK = 256
N_FF_CHUNKS = D_FF // FF_CHUNK


def _expert_kernel(te_ref, tv_ref, first_ref, clo_ref, chi_ref, x_ref, wgu_hbm, wd_hbm, o_ref,
                   wg_ref, wu_ref, wd_ref, sg_ref, su_ref, sd_ref, xb_ref, h1_ref, sem, *, layer):
    i = pl.program_id(0)
    e = te_ref[i]
    cur = e % 2

    def copies(expert, c):
        s = c % 2
        cols = slice(c * FF_CHUNK, (c + 1) * FF_CHUNK)
        ucols = slice(D_FF + c * FF_CHUNK, D_FF + (c + 1) * FF_CHUNK)
        return (pltpu.make_async_copy(wgu_hbm.at[layer, expert, :, cols], sg_ref.at[s], sem.at[s]),
                pltpu.make_async_copy(wgu_hbm.at[layer, expert, :, ucols], su_ref.at[s], sem.at[s]),
                pltpu.make_async_copy(wd_hbm.at[layer, expert, cols, :], sd_ref.at[s], sem.at[s]))

    def fetch(expert, c):
        for cp in copies(expert, c):
            cp.start()

    def convert(expert, c, wslot):
        for cp in copies(expert, c):
            cp.wait()
        s = c % 2
        wg_ref[wslot, c] = sg_ref[s].astype(BF16)
        wu_ref[wslot, c] = su_ref[s].astype(BF16)
        wd_ref[wslot, c * FF_CHUNK:(c + 1) * FF_CHUNK, :] = sd_ref[s].astype(BF16)
        if c + 2 < N_FF_CHUNKS:
            fetch(expert, c + 2)

    @pl.when(i == 0)
    def _():
        fetch(e, 0)
        fetch(e, 1)
        for c in range(N_FF_CHUNKS):
            convert(e, c, cur)

    nxt = jnp.minimum(e + 1, N_EXP - 1)

    @pl.when((first_ref[i] == 1) & (e + 1 < N_EXP))
    def _():
        fetch(nxt, 0)
        fetch(nxt, 1)

    valid = tv_ref[i] == 1

    @pl.when(valid)
    def _():
        for k in range(SLAB):
            xb_ref[:, k * LANE:(k + 1) * LANE] = x_ref[pl.ds(k, TM_E, stride=SLAB), :].astype(BF16)
        xb = xb_ref[...]
        for c in range(N_FF_CHUNKS):
            gg = jnp.dot(xb, wg_ref[cur, c], preferred_element_type=F32)
            uu = jnp.dot(xb, wu_ref[cur, c], preferred_element_type=F32)
            h1_ref[:, c * FF_CHUNK:(c + 1) * FF_CHUNK] = (jax.nn.silu(gg) * uu).astype(BF16)
        acc = jnp.dot(h1_ref[...], wd_ref[cur], preferred_element_type=F32)
        for k in range(SLAB):
            o_ref[:, k, :] = acc[:, k * LANE:(k + 1) * LANE]

    @pl.when(jnp.logical_not(valid))
    def _():
        o_ref[...] = jnp.zeros_like(o_ref)

    for c in range(N_FF_CHUNKS):
        pl.when((clo_ref[i] <= c) & (c < chi_ref[i]))(functools.partial(convert, nxt, c, 1 - cur))


def _experts(tile_tabs, xs, w_gu, w_down, layer):
    return pl.pallas_call(
        functools.partial(_expert_kernel, layer=layer),
        grid_spec=pltpu.PrefetchScalarGridSpec(
            num_scalar_prefetch=5,
            grid=(MAX_TILES,),
            in_specs=[
                pl.BlockSpec((TM_E * SLAB, LANE), lambda i, *_: (i, 0)),
                pl.BlockSpec(memory_space=pl.ANY),
                pl.BlockSpec(memory_space=pl.ANY),
            ],
            out_specs=pl.BlockSpec((TM_E, SLAB, LANE), lambda i, *_: (i, 0, 0)),
            scratch_shapes=[
                pltpu.VMEM((2, N_FF_CHUNKS, D, FF_CHUNK), BF16),
                pltpu.VMEM((2, N_FF_CHUNKS, D, FF_CHUNK), BF16),
                pltpu.VMEM((2, D_FF, D), BF16),
                pltpu.VMEM((2, D, FF_CHUNK), F32),
                pltpu.VMEM((2, D, FF_CHUNK), F32),
                pltpu.VMEM((2, FF_CHUNK, D), F32),
                pltpu.VMEM((TM_E, D), BF16),
                pltpu.VMEM((TM_E, D_FF), BF16),
                pltpu.SemaphoreType.DMA((2,)),
            ],
        ),
        out_shape=jax.ShapeDtypeStruct(xs.shape, F32),
        compiler_params=_cparams(("arbitrary",)),
        name="moe_experts",
    )(*tile_tabs, xs.reshape(-1, LANE), w_gu, w_down)


def _combine_kernel(l1_ref, l2_ref, seg_ref, dst_ref, nch_ref, *refs, final):
    if final:
        y_ref, mod_ref, w_ref, fg_ref, ys_hbm, op_ref, ol_ref, stage_ref, g1_ref, g2_ref, sem, o_ref = refs
    else:
        y_ref, mod_ref, w_ref, ys_hbm, o_ref, stage_ref, g1_ref, g2_ref, sem = refs
    i = pl.program_id(0)
    copies = _SegmentCopies(seg_ref, dst_ref, nch_ref, stage_ref, ys_hbm, sem, to_hbm=False)
    pl.when(i == 0)(lambda: copies.start(i))
    pl.when(i + 1 < pl.num_programs(0))(lambda: copies.start(i + 1))
    copies.wait(i)
    slot = i % 2
    base = i * TM

    def pick(t8, carry):
        for u in range(SUBLANE):
            t = t8 * SUBLANE + u
            rows = pl.ds(pl.multiple_of(t * SLAB, SLAB), SLAB)
            g1_ref[rows, :] = stage_ref[slot, l1_ref[base + t]]
            g2_ref[rows, :] = stage_ref[slot, l2_ref[base + t]]
        return carry

    lax.fori_loop(0, TM // SUBLANE, pick, 0)
    w1 = w_ref[:, 0:1]
    w2 = w_ref[:, 1:2]
    for k in range(SLAB):
        cs = slice(k * LANE, (k + 1) * LANE)
        chunk_k = pl.ds(k, TM, stride=SLAB)
        f = w1 * g1_ref[chunk_k, :] + w2 * g2_ref[chunk_k, :]
        o_ref[:, cs] = y_ref[:, cs] + mod_ref[5:6, cs] * f
    if final:
        yn = o_ref[...]
        out = yn * lax.rsqrt(jnp.mean(yn * yn, axis=-1, keepdims=True) + 1e-6) * fg_ref[...]

        def write(part):
            (op_ref, ol_ref)[part][...] = out

        _for_stream_of_tile(write)


def _combine(tabs, y, mod, w_cols, ys, final_g=None):
    final = final_g is not None
    row_spec = pl.BlockSpec((TM, D), lambda i, *_: (i, 0))
    scratch = [
        pltpu.VMEM((2, STAGE_ROWS, SLAB, LANE), F32),
        pltpu.VMEM((TM * SLAB, LANE), F32),
        pltpu.VMEM((TM * SLAB, LANE), F32),
        pltpu.SemaphoreType.DMA((2,)),
    ]
    if final:
        out_specs = [pl.BlockSpec((TM, D), lambda i, *_: (jnp.minimum(i, N_PROMPT_TILES - 1), 0)),
                     pl.BlockSpec((TM, D), lambda i, *_: (jnp.maximum(i - N_PROMPT_TILES, 0), 0))]
        out_shape = [jax.ShapeDtypeStruct((N_P, D), F32), jax.ShapeDtypeStruct((N_L, D), F32)]
        scratch.append(pltpu.VMEM((TM, D), F32))
        extra_specs, extra_args = [pl.BlockSpec((1, D), lambda i, *_: (0, 0))], [final_g.reshape(1, D)]
    else:
        out_specs, out_shape, extra_specs, extra_args = row_spec, jax.ShapeDtypeStruct((N_TOK, D), F32), [], []
    return pl.pallas_call(
        functools.partial(_combine_kernel, final=final),
        grid_spec=pltpu.PrefetchScalarGridSpec(
            num_scalar_prefetch=5,
            grid=(N_TOK // TM,),
            in_specs=[
                row_spec,
                pl.BlockSpec((None, 6, D), lambda i, *_: (_cond_of_tile(i, TM), 0, 0)),
                pl.BlockSpec((TM, 2), lambda i, *_: (i, 0)),
                *extra_specs,
                pl.BlockSpec(memory_space=pl.ANY),
            ],
            out_specs=out_specs,
            scratch_shapes=scratch,
        ),
        out_shape=out_shape,
        compiler_params=_cparams(("arbitrary",)),
        name="moe_combine",
    )(*tabs, y, mod, w_cols, *extra_args, ys)


def _moe(y, g, mod, router, w_gu, w_down, slot, mixer_out=None, final_g=None):
    y, hs, ii, wf, cnt, tbase = _router(y, g, mod, router.T.astype(BF16), mixer_out)
    n_tt = N_TOK // TM
    counts = cnt[:, 0]
    before = tbase[:, 0].reshape(n_tt, N_EXP)
    n_seg = jnp.concatenate([before[1:], counts[None]], axis=0) - before
    region = (counts + CH + TM_E - 1) // TM_E * TM_E
    region_end = jnp.cumsum(region)
    offs = region_end - region
    seg_pad = (n_seg + CH - 1) // CH * CH
    seg = jnp.cumsum(seg_pad, axis=1) - seg_pad
    dst = offs[None, :] + before
    e_ids = jnp.arange(N_EXP, dtype=I32)[:, None]
    shift = jnp.repeat((seg - before).T, TM, axis=1)
    l1 = jnp.sum(jnp.where(ii[0:1] == e_ids, shift, 0), axis=0) + ii[2]
    l2 = jnp.sum(jnp.where(ii[1:2] == e_ids, shift, 0), axis=0) + ii[3]
    tabs = (l1, l2, seg.reshape(-1), dst.reshape(-1), (seg_pad // CH).reshape(-1))
    row0 = jnp.arange(MAX_TILES, dtype=I32) * TM_E
    tile_expert = jnp.minimum(jnp.sum((row0[:, None] >= region_end[None, :]).astype(I32), axis=1), N_EXP - 1)
    tile_valid = (row0 < (offs + counts)[tile_expert]).astype(I32)
    in_region = row0 < region_end[-1]
    j = (row0 - offs[tile_expert]) // TM_E
    n = (region // TM_E)[tile_expert]
    streams = in_region & (tile_expert < N_EXP - 1)
    tile_tabs = (tile_expert, tile_valid, (in_region & (j == 0)).astype(I32),
                 jnp.where(streams, j * N_FF_CHUNKS // n, 0), jnp.where(streams, (j + 1) * N_FF_CHUNKS // n, 0))
    zero_start = offs + counts // CH * CH
    zero_tabs = (jnp.concatenate([zero_start, region_end[-1:]]),
                 jnp.concatenate([region_end - zero_start, MAX_TILES * TM_E - region_end[-1:]]) // CH)
    xs = _dispatch(tabs, zero_tabs, hs)
    ys = _experts(tile_tabs, xs, w_gu, w_down, slot)
    return _combine(tabs, y, mod, wf[0:2].T, ys, final_g)


def kernel(x_prompt, x_sample, state_rglru, cache_k, cache_v, c, c_ctx, norm_g, ada_w, ada_b,
           gm_w_in, gm_ln_g, gm_w_s, gm_b_s, gm_w_out, rg_w_in, rg_conv_w, rg_conv_b, rg_w_gate,
           rg_b_gate, rg_lambda, rg_w_out, att_w_qkv, att_lambda, att_subln_g, att_w_out,
           ff_w_gu, ff_w_down, moe_router, moe_w_gu, moe_w_down, final_g):
    y = jnp.concatenate([x_prompt.reshape(N_P, D), x_sample.reshape(N_L, D)], axis=0)
    conds = jnp.concatenate([c_ctx[None], c, jnp.zeros((N_COND - 1 - B_L, D), F32)], axis=0)
    mods = _adaln(conds, ada_w, ada_b)
    nb = D_RNN // RNN_BS
    gm_w_in, gm_w_out, rg_w_in, rg_w_out, att_w_qkv, att_w_out, ff_w_gu, ff_w_down = (
        w.astype(BF16) for w in (gm_w_in, gm_w_out, rg_w_in, rg_w_out, att_w_qkv, att_w_out, ff_w_gu, ff_w_down))
    new_rnn = new_k = new_v = None
    for i in range(DEPTH):
        mod = mods[i]
        kind, slot = i % 3, i // 3
        mixer_out = None
        if kind == 0:
            y = _gmlp_mixer(y, norm_g[i, 0], mod, gm_w_in, gm_ln_g[slot], gm_w_s[slot].astype(BF16),
                            gm_b_s[slot].T, gm_w_out, slot)
        elif kind == 1:
            yx = _mod_matmul(y, norm_g[i, 0], mod, rg_w_in, slot, tn=256, gelu_cols=D_RNN)
            wg_cat = rg_w_gate[slot].transpose(2, 3, 0, 1, 4).reshape(nb, RNN_BS, 4 * RNN_BS).astype(BF16)
            bg_cat = rg_b_gate[slot].reshape(2, 2, nb, RNN_BS).transpose(2, 0, 1, 3).reshape(nb, 1, 4 * RNN_BS)
            rg_args = (yx, rg_conv_w[slot], rg_conv_b[slot], wg_cat, bg_cat, rg_lambda[slot])
            a_p, fin = _rglru(*rg_args, jnp.zeros((2, B_P, D_RNN), F32),
                              row0=0, n_batch=B_P, n_seq=SUBLANE, t_len=T_P)
            a_l, _ = _rglru(*rg_args, state_rglru[:, slot].transpose(1, 0, 2),
                            row0=N_P, n_batch=B_L, n_seq=B_L, t_len=T_L)
            new_rnn = fin.transpose(1, 0, 2)[:, None]
            mixer_out = ((a_p, a_l), rg_w_out, slot)
        else:
            lambda_init = 0.8 - 0.6 * math.exp(-0.3 * i)
            qkv = _mod_matmul(y, norm_g[i, 0], mod, att_w_qkv, slot, tn=512, gelu_cols=0)
            att = (qkv, att_lambda[slot], att_subln_g[slot], lambda_init)
            a_p = _attention(*att, row0=0, n_batch=B_P, t_len=T_P, heads=HEADS, tq=T_P)
            a_l = _attention(*att, row0=N_P, n_batch=B_L, t_len=T_L, heads=4, tq=256,
                             ctx_k=cache_k[:, slot].reshape(B_L, PAST, D),
                             ctx_v=cache_v[:, slot].reshape(B_L, PAST, D),
                             rope_tabs=_rope_tables(T_L))
            new_k = qkv[:N_P, D:2 * D].reshape(B_P, 1, T_P, HEADS, 2 * HD)
            new_v = qkv[:N_P, 2 * D:].reshape(B_P, 1, T_P, HEADS, VD)
            mixer_out = ((a_p, a_l), att_w_out, slot)
        fslot = i // 2
        if i % 2 == 0:
            y = _ffn(y, norm_g[i, 1], mod, ff_w_gu, ff_w_down, fslot, mixer_out)
        else:
            y = _moe(y, norm_g[i, 1], mod, moe_router[fslot], moe_w_gu, moe_w_down, fslot, mixer_out,
                     final_g if i == DEPTH - 1 else None)
    y_prompt, y_sample = y
    return (y_prompt.reshape(B_P, T_P, D), y_sample.reshape(B_L, T_L, D), new_rnn, new_k, new_v)
```

```python
import functools
import math

import jax
import jax.numpy as jnp
import numpy as np
from jax import lax
from jax.experimental import pallas as pl
from jax.experimental.pallas import tpu as pltpu

F32 = jnp.float32
BF16 = jnp.bfloat16
I32 = jnp.int32

D = 1024
DEPTH = 4
B_P, T_P = 32, 256
B_L, T_L = 4, 2048
PAST = 256
N_P = B_P * T_P
N_L = B_L * T_L
N_TOK = N_P + N_L
N_COND = 8
GRID_W = 64
CHUNK = 128
GROUPS = 8
D_RNN = 1280
RNN_BS = 128
CONV_W = 4
RGLRU_C = 8.0
HD = 64
VD = 128
HEADS = 8
ROPE_THETA = 10000.0
LOG2_E = 1.4426950408889634
D_FF = 2816
N_EXP = 8

LANE = 128
SUBLANE = 8
VMEM_LIMIT = 56 * 1024 * 1024
TM = 512
TM_E = 256
CH = 32
MAX_TILES = -(-(2 * N_TOK + N_EXP * (CH + TM_E - 1)) // TM_E)
STAGE_ROWS = 2 * TM + N_EXP * CH
SLAB = D // LANE


def _cparams(sem):
    return pltpu.CompilerParams(dimension_semantics=sem, vmem_limit_bytes=VMEM_LIMIT)


def _cond_of_tile(i, tm):
    r0 = i * tm
    return jnp.where(r0 < N_P, 0, 1 + (r0 - N_P) // T_L)


def _modnorm(x, g, shift, scale):
    ms = jnp.mean(x * x, axis=-1, keepdims=True)
    h = x * lax.rsqrt(ms + 1e-6) * g
    return h * (1.0 + scale) + shift


def _adaln_kernel(c_ref, w_ref, b_ref, o_ref):
    s = jax.nn.silu(c_ref[...]).astype(BF16)
    o_ref[...] = jnp.dot(s, w_ref[...].astype(BF16), preferred_element_type=F32) + b_ref[...]


def _adaln(conds, ada_w, ada_b):
    tn = 1536
    n6 = 6 * D
    out = pl.pallas_call(
        _adaln_kernel,
        grid=(DEPTH, n6 // tn),
        in_specs=[
            pl.BlockSpec((N_COND, D), lambda l, j: (0, 0)),
            pl.BlockSpec((None, D, tn), lambda l, j: (l, 0, j)),
            pl.BlockSpec((None, 1, tn), lambda l, j: (l, 0, j)),
        ],
        out_specs=pl.BlockSpec((None, N_COND, tn), lambda l, j: (l, 0, j)),
        out_shape=jax.ShapeDtypeStruct((DEPTH, N_COND, n6), F32),
        compiler_params=_cparams(("arbitrary", "arbitrary")),
        name="adaln",
    )(conds, ada_w, ada_b.reshape(DEPTH, 1, n6))
    return out.reshape(DEPTH, N_COND, 6, D)


def _mod_matmul_kernel(x_ref, g_ref, mod_ref, w_ref, o_ref, *kv_refs, tn, gelu_cols):
    h = _modnorm(x_ref[...], g_ref[...], mod_ref[0:1, :], mod_ref[1:2, :]).astype(BF16)
    for c in range(o_ref.shape[1] // tn):
        acc = jnp.dot(h, w_ref[:, c * tn:(c + 1) * tn], preferred_element_type=F32)
        if c * tn < gelu_cols:
            acc = jax.nn.gelu(acc)
        o_ref[:, c * tn:(c + 1) * tn] = acc.astype(o_ref.dtype)
        if kv_refs and c * tn >= D:
            kv_ref = kv_refs[c * tn // D - 1]
            cols = slice(c * tn % D, c * tn % D + tn)

            @pl.when(pl.program_id(0) < N_PROMPT_TILES)
            def _(kv_ref=kv_ref, cols=cols, acc=acc):
                kv_ref[:, cols] = acc


def _mod_matmul(y, g, mod, w_bf16, slot, *, tn, gelu_cols, prompt_kv=False, out_dtype=F32):
    nout = w_bf16.shape[2]
    out_specs = [pl.BlockSpec((TM, nout), lambda i: (i, 0))]
    out_shape = [jax.ShapeDtypeStruct((N_TOK, nout), out_dtype)]
    if prompt_kv:
        kv_spec = pl.BlockSpec((TM, D), lambda i: (jnp.minimum(i, N_PROMPT_TILES - 1), 0))
        out_specs += [kv_spec, kv_spec]
        out_shape += [jax.ShapeDtypeStruct((N_P, D), F32)] * 2
    outs = pl.pallas_call(
        functools.partial(_mod_matmul_kernel, tn=tn, gelu_cols=gelu_cols),
        grid=(N_TOK // TM,),
        in_specs=[
            pl.BlockSpec((TM, D), lambda i: (i, 0)),
            pl.BlockSpec((1, D), lambda i: (0, 0)),
            pl.BlockSpec((None, 6, D), lambda i: (_cond_of_tile(i, TM), 0, 0)),
            pl.BlockSpec((None, D, nout), lambda i: (slot, 0, 0)),
        ],
        out_specs=out_specs,
        out_shape=out_shape,
        compiler_params=_cparams(("arbitrary",)),
        name="mod_matmul",
    )(y, g.reshape(1, D), mod, w_bf16)
    return outs if prompt_kv else outs[0]


N_PROMPT_TILES = N_P // TM


def _mixer_out_specs(a_parts, w_out_bf16, slot):
    k = a_parts[0].shape[1]
    return [
        pl.BlockSpec((TM, k), lambda i, *_: (jnp.minimum(i, N_PROMPT_TILES - 1), 0)),
        pl.BlockSpec((TM, k), lambda i, *_: (jnp.maximum(i - N_PROMPT_TILES, 0), 0)),
        pl.BlockSpec((None, k, D), lambda i, *_: (slot, 0, 0)),
    ]


def _for_stream_of_tile(body):
    pl.when(pl.program_id(0) < N_PROMPT_TILES)(lambda: body(0))
    pl.when(pl.program_id(0) >= N_PROMPT_TILES)(lambda: body(1))


def _gmlp_kernel(*refs):
    *y_refs, g_ref, mod_ref, win_ref, lng_ref, ws_ref, bs_ref, wout_ref, o_ref, uv_ref, a_ref = refs
    body = functools.partial(_gmlp_tile, g_ref, mod_ref, win_ref, lng_ref, ws_ref, bs_ref, wout_ref,
                             o_ref, uv_ref, a_ref)
    if len(y_refs) == 1:
        body(y_refs[0])
    else:
        _for_stream_of_tile(lambda part: body(y_refs[part]))


def _gmlp_tile(g_ref, mod_ref, win_ref, lng_ref, ws_ref, bs_ref, wout_ref, o_ref, uv_ref, a_ref, y_ref):
    y = y_ref[...]
    h = _modnorm(y, g_ref[...], mod_ref[0:1, :], mod_ref[1:2, :]).astype(BF16)
    tn = 512
    for c in range(2 * D // tn):
        uv = jnp.dot(h, win_ref[:, c * tn:(c + 1) * tn], preferred_element_type=F32)
        uv_ref[:, c * tn:(c + 1) * tn] = jax.nn.gelu(uv)
    v = uv_ref[:, D:]
    vc = v - jnp.mean(v, axis=-1, keepdims=True)
    vn = vc * lax.rsqrt(jnp.mean(vc * vc, axis=-1, keepdims=True) + 1e-5) * lng_ref[...]
    vn = vn.astype(BF16)
    for ci in range(TM // CHUNK):
        r0 = ci * CHUNK
        for g in range(GROUPS):
            c0 = g * LANE
            s = jnp.dot(ws_ref[g], vn[r0:r0 + CHUNK, c0:c0 + LANE], preferred_element_type=F32)
            s = s + bs_ref[:, g:g + 1]
            a_ref[r0:r0 + CHUNK, c0:c0 + LANE] = (uv_ref[r0:r0 + CHUNK, c0:c0 + LANE] * s).astype(BF16)
    o_ref[...] = y + mod_ref[2:3, :] * jnp.dot(a_ref[...], wout_ref[...], preferred_element_type=F32)


def _gmlp_mixer(y_parts, g, mod, w_in_bf16, ln_g, w_s_bf16, b_s_t, w_out_bf16, slot):
    if len(y_parts) == 1:
        y_specs = [pl.BlockSpec((TM, D), lambda i: (i, 0))]
    else:
        y_specs = [pl.BlockSpec((TM, D), lambda i: (jnp.minimum(i, N_PROMPT_TILES - 1), 0)),
                   pl.BlockSpec((TM, D), lambda i: (jnp.maximum(i - N_PROMPT_TILES, 0), 0))]
    return pl.pallas_call(
        _gmlp_kernel,
        grid=(N_TOK // TM,),
        in_specs=y_specs + [
            pl.BlockSpec((1, D), lambda i: (0, 0)),
            pl.BlockSpec((None, 6, D), lambda i: (_cond_of_tile(i, TM), 0, 0)),
            pl.BlockSpec((None, D, 2 * D), lambda i: (slot, 0, 0)),
            pl.BlockSpec((1, D), lambda i: (0, 0)),
            pl.BlockSpec((GROUPS, CHUNK, CHUNK), lambda i: (0, 0, 0)),
            pl.BlockSpec((CHUNK, GROUPS), lambda i: (0, 0)),
            pl.BlockSpec((None, D, D), lambda i: (slot, 0, 0)),
        ],
        out_specs=pl.BlockSpec((TM, D), lambda i: (i, 0)),
        out_shape=jax.ShapeDtypeStruct((N_TOK, D), F32),
        scratch_shapes=[pltpu.VMEM((TM, 2 * D), F32), pltpu.VMEM((TM, D), BF16)],
        compiler_params=_cparams(("arbitrary",)),
        name="gmlp_mixer",
    )(*y_parts, g.reshape(1, D), mod, w_in_bf16, ln_g.reshape(1, D), w_s_bf16, b_s_t, w_out_bf16)


SCAN_UNROLL = 8


def _rglru_kernel(yb_ref, xb_ref, cw_ref, cb_ref, wg_ref, bg_ref, lam_ref, h0_ref,
                  o_ref, fin_ref, af_ref, bf_ref, ab_ref, bb_ref, *, n_seq, t_len, row_chunk):
    stride = t_len + SUBLANE
    tix = lax.broadcasted_iota(I32, (t_len, 1), 0)
    sp_f = jax.nn.softplus(-lam_ref[0:1, :])
    sp_b = jax.nn.softplus(-lam_ref[1:2, :])
    for g in range(n_seq):
        x = xb_ref[g * t_len:(g + 1) * t_len, :]
        xm1 = jnp.where(tix >= 1, pltpu.roll(x, 1, 0), 0.0)
        xp1 = jnp.where(tix <= t_len - 2, pltpu.roll(x, t_len - 1, 0), 0.0)
        xp2 = jnp.where(tix <= t_len - 3, pltpu.roll(x, t_len - 2, 0), 0.0)
        xc = cb_ref[...] + xm1 * cw_ref[0:1, :]
        xc = xc + x * cw_ref[1:2, :]
        xc = xc + xp1 * cw_ref[2:3, :]
        xc = xc + xp2 * cw_ref[3:4, :]
        for r0 in range(0, t_len, row_chunk):
            xcc = xc[r0:r0 + row_chunk]
            gates = jnp.dot(xcc.astype(BF16), wg_ref[...], preferred_element_type=F32) + bg_ref[...]
            d0 = g * stride + r0
            for d, (sp, a_ref, b_ref) in enumerate(((sp_f, af_ref, bf_ref), (sp_b, ab_ref, bb_ref))):
                r = jax.nn.sigmoid(gates[:, (2 * d) * LANE:(2 * d + 1) * LANE])
                i = jax.nn.sigmoid(gates[:, (2 * d + 1) * LANE:(2 * d + 2) * LANE])
                log_a = (-RGLRU_C * r) * sp
                a = jnp.exp(log_a)
                one_minus_a2 = jnp.tanh(-log_a) * (a * a + 1.0)
                a_ref[d0:d0 + row_chunk, :] = a
                b_ref[d0:d0 + row_chunk, :] = jnp.sqrt(one_minus_a2) * (i * xcc)

    def steps(s8, carry):
        hf, hb = carry
        for u in range(SCAN_UNROLL):
            s = s8 * SCAN_UNROLL + u
            tf = pl.ds(s, n_seq, stride=stride)
            tb = pl.ds(t_len - 1 - s, n_seq, stride=stride)
            hf = af_ref[tf, :] * hf + bf_ref[tf, :]
            hb = ab_ref[tb, :] * hb + bb_ref[tb, :]
            bf_ref[tf, :] = hf
            bb_ref[tb, :] = hb
        return hf, hb

    hf, hb = lax.fori_loop(0, t_len // SCAN_UNROLL, steps, (h0_ref[0], h0_ref[1]))
    fin_ref[0] = hf
    fin_ref[1] = hb
    for g in range(n_seq):
        rs = slice(g * stride, g * stride + t_len)
        hsum = bf_ref[rs, :] + bb_ref[rs, :]
        o_ref[g * t_len:(g + 1) * t_len, :] = (hsum * yb_ref[g * t_len:(g + 1) * t_len, :]).astype(BF16)


def _rglru(yx, cw, cb, wg_cat, bg_cat, lam, h0, *, row0, n_batch, n_seq, t_len):
    rows = n_seq * t_len
    nb = D_RNN // RNN_BS
    rb0 = row0 // rows
    kern = functools.partial(_rglru_kernel, n_seq=n_seq, t_len=t_len, row_chunk=min(t_len, 512))
    scan_buf = pltpu.VMEM((n_seq * (t_len + SUBLANE), RNN_BS), F32)
    return pl.pallas_call(
        kern,
        grid=(n_batch // n_seq, nb),
        in_specs=[
            pl.BlockSpec((rows, RNN_BS), lambda s, c: (rb0 + s, c)),
            pl.BlockSpec((rows, RNN_BS), lambda s, c: (rb0 + s, nb + c)),
            pl.BlockSpec((CONV_W, RNN_BS), lambda s, c: (0, c)),
            pl.BlockSpec((1, RNN_BS), lambda s, c: (0, c)),
            pl.BlockSpec((None, RNN_BS, 4 * RNN_BS), lambda s, c: (c, 0, 0)),
            pl.BlockSpec((None, 1, 4 * RNN_BS), lambda s, c: (c, 0, 0)),
            pl.BlockSpec((2, RNN_BS), lambda s, c: (0, c)),
            pl.BlockSpec((2, n_seq, RNN_BS), lambda s, c: (0, s, c)),
        ],
        out_specs=[
            pl.BlockSpec((rows, RNN_BS), lambda s, c: (s, c)),
            pl.BlockSpec((2, n_seq, RNN_BS), lambda s, c: (0, s, c)),
        ],
        out_shape=[
            jax.ShapeDtypeStruct((n_batch * t_len, D_RNN), BF16),
            jax.ShapeDtypeStruct((2, n_batch, D_RNN), F32),
        ],
        scratch_shapes=[scan_buf, scan_buf, scan_buf, scan_buf],
        compiler_params=_cparams(("arbitrary", "arbitrary")),
        name="rglru",
    )(yx, yx, cw, cb.reshape(1, D_RNN), wg_cat, bg_cat, lam, h0)


def _rope(x, cos, sin_signed, first_of_pair):
    partner = jnp.where(first_of_pair, pltpu.roll(x, LANE - 16, 1), pltpu.roll(x, 16, 1))
    return x * cos + partner * sin_signed


def _attn_kernel(*refs, heads, t_len, s_len, rope, lambda_init, mxu_denominator):
    if rope:
        (lam_ref, g_ref, q_ref, k_ref, v_ref, ck_ref, cv_ref, cq_ref, sq_ref, ckk_ref, skk_ref,
         o_ref, kk_ref, vv_ref) = refs
    else:
        lam_ref, g_ref, q_ref, k_ref, v_ref, o_ref, kk_ref, vv_ref = refs
    tq = q_ref.shape[0]
    lane = lax.broadcasted_iota(I32, (1, LANE), 1)
    first_of_pair = (lane & 16) == 0
    map0 = lane < HD

    @pl.when(pl.program_id(2) == 0)
    def _():
        for j in range(heads):
            cs = slice(j * LANE, (j + 1) * LANE)
            k = k_ref[:, cs]
            if rope:
                k = _rope(k.astype(F32), ckk_ref[...], skk_ref[...], first_of_pair)
            kk_ref[j, 0:t_len, :] = k.astype(BF16)
            vv_ref[j, 0:t_len, 0:LANE] = v_ref[:, cs].astype(BF16)
            if s_len > t_len:
                ctx_rows = pl.ds(pl.program_id(1) * heads + j, s_len - t_len, stride=HEADS)
                kk_ref[j, t_len:s_len, :] = ck_ref[ctx_rows, :].astype(BF16)
                vv_ref[j, t_len:s_len, 0:LANE] = cv_ref[ctx_rows, :].astype(BF16)
            if mxu_denominator:
                vv_ref[j, :, LANE:2 * LANE] = jnp.ones((s_len, LANE), BF16)

    lp = lam_ref[...]
    lam = (jnp.exp(jnp.sum(lp[0:1] * lp[1:2], axis=-1, keepdims=True))
           - jnp.exp(jnp.sum(lp[2:3] * lp[3:4], axis=-1, keepdims=True)) + lambda_init)
    for j in range(heads):
        cs = slice(j * LANE, (j + 1) * LANE)
        q = q_ref[:, cs].astype(F32)
        if rope:
            q = _rope(q, cq_ref[...], sq_ref[...], first_of_pair)
        q = q * (HD ** -0.5 * LOG2_E)
        q2 = jnp.concatenate([jnp.where(map0, q, 0.0), jnp.where(map0, 0.0, q)], axis=0).astype(BF16)
        s = lax.dot_general(q2, kk_ref[j], (((1,), (1,)), ((), ())), preferred_element_type=F32)
        e = jnp.exp2(s - jnp.max(s, axis=-1, keepdims=True))
        if mxu_denominator:
            nd = jnp.dot(e.astype(BF16), vv_ref[j], preferred_element_type=F32)
            av = nd[:, 0:LANE] * (1.0 / nd[:, LANE:LANE + 1])
            o = av[0:tq] - lam * av[tq:2 * tq]
        else:
            p = e * (1.0 / jnp.sum(e, axis=-1, keepdims=True))
            w = p[0:tq] - lam * p[tq:2 * tq]
            o = jnp.dot(w.astype(BF16), vv_ref[j], preferred_element_type=F32)
        o = o * lax.rsqrt(jnp.mean(o * o, axis=-1, keepdims=True) + 1e-6) * g_ref[...]
        o_ref[:, cs] = (o * (1.0 - lambda_init)).astype(BF16)


def _attention(qkv, lam_p, subln_g, lambda_init, *, row0, n_batch, t_len, heads, tq, ctx=None, rope_tabs=None):
    rope = rope_tabs is not None
    s_len = t_len + (PAST if rope else 0)
    w = heads * LANE
    nhb = HEADS // heads
    nq = t_len // tq
    rq0 = row0 // tq
    rk0 = row0 // t_len
    mxu_denominator = s_len >= 1024
    kern = functools.partial(_attn_kernel, heads=heads, t_len=t_len, s_len=s_len, rope=rope,
                             lambda_init=lambda_init, mxu_denominator=mxu_denominator)
    v_width = 2 * LANE if mxu_denominator else LANE
    in_specs = [
        pl.BlockSpec((4, HD), lambda b, h, q: (0, 0)),
        pl.BlockSpec((1, VD), lambda b, h, q: (0, 0)),
        pl.BlockSpec((tq, w), lambda b, h, q: (rq0 + b * nq + q, h)),
        pl.BlockSpec((t_len, w), lambda b, h, q: (rk0 + b, nhb + h)),
        pl.BlockSpec((t_len, w), lambda b, h, q: (rk0 + b, 2 * nhb + h)),
    ]
    args = [lam_p, subln_g.reshape(1, VD), qkv, qkv, qkv]
    if rope:
        cos_t, sin_t = rope_tabs
        cache_k, cache_v, slot = ctx
        n_slots = cache_k.shape[1]
        ctx_k = cache_k.reshape(n_batch, n_slots, PAST * HEADS, LANE)
        ctx_v = cache_v.reshape(n_batch, n_slots, PAST * HEADS, LANE)
        in_specs += [
            pl.BlockSpec((None, None, PAST * HEADS, LANE), lambda b, h, q: (b, slot, 0, 0)),
            pl.BlockSpec((None, None, PAST * HEADS, LANE), lambda b, h, q: (b, slot, 0, 0)),
            pl.BlockSpec((tq, LANE), lambda b, h, q: (q, 0)),
            pl.BlockSpec((tq, LANE), lambda b, h, q: (q, 0)),
            pl.BlockSpec((t_len, LANE), lambda b, h, q: (0, 0)),
            pl.BlockSpec((t_len, LANE), lambda b, h, q: (0, 0)),
        ]
        args += [ctx_k, ctx_v, cos_t, sin_t, cos_t, sin_t]
    return pl.pallas_call(
        kern,
        grid=(n_batch, nhb, nq),
        in_specs=in_specs,
        out_specs=pl.BlockSpec((tq, w), lambda b, h, q: (b * nq + q, h)),
        out_shape=jax.ShapeDtypeStruct((n_batch * t_len, D), BF16),
        scratch_shapes=[pltpu.VMEM((heads, s_len, LANE), BF16), pltpu.VMEM((heads, s_len, v_width), BF16)],
        compiler_params=_cparams(("arbitrary", "arbitrary", "arbitrary")),
        name="diff_attn",
    )(*args)


def _rope_tables(t_len):
    t = np.arange(t_len)
    half = HD // 2
    freqs = (ROPE_THETA ** (-np.arange(0, half, 2, dtype=np.float32) / half)).astype(np.float32)
    ang_r = (t // GRID_W).astype(np.float32)[:, None] * freqs[None]
    ang_c = (t % GRID_W).astype(np.float32)[:, None] * freqs[None]
    cr, sr, cc, sc = np.cos(ang_r), np.sin(ang_r), np.cos(ang_c), np.sin(ang_c)
    cos64 = np.concatenate([cr, cr, cc, cc], axis=-1)
    sin64 = np.concatenate([-sr, sr, -sc, sc], axis=-1)
    return (jnp.asarray(np.concatenate([cos64, cos64], axis=-1), F32),
            jnp.asarray(np.concatenate([sin64, sin64], axis=-1), F32))


def _swiglu_rows(h_bf16, wgu_ref, wd_ref, h1_ref):
    tn = 256
    for c in range(D_FF // tn):
        gg = jnp.dot(h_bf16, wgu_ref[:, c * tn:(c + 1) * tn], preferred_element_type=F32)
        uu = jnp.dot(h_bf16, wgu_ref[:, D_FF + c * tn:D_FF + (c + 1) * tn], preferred_element_type=F32)
        h1_ref[:, c * tn:(c + 1) * tn] = (jax.nn.silu(gg) * uu).astype(BF16)
    return jnp.dot(h1_ref[...], wd_ref[...], preferred_element_type=F32)


def _ffn_kernel(*refs, mixer_out):
    if mixer_out:
        ap_ref, al_ref, wout_ref, y_ref, g_ref, mod_ref, wgu_ref, wd_ref, o_ref, h1_ref = refs
    else:
        y_ref, g_ref, mod_ref, wgu_ref, wd_ref, o_ref, h1_ref = refs

    def body(a_ref):
        y = y_ref[...]
        if a_ref is not None:
            y = y + mod_ref[2:3, :] * jnp.dot(a_ref[...], wout_ref[...], preferred_element_type=F32)
        h = _modnorm(y, g_ref[...], mod_ref[3:4, :], mod_ref[4:5, :]).astype(BF16)
        o_ref[...] = y + mod_ref[5:6, :] * _swiglu_rows(h, wgu_ref, wd_ref, h1_ref)

    if mixer_out:
        _for_stream_of_tile(lambda part: body((ap_ref, al_ref)[part]))
    else:
        body(None)


def _ffn(y, g, mod, wgu_bf16, wd_bf16, slot, mixer_out=None):
    pre_specs, pre_args = [], []
    if mixer_out is not None:
        pre_specs = _mixer_out_specs(*mixer_out)
        pre_args = [*mixer_out[0], mixer_out[1]]
    return pl.pallas_call(
        functools.partial(_ffn_kernel, mixer_out=mixer_out is not None),
        grid=(N_TOK // TM,),
        in_specs=pre_specs + [
            pl.BlockSpec((TM, D), lambda i: (i, 0)),
            pl.BlockSpec((1, D), lambda i: (0, 0)),
            pl.BlockSpec((None, 6, D), lambda i: (_cond_of_tile(i, TM), 0, 0)),
            pl.BlockSpec((None, D, 2 * D_FF), lambda i: (slot, 0, 0), pipeline_mode=pl.Buffered(1)),
            pl.BlockSpec((None, D_FF, D), lambda i: (slot, 0, 0), pipeline_mode=pl.Buffered(1)),
        ],
        out_specs=pl.BlockSpec((TM, D), lambda i: (i, 0)),
        out_shape=jax.ShapeDtypeStruct((N_TOK, D), F32),
        scratch_shapes=[pltpu.VMEM((TM, D_FF), BF16)],
        compiler_params=_cparams(("arbitrary",)),
        name="ffn",
    )(*pre_args, y, g.reshape(1, D), mod, wgu_bf16, wd_bf16)


def _router_kernel(*refs, mixer_out):
    if mixer_out:
        (ap_ref, al_ref, wout_ref, y_ref, g_ref, mod_ref, rt_ref,
         ym_ref, hs_ref, ii_ref, wf_ref, cnt_ref, tb_ref, carry_ref, tri_ref) = refs

        def add_mixer_out(part):
            acc = jnp.dot((ap_ref, al_ref)[part][...], wout_ref[...], preferred_element_type=F32)
            ym_ref[...] = y_ref[...] + mod_ref[2:3, :] * acc

        _for_stream_of_tile(add_mixer_out)
        y = ym_ref[...]
    else:
        y_ref, g_ref, mod_ref, rt_ref, hs_ref, ii_ref, wf_ref, cnt_ref, tb_ref, carry_ref, tri_ref = refs
        y = y_ref[...]
    tm = y_ref.shape[0]

    @pl.when(pl.program_id(0) == 0)
    def _():
        carry_ref[...] = jnp.zeros_like(carry_ref)
        before = (lax.broadcasted_iota(I32, (tm, tm), 0) < lax.broadcasted_iota(I32, (tm, tm), 1))
        tri_ref[...] = jnp.where(before, 1.0, 0.0).astype(BF16)

    tb_ref[...] = carry_ref[...].astype(I32)

    h = _modnorm(y, g_ref[...], mod_ref[3:4, :], mod_ref[4:5, :])
    for k in range(SLAB):
        hs_ref[:, k, :] = h[:, k * LANE:(k + 1) * LANE]
    logits = lax.dot_general(rt_ref[...], h.astype(BF16), (((1,), (1,)), ((), ())),
                             preferred_element_type=F32)
    eidx = lax.broadcasted_iota(I32, (N_EXP, tm), 0)
    eidx_f = eidx.astype(F32)
    m1 = jnp.max(logits, axis=0, keepdims=True)
    i1 = jnp.min(jnp.where(logits == m1, eidx_f, float(N_EXP)), axis=0, keepdims=True)
    oh1 = eidx_f == i1
    rest = jnp.where(oh1, -jnp.inf, logits)
    m2 = jnp.max(rest, axis=0, keepdims=True)
    i2 = jnp.min(jnp.where(rest == m2, eidx_f, float(N_EXP)), axis=0, keepdims=True)
    oh2 = eidx_f == i2
    i1 = i1.astype(I32)
    i2 = i2.astype(I32)
    e2 = jnp.exp(m2 - m1)
    w1 = 1.0 / (1.0 + e2)
    w2 = e2 / (1.0 + e2)
    sel = jnp.where(oh1 | oh2, 1.0, 0.0)
    cum = jnp.dot(sel.astype(BF16), tri_ref[...], preferred_element_type=F32)
    cum = cum + carry_ref[:, 0:1]
    r1 = jnp.sum(jnp.where(oh1, cum, 0.0), axis=0, keepdims=True).astype(I32)
    r2 = jnp.sum(jnp.where(oh2, cum, 0.0), axis=0, keepdims=True).astype(I32)
    ii_ref[...] = jnp.where(eidx == 0, i1, jnp.where(eidx == 1, i2, jnp.where(eidx == 2, r1,
                            jnp.where(eidx == 3, r2, 0))))
    wf_ref[...] = jnp.where(eidx == 0, w1, jnp.where(eidx == 1, w2, 0.0))
    carry_ref[...] = carry_ref[...] + jnp.sum(sel, axis=1, keepdims=True)
    cnt_ref[...] = carry_ref[...].astype(I32)


def _router(y, g, mod, router_t_bf16, mixer_out=None):
    pre_specs, pre_args, pre_out_specs, pre_out_shape = [], [], [], []
    if mixer_out is not None:
        pre_specs = _mixer_out_specs(*mixer_out)
        pre_args = [*mixer_out[0], mixer_out[1]]
        pre_out_specs = [pl.BlockSpec((TM, D), lambda i: (i, 0))]
        pre_out_shape = [jax.ShapeDtypeStruct((N_TOK, D), F32)]
    outs = pl.pallas_call(
        functools.partial(_router_kernel, mixer_out=mixer_out is not None),
        grid=(N_TOK // TM,),
        in_specs=pre_specs + [
            pl.BlockSpec((TM, D), lambda i: (i, 0)),
            pl.BlockSpec((1, D), lambda i: (0, 0)),
            pl.BlockSpec((None, 6, D), lambda i: (_cond_of_tile(i, TM), 0, 0)),
            pl.BlockSpec((N_EXP, D), lambda i: (0, 0)),
        ],
        out_specs=pre_out_specs + [
            pl.BlockSpec((TM, SLAB, LANE), lambda i: (i, 0, 0)),
            pl.BlockSpec((N_EXP, TM), lambda i: (0, i)),
            pl.BlockSpec((N_EXP, TM), lambda i: (0, i)),
            pl.BlockSpec((N_EXP, LANE), lambda i: (0, 0)),
            pl.BlockSpec((N_EXP, LANE), lambda i: (i, 0)),
        ],
        out_shape=pre_out_shape + [
            jax.ShapeDtypeStruct((N_TOK, SLAB, LANE), F32),
            jax.ShapeDtypeStruct((N_EXP, N_TOK), I32),
            jax.ShapeDtypeStruct((N_EXP, N_TOK), F32),
            jax.ShapeDtypeStruct((N_EXP, LANE), I32),
            jax.ShapeDtypeStruct((N_TOK // TM * N_EXP, LANE), I32),
        ],
        scratch_shapes=[pltpu.VMEM((N_EXP, LANE), F32), pltpu.VMEM((TM, TM), BF16)],
        compiler_params=_cparams(("arbitrary",)),
        name="router",
    )(*pre_args, y, g.reshape(1, D), mod, router_t_bf16)
    return tuple(outs) if mixer_out is not None else (y, *outs)


class _SegmentCopies:
    def __init__(self, seg_ref, dst_ref, nch_ref, stage_ref, hbm_ref, sem, *, to_hbm):
        self.tabs = (seg_ref, dst_ref, nch_ref)
        self.stage_ref, self.hbm_ref, self.sem, self.to_hbm = stage_ref, hbm_ref, sem, to_hbm

    def _chunk(self, slot, s_row, d_row):
        st = self.stage_ref.at[slot, pl.ds(s_row, CH)]
        hb = self.hbm_ref.at[pl.ds(d_row, CH)]
        sem = self.sem.at[slot]
        return pltpu.make_async_copy(st, hb, sem) if self.to_hbm else pltpu.make_async_copy(hb, st, sem)

    def start(self, tile):
        seg_ref, dst_ref, nch_ref = self.tabs
        slot = tile % 2
        for e in range(N_EXP):
            k = tile * N_EXP + e
            s0, d0 = seg_ref[k], dst_ref[k]

            def start_chunk(c, carry, s0=s0, d0=d0):
                self._chunk(slot, s0 + c * CH, d0 + c * CH).start()
                return carry

            lax.fori_loop(0, nch_ref[k], start_chunk, 0)

    def wait(self, tile):
        nch_ref = self.tabs[2]
        slot = tile % 2
        total = 0
        for e in range(N_EXP):
            total = total + nch_ref[tile * N_EXP + e]

        def wait_chunk(c, carry):
            self._chunk(slot, 0, 0).wait()
            return carry

        lax.fori_loop(0, total, wait_chunk, 0)


def _dispatch_kernel(l1_ref, l2_ref, seg_ref, dst_ref, nch_ref, zs_ref, zn_ref, hs_ref, xs_hbm, stage_ref, sem):
    i = pl.program_id(0)
    n_tiles = pl.num_programs(0)
    copies = _SegmentCopies(seg_ref, dst_ref, nch_ref, stage_ref, xs_hbm, sem, to_hbm=True)

    @pl.when(i == 0)
    def _():
        stage_ref[...] = jnp.zeros_like(stage_ref)

        def zero_chunk(row):
            return pltpu.make_async_copy(stage_ref.at[0, pl.ds(0, CH)], xs_hbm.at[pl.ds(row, CH)], sem.at[0])

        total = 0
        for z in range(N_EXP + 1):
            z0, n = zs_ref[z], zn_ref[z]

            def start(c, carry, z0=z0):
                zero_chunk(z0 + c * CH).start()
                return carry

            lax.fori_loop(0, n, start, 0)
            total = total + n

        def wait(c, carry):
            zero_chunk(0).wait()
            return carry

        lax.fori_loop(0, total, wait, 0)

    slot = i % 2
    base = i * TM

    def place(t8, carry):
        for u in range(SUBLANE):
            t = t8 * SUBLANE + u
            row = hs_ref[t]
            stage_ref[slot, l1_ref[base + t]] = row
            stage_ref[slot, l2_ref[base + t]] = row
        return carry

    lax.fori_loop(0, TM // SUBLANE, place, 0)
    pl.when(i >= 1)(lambda: copies.wait(i - 1))
    copies.start(i)
    pl.when(i == n_tiles - 1)(lambda: copies.wait(i))


def _dispatch(tabs, zero_tabs, hs):
    return pl.pallas_call(
        _dispatch_kernel,
        grid_spec=pltpu.PrefetchScalarGridSpec(
            num_scalar_prefetch=7,
            grid=(N_TOK // TM,),
            in_specs=[pl.BlockSpec((TM, SLAB, LANE), lambda i, *_: (i, 0, 0))],
            out_specs=pl.BlockSpec(memory_space=pl.ANY),
            scratch_shapes=[pltpu.VMEM((2, STAGE_ROWS, SLAB, LANE), F32), pltpu.SemaphoreType.DMA((2,))],
        ),
        out_shape=jax.ShapeDtypeStruct((MAX_TILES * TM_E, SLAB, LANE), F32),
        compiler_params=_cparams(("arbitrary",)),
        name="moe_dispatch",
    )(*tabs, *zero_tabs, hs)


FF_CHUNK = 256
N_FF_CHUNKS = D_FF // FF_CHUNK


def _expert_kernel(te_ref, tv_ref, first_ref, clo_ref, chi_ref, x_ref, wgu_hbm, wd_hbm, o_ref,
                   wg_ref, wu_ref, wd_ref, sg_ref, su_ref, sd_ref, xb_ref, h1_ref, sem, *, layer):
    i = pl.program_id(0)
    e = te_ref[i]
    cur = e % 2

    def copies(expert, c):
        s = c % 2
        cols = slice(c * FF_CHUNK, (c + 1) * FF_CHUNK)
        ucols = slice(D_FF + c * FF_CHUNK, D_FF + (c + 1) * FF_CHUNK)
        return (pltpu.make_async_copy(wgu_hbm.at[layer, expert, :, cols], sg_ref.at[s], sem.at[s]),
                pltpu.make_async_copy(wgu_hbm.at[layer, expert, :, ucols], su_ref.at[s], sem.at[s]),
                pltpu.make_async_copy(wd_hbm.at[layer, expert, cols, :], sd_ref.at[s], sem.at[s]))

    def fetch(expert, c):
        for cp in copies(expert, c):
            cp.start()

    def convert(expert, c, wslot):
        for cp in copies(expert, c):
            cp.wait()
        s = c % 2
        wg_ref[wslot, c] = sg_ref[s].astype(BF16)
        wu_ref[wslot, c] = su_ref[s].astype(BF16)
        wd_ref[wslot, c * FF_CHUNK:(c + 1) * FF_CHUNK, :] = sd_ref[s].astype(BF16)
        if c + 2 < N_FF_CHUNKS:
            fetch(expert, c + 2)

    @pl.when(i == 0)
    def _():
        fetch(e, 0)
        fetch(e, 1)
        for c in range(N_FF_CHUNKS):
            convert(e, c, cur)

    nxt = jnp.minimum(e + 1, N_EXP - 1)

    @pl.when((first_ref[i] == 1) & (e + 1 < N_EXP))
    def _():
        fetch(nxt, 0)
        fetch(nxt, 1)

    valid = tv_ref[i] == 1

    @pl.when(valid)
    def _():
        for k in range(SLAB):
            xb_ref[:, k * LANE:(k + 1) * LANE] = x_ref[pl.ds(k, TM_E, stride=SLAB), :].astype(BF16)
        xb = xb_ref[...]
        for c in range(N_FF_CHUNKS):
            gg = jnp.dot(xb, wg_ref[cur, c], preferred_element_type=F32)
            uu = jnp.dot(xb, wu_ref[cur, c], preferred_element_type=F32)
            h1_ref[:, c * FF_CHUNK:(c + 1) * FF_CHUNK] = (jax.nn.silu(gg) * uu).astype(BF16)
        acc = jnp.dot(h1_ref[...], wd_ref[cur], preferred_element_type=F32)
        for k in range(SLAB):
            o_ref[:, k, :] = acc[:, k * LANE:(k + 1) * LANE]

    @pl.when(jnp.logical_not(valid))
    def _():
        o_ref[...] = jnp.zeros_like(o_ref)

    for c in range(N_FF_CHUNKS):
        pl.when((clo_ref[i] <= c) & (c < chi_ref[i]))(functools.partial(convert, nxt, c, 1 - cur))


def _experts(tile_tabs, xs, w_gu, w_down, layer):
    return pl.pallas_call(
        functools.partial(_expert_kernel, layer=layer),
        grid_spec=pltpu.PrefetchScalarGridSpec(
            num_scalar_prefetch=5,
            grid=(MAX_TILES,),
            in_specs=[
                pl.BlockSpec((TM_E * SLAB, LANE), lambda i, *_: (i, 0)),
                pl.BlockSpec(memory_space=pl.ANY),
                pl.BlockSpec(memory_space=pl.ANY),
            ],
            out_specs=pl.BlockSpec((TM_E, SLAB, LANE), lambda i, *_: (i, 0, 0)),
            scratch_shapes=[
                pltpu.VMEM((2, N_FF_CHUNKS, D, FF_CHUNK), BF16),
                pltpu.VMEM((2, N_FF_CHUNKS, D, FF_CHUNK), BF16),
                pltpu.VMEM((2, D_FF, D), BF16),
                pltpu.VMEM((2, D, FF_CHUNK), F32),
                pltpu.VMEM((2, D, FF_CHUNK), F32),
                pltpu.VMEM((2, FF_CHUNK, D), F32),
                pltpu.VMEM((TM_E, D), BF16),
                pltpu.VMEM((TM_E, D_FF), BF16),
                pltpu.SemaphoreType.DMA((2,)),
            ],
        ),
        out_shape=jax.ShapeDtypeStruct(xs.shape, F32),
        compiler_params=_cparams(("arbitrary",)),
        name="moe_experts",
    )(*tile_tabs, xs.reshape(-1, LANE), w_gu, w_down)


def _combine_kernel(l1_ref, l2_ref, seg_ref, dst_ref, nch_ref, *refs, final):
    if final:
        y_ref, mod_ref, w_ref, fg_ref, ys_hbm, op_ref, ol_ref, stage_ref, g1_ref, g2_ref, sem, o_ref = refs
    else:
        y_ref, mod_ref, w_ref, ys_hbm, o_ref, stage_ref, g1_ref, g2_ref, sem = refs
    i = pl.program_id(0)
    copies = _SegmentCopies(seg_ref, dst_ref, nch_ref, stage_ref, ys_hbm, sem, to_hbm=False)
    pl.when(i == 0)(lambda: copies.start(i))
    pl.when(i + 1 < pl.num_programs(0))(lambda: copies.start(i + 1))
    copies.wait(i)
    slot = i % 2
    base = i * TM

    def pick(t8, carry):
        for u in range(SUBLANE):
            t = t8 * SUBLANE + u
            rows = pl.ds(pl.multiple_of(t * SLAB, SLAB), SLAB)
            g1_ref[rows, :] = stage_ref[slot, l1_ref[base + t]]
            g2_ref[rows, :] = stage_ref[slot, l2_ref[base + t]]
        return carry

    lax.fori_loop(0, TM // SUBLANE, pick, 0)
    w1 = w_ref[:, 0:1]
    w2 = w_ref[:, 1:2]
    for k in range(SLAB):
        cs = slice(k * LANE, (k + 1) * LANE)
        chunk_k = pl.ds(k, TM, stride=SLAB)
        f = w1 * g1_ref[chunk_k, :] + w2 * g2_ref[chunk_k, :]
        o_ref[:, cs] = y_ref[:, cs] + mod_ref[5:6, cs] * f
    if final:
        yn = o_ref[...]
        out = yn * lax.rsqrt(jnp.mean(yn * yn, axis=-1, keepdims=True) + 1e-6) * fg_ref[...]

        def write(part):
            (op_ref, ol_ref)[part][...] = out

        _for_stream_of_tile(write)


def _combine(tabs, y, mod, w_cols, ys, final_g=None):
    final = final_g is not None
    row_spec = pl.BlockSpec((TM, D), lambda i, *_: (i, 0))
    scratch = [
        pltpu.VMEM((2, STAGE_ROWS, SLAB, LANE), F32),
        pltpu.VMEM((TM * SLAB, LANE), F32),
        pltpu.VMEM((TM * SLAB, LANE), F32),
        pltpu.SemaphoreType.DMA((2,)),
    ]
    if final:
        out_specs = [pl.BlockSpec((TM, D), lambda i, *_: (jnp.minimum(i, N_PROMPT_TILES - 1), 0)),
                     pl.BlockSpec((TM, D), lambda i, *_: (jnp.maximum(i - N_PROMPT_TILES, 0), 0))]
        out_shape = [jax.ShapeDtypeStruct((N_P, D), F32), jax.ShapeDtypeStruct((N_L, D), F32)]
        scratch.append(pltpu.VMEM((TM, D), F32))
        extra_specs, extra_args = [pl.BlockSpec((1, D), lambda i, *_: (0, 0))], [final_g.reshape(1, D)]
    else:
        out_specs, out_shape, extra_specs, extra_args = row_spec, jax.ShapeDtypeStruct((N_TOK, D), F32), [], []
    return pl.pallas_call(
        functools.partial(_combine_kernel, final=final),
        grid_spec=pltpu.PrefetchScalarGridSpec(
            num_scalar_prefetch=5,
            grid=(N_TOK // TM,),
            in_specs=[
                row_spec,
                pl.BlockSpec((None, 6, D), lambda i, *_: (_cond_of_tile(i, TM), 0, 0)),
                pl.BlockSpec((TM, 2), lambda i, *_: (i, 0)),
                *extra_specs,
                pl.BlockSpec(memory_space=pl.ANY),
            ],
            out_specs=out_specs,
            scratch_shapes=scratch,
        ),
        out_shape=out_shape,
        compiler_params=_cparams(("arbitrary",)),
        name="moe_combine",
    )(*tabs, y, mod, w_cols, *extra_args, ys)


def _moe(y, g, mod, router, w_gu, w_down, slot, mixer_out=None, final_g=None):
    y, hs, ii, wf, cnt, tbase = _router(y, g, mod, router.T.astype(BF16), mixer_out)
    n_tt = N_TOK // TM
    counts = cnt[:, 0]
    before = tbase[:, 0].reshape(n_tt, N_EXP)
    n_seg = jnp.concatenate([before[1:], counts[None]], axis=0) - before
    region = (counts + CH + TM_E - 1) // TM_E * TM_E
    region_end = jnp.cumsum(region)
    offs = region_end - region
    seg_pad = (n_seg + CH - 1) // CH * CH
    seg = jnp.cumsum(seg_pad, axis=1) - seg_pad
    dst = offs[None, :] + before
    e_ids = jnp.arange(N_EXP, dtype=I32)[:, None]
    shift = jnp.repeat((seg - before).T, TM, axis=1)
    l1 = jnp.sum(jnp.where(ii[0:1] == e_ids, shift, 0), axis=0) + ii[2]
    l2 = jnp.sum(jnp.where(ii[1:2] == e_ids, shift, 0), axis=0) + ii[3]
    tabs = (l1, l2, seg.reshape(-1), dst.reshape(-1), (seg_pad // CH).reshape(-1))
    row0 = jnp.arange(MAX_TILES, dtype=I32) * TM_E
    tile_expert = jnp.minimum(jnp.sum((row0[:, None] >= region_end[None, :]).astype(I32), axis=1), N_EXP - 1)
    tile_valid = (row0 < (offs + counts)[tile_expert]).astype(I32)
    in_region = row0 < region_end[-1]
    j = (row0 - offs[tile_expert]) // TM_E
    n = (region // TM_E)[tile_expert]
    streams = in_region & (tile_expert < N_EXP - 1)
    tile_tabs = (tile_expert, tile_valid, (in_region & (j == 0)).astype(I32),
                 jnp.where(streams, j * N_FF_CHUNKS // n, 0), jnp.where(streams, (j + 1) * N_FF_CHUNKS // n, 0))
    zero_start = offs + counts // CH * CH
    zero_tabs = (jnp.concatenate([zero_start, region_end[-1:]]),
                 jnp.concatenate([region_end - zero_start, MAX_TILES * TM_E - region_end[-1:]]) // CH)
    xs = _dispatch(tabs, zero_tabs, hs)
    ys = _experts(tile_tabs, xs, w_gu, w_down, slot)
    return _combine(tabs, y, mod, wf[0:2].T, ys, final_g)


def kernel(x_prompt, x_sample, state_rglru, cache_k, cache_v, c, c_ctx, norm_g, ada_w, ada_b,
           gm_w_in, gm_ln_g, gm_w_s, gm_b_s, gm_w_out, rg_w_in, rg_conv_w, rg_conv_b, rg_w_gate,
           rg_b_gate, rg_lambda, rg_w_out, att_w_qkv, att_lambda, att_subln_g, att_w_out,
           ff_w_gu, ff_w_down, moe_router, moe_w_gu, moe_w_down, final_g):
    y = (x_prompt.reshape(N_P, D), x_sample.reshape(N_L, D))
    conds = jnp.concatenate([c_ctx[None], c, jnp.zeros((N_COND - 1 - B_L, D), F32)], axis=0)
    mods = _adaln(conds, ada_w, ada_b)
    nb = D_RNN // RNN_BS
    gm_w_in, gm_w_out, rg_w_in, rg_w_out, att_w_qkv, att_w_out, ff_w_gu, ff_w_down = (
        w.astype(BF16) for w in (gm_w_in, gm_w_out, rg_w_in, rg_w_out, att_w_qkv, att_w_out, ff_w_gu, ff_w_down))
    new_rnn = new_k = new_v = None
    for i in range(DEPTH):
        mod = mods[i]
        kind, slot = i % 3, i // 3
        mixer_out = None
        if kind == 0:
            y = _gmlp_mixer(y if i == 0 else (y,), norm_g[i, 0], mod, gm_w_in, gm_ln_g[slot],
                            gm_w_s[slot].astype(BF16), gm_b_s[slot].T, gm_w_out, slot)
        elif kind == 1:
            yx = _mod_matmul(y, norm_g[i, 0], mod, rg_w_in, slot, tn=256, gelu_cols=D_RNN)
            wg_cat = rg_w_gate[slot].transpose(2, 3, 0, 1, 4).reshape(nb, RNN_BS, 4 * RNN_BS).astype(BF16)
            bg_cat = rg_b_gate[slot].reshape(2, 2, nb, RNN_BS).transpose(2, 0, 1, 3).reshape(nb, 1, 4 * RNN_BS)
            rg_args = (yx, rg_conv_w[slot], rg_conv_b[slot], wg_cat, bg_cat, rg_lambda[slot])
            a_p, fin = _rglru(*rg_args, jnp.zeros((2, B_P, D_RNN), F32),
                              row0=0, n_batch=B_P, n_seq=SUBLANE, t_len=T_P)
            a_l, _ = _rglru(*rg_args, state_rglru[:, slot].transpose(1, 0, 2),
                            row0=N_P, n_batch=B_L, n_seq=B_L, t_len=T_L)
            new_rnn = fin.transpose(1, 0, 2)[:, None]
            mixer_out = ((a_p, a_l), rg_w_out, slot)
        else:
            lambda_init = 0.8 - 0.6 * math.exp(-0.3 * i)
            qkv, new_k, new_v = _mod_matmul(y, norm_g[i, 0], mod, att_w_qkv, slot, tn=512, gelu_cols=0,
                                            prompt_kv=True, out_dtype=BF16)
            att = (qkv, att_lambda[slot], att_subln_g[slot], lambda_init)
            a_p = _attention(*att, row0=0, n_batch=B_P, t_len=T_P, heads=HEADS, tq=T_P)
            a_l = _attention(*att, row0=N_P, n_batch=B_L, t_len=T_L, heads=4, tq=256,
                             ctx=(cache_k, cache_v, slot), rope_tabs=_rope_tables(T_L))
            new_k = new_k.reshape(B_P, 1, T_P, HEADS, 2 * HD)
            new_v = new_v.reshape(B_P, 1, T_P, HEADS, VD)
            mixer_out = ((a_p, a_l), att_w_out, slot)
        fslot = i // 2
        if i % 2 == 0:
            y = _ffn(y, norm_g[i, 1], mod, ff_w_gu, ff_w_down, fslot, mixer_out)
        else:
            y = _moe(y, norm_g[i, 1], mod, moe_router[fslot], moe_w_gu, moe_w_down, fslot, mixer_out,
                     final_g if i == DEPTH - 1 else None)
    y_prompt, y_sample = y
    return (y_prompt.reshape(B_P, T_P, D), y_sample.reshape(B_L, T_L, D), new_rnn, new_k, new_v)
```

```python
import functools
import math

import jax
import jax.numpy as jnp
import numpy as np
from jax import lax
from jax.experimental import pallas as pl
from jax.experimental.pallas import tpu as pltpu

F32 = jnp.float32
BF16 = jnp.bfloat16
I32 = jnp.int32

D = 1024
DEPTH = 4
B_P, T_P = 32, 256
B_L, T_L = 4, 2048
PAST = 256
N_P = B_P * T_P
N_L = B_L * T_L
N_TOK = N_P + N_L
N_COND = 8
GRID_W = 64
CHUNK = 128
GROUPS = 8
D_RNN = 1280
RNN_BS = 128
CONV_W = 4
RGLRU_C = 8.0
HD = 64
VD = 128
HEADS = 8
ROPE_THETA = 10000.0
LOG2_E = 1.4426950408889634
D_FF = 2816
N_EXP = 8

LANE = 128
SUBLANE = 8
VMEM_LIMIT = 56 * 1024 * 1024
TM = 512
TM_E = 256
CH = 32
MAX_TILES = -(-(2 * N_TOK + N_EXP * (CH + TM_E - 1)) // TM_E)
STAGE_ROWS = 2 * TM + N_EXP * CH
SLAB = D // LANE


def _cparams(sem):
    return pltpu.CompilerParams(dimension_semantics=sem, vmem_limit_bytes=VMEM_LIMIT)


def _cond_of_tile(i, tm):
    r0 = i * tm
    return jnp.where(r0 < N_P, 0, 1 + (r0 - N_P) // T_L)


def _modnorm(x, g, shift, scale):
    ms = jnp.mean(x * x, axis=-1, keepdims=True)
    h = x * lax.rsqrt(ms + 1e-6) * g
    return h * (1.0 + scale) + shift


def _adaln_kernel(c_ref, w_ref, b_ref, o_ref):
    s = jax.nn.silu(c_ref[...]).astype(BF16)
    o_ref[...] = jnp.dot(s, w_ref[...].astype(BF16), preferred_element_type=F32) + b_ref[...]


def _adaln(conds, ada_w, ada_b):
    tn = 1536
    n6 = 6 * D
    out = pl.pallas_call(
        _adaln_kernel,
        grid=(DEPTH, n6 // tn),
        in_specs=[
            pl.BlockSpec((N_COND, D), lambda l, j: (0, 0)),
            pl.BlockSpec((None, D, tn), lambda l, j: (l, 0, j)),
            pl.BlockSpec((None, 1, tn), lambda l, j: (l, 0, j)),
        ],
        out_specs=pl.BlockSpec((None, N_COND, tn), lambda l, j: (l, 0, j)),
        out_shape=jax.ShapeDtypeStruct((DEPTH, N_COND, n6), F32),
        compiler_params=_cparams(("arbitrary", "arbitrary")),
        name="adaln",
    )(conds, ada_w, ada_b.reshape(DEPTH, 1, n6))
    return out.reshape(DEPTH, N_COND, 6, D)


def _mod_matmul_kernel(x_ref, g_ref, mod_ref, w_ref, o_ref, *, tn, gelu_cols):
    h = _modnorm(x_ref[...], g_ref[...], mod_ref[0:1, :], mod_ref[1:2, :]).astype(BF16)
    for c in range(o_ref.shape[1] // tn):
        acc = jnp.dot(h, w_ref[:, c * tn:(c + 1) * tn], preferred_element_type=F32)
        if c * tn < gelu_cols:
            acc = jax.nn.gelu(acc)
        o_ref[:, c * tn:(c + 1) * tn] = acc


def _mod_matmul(y, g, mod, w_bf16, slot, *, tn, gelu_cols):
    nout = w_bf16.shape[2]
    return pl.pallas_call(
        functools.partial(_mod_matmul_kernel, tn=tn, gelu_cols=gelu_cols),
        grid=(N_TOK // TM,),
        in_specs=[
            pl.BlockSpec((TM, D), lambda i: (i, 0)),
            pl.BlockSpec((1, D), lambda i: (0, 0)),
            pl.BlockSpec((None, 6, D), lambda i: (_cond_of_tile(i, TM), 0, 0)),
            pl.BlockSpec((None, D, nout), lambda i: (slot, 0, 0)),
        ],
        out_specs=pl.BlockSpec((TM, nout), lambda i: (i, 0)),
        out_shape=jax.ShapeDtypeStruct((N_TOK, nout), F32),
        compiler_params=_cparams(("arbitrary",)),
        name="mod_matmul",
    )(y, g.reshape(1, D), mod, w_bf16)


N_PROMPT_TILES = N_P // TM


def _mixer_out_specs(a_parts, w_out_bf16, slot):
    k = a_parts[0].shape[1]
    return [
        pl.BlockSpec((TM, k), lambda i, *_: (jnp.minimum(i, N_PROMPT_TILES - 1), 0)),
        pl.BlockSpec((TM, k), lambda i, *_: (jnp.maximum(i - N_PROMPT_TILES, 0), 0)),
        pl.BlockSpec((None, k, D), lambda i, *_: (slot, 0, 0)),
    ]


def _for_stream_of_tile(body):
    pl.when(pl.program_id(0) < N_PROMPT_TILES)(lambda: body(0))
    pl.when(pl.program_id(0) >= N_PROMPT_TILES)(lambda: body(1))


def _gmlp_kernel(*refs):
    *y_refs, g_ref, mod_ref, win_ref, lng_ref, ws_ref, bs_ref, wout_ref, o_ref, uv_ref, a_ref = refs
    body = functools.partial(_gmlp_tile, g_ref, mod_ref, win_ref, lng_ref, ws_ref, bs_ref, wout_ref,
                             o_ref, uv_ref, a_ref)
    if len(y_refs) == 1:
        body(y_refs[0])
    else:
        _for_stream_of_tile(lambda part: body(y_refs[part]))


def _gmlp_tile(g_ref, mod_ref, win_ref, lng_ref, ws_ref, bs_ref, wout_ref, o_ref, uv_ref, a_ref, y_ref):
    y = y_ref[...]
    h = _modnorm(y, g_ref[...], mod_ref[0:1, :], mod_ref[1:2, :]).astype(BF16)
    tn = 512
    for c in range(2 * D // tn):
        uv = jnp.dot(h, win_ref[:, c * tn:(c + 1) * tn], preferred_element_type=F32)
        uv_ref[:, c * tn:(c + 1) * tn] = jax.nn.gelu(uv)
    v = uv_ref[:, D:]
    vc = v - jnp.mean(v, axis=-1, keepdims=True)
    vn = vc * lax.rsqrt(jnp.mean(vc * vc, axis=-1, keepdims=True) + 1e-5) * lng_ref[...]
    vn = vn.astype(BF16)
    for ci in range(TM // CHUNK):
        r0 = ci * CHUNK
        for g in range(GROUPS):
            c0 = g * LANE
            s = jnp.dot(ws_ref[g], vn[r0:r0 + CHUNK, c0:c0 + LANE], preferred_element_type=F32)
            s = s + bs_ref[:, g:g + 1]
            a_ref[r0:r0 + CHUNK, c0:c0 + LANE] = (uv_ref[r0:r0 + CHUNK, c0:c0 + LANE] * s).astype(BF16)
    o_ref[...] = y + mod_ref[2:3, :] * jnp.dot(a_ref[...], wout_ref[...], preferred_element_type=F32)


def _gmlp_mixer(y_parts, g, mod, w_in_bf16, ln_g, w_s_bf16, b_s_t, w_out_bf16, slot):
    if len(y_parts) == 1:
        y_specs = [pl.BlockSpec((TM, D), lambda i: (i, 0))]
    else:
        y_specs = [pl.BlockSpec((TM, D), lambda i: (jnp.minimum(i, N_PROMPT_TILES - 1), 0)),
                   pl.BlockSpec((TM, D), lambda i: (jnp.maximum(i - N_PROMPT_TILES, 0), 0))]
    return pl.pallas_call(
        _gmlp_kernel,
        grid=(N_TOK // TM,),
        in_specs=y_specs + [
            pl.BlockSpec((1, D), lambda i: (0, 0)),
            pl.BlockSpec((None, 6, D), lambda i: (_cond_of_tile(i, TM), 0, 0)),
            pl.BlockSpec((None, D, 2 * D), lambda i: (slot, 0, 0)),
            pl.BlockSpec((1, D), lambda i: (0, 0)),
            pl.BlockSpec((GROUPS, CHUNK, CHUNK), lambda i: (0, 0, 0)),
            pl.BlockSpec((CHUNK, GROUPS), lambda i: (0, 0)),
            pl.BlockSpec((None, D, D), lambda i: (slot, 0, 0)),
        ],
        out_specs=pl.BlockSpec((TM, D), lambda i: (i, 0)),
        out_shape=jax.ShapeDtypeStruct((N_TOK, D), F32),
        scratch_shapes=[pltpu.VMEM((TM, 2 * D), F32), pltpu.VMEM((TM, D), BF16)],
        compiler_params=_cparams(("arbitrary",)),
        name="gmlp_mixer",
    )(*y_parts, g.reshape(1, D), mod, w_in_bf16, ln_g.reshape(1, D), w_s_bf16, b_s_t, w_out_bf16)


SCAN_UNROLL = 8


def _rglru_kernel(yb_ref, xb_ref, cw_ref, cb_ref, wg_ref, bg_ref, lam_ref, h0_ref,
                  o_ref, fin_ref, af_ref, bf_ref, ab_ref, bb_ref, *, n_seq, t_len, row_chunk):
    stride = t_len + SUBLANE
    tix = lax.broadcasted_iota(I32, (t_len, 1), 0)
    sp_f = jax.nn.softplus(-lam_ref[0:1, :])
    sp_b = jax.nn.softplus(-lam_ref[1:2, :])
    for g in range(n_seq):
        x = xb_ref[g * t_len:(g + 1) * t_len, :]
        xm1 = jnp.where(tix >= 1, pltpu.roll(x, 1, 0), 0.0)
        xp1 = jnp.where(tix <= t_len - 2, pltpu.roll(x, t_len - 1, 0), 0.0)
        xp2 = jnp.where(tix <= t_len - 3, pltpu.roll(x, t_len - 2, 0), 0.0)
        xc = cb_ref[...] + xm1 * cw_ref[0:1, :]
        xc = xc + x * cw_ref[1:2, :]
        xc = xc + xp1 * cw_ref[2:3, :]
        xc = xc + xp2 * cw_ref[3:4, :]
        for r0 in range(0, t_len, row_chunk):
            xcc = xc[r0:r0 + row_chunk]
            gates = jnp.dot(xcc.astype(BF16), wg_ref[...], preferred_element_type=F32) + bg_ref[...]
            d0 = g * stride + r0
            for d, (sp, a_ref, b_ref) in enumerate(((sp_f, af_ref, bf_ref), (sp_b, ab_ref, bb_ref))):
                r = jax.nn.sigmoid(gates[:, (2 * d) * LANE:(2 * d + 1) * LANE])
                i = jax.nn.sigmoid(gates[:, (2 * d + 1) * LANE:(2 * d + 2) * LANE])
                log_a = (-RGLRU_C * r) * sp
                a = jnp.exp(log_a)
                one_minus_a2 = jnp.tanh(-log_a) * (a * a + 1.0)
                a_ref[d0:d0 + row_chunk, :] = a
                b_ref[d0:d0 + row_chunk, :] = jnp.sqrt(one_minus_a2) * (i * xcc)

    def block_scan(a_ref, b_ref, rows, h_in):
        a = [a_ref[r, :] for r in rows]
        b = [b_ref[r, :] for r in rows]
        d = 1
        while d < SCAN_UNROLL:
            a, b = ([a[u] if u < d else a[u] * a[u - d] for u in range(SCAN_UNROLL)],
                    [b[u] if u < d else a[u] * b[u - d] + b[u] for u in range(SCAN_UNROLL)])
            d *= 2
        h = [a[u] * h_in + b[u] for u in range(SCAN_UNROLL)]
        for r, hu in zip(rows, h):
            b_ref[r, :] = hu
        return h[-1]

    def steps(s8, carry):
        hf, hb = carry
        s0 = s8 * SCAN_UNROLL
        hf = block_scan(af_ref, bf_ref, [pl.ds(s0 + u, n_seq, stride=stride) for u in range(SCAN_UNROLL)], hf)
        hb = block_scan(ab_ref, bb_ref,
                        [pl.ds(t_len - 1 - s0 - u, n_seq, stride=stride) for u in range(SCAN_UNROLL)], hb)
        return hf, hb

    hf, hb = lax.fori_loop(0, t_len // SCAN_UNROLL, steps, (h0_ref[0], h0_ref[1]))
    fin_ref[0] = hf
    fin_ref[1] = hb
    for g in range(n_seq):
        rs = slice(g * stride, g * stride + t_len)
        hsum = bf_ref[rs, :] + bb_ref[rs, :]
        o_ref[g * t_len:(g + 1) * t_len, :] = (hsum * yb_ref[g * t_len:(g + 1) * t_len, :]).astype(BF16)


def _rglru(yx, cw, cb, wg_cat, bg_cat, lam, h0, *, row0, n_batch, n_seq, t_len):
    rows = n_seq * t_len
    nb = D_RNN // RNN_BS
    rb0 = row0 // rows
    kern = functools.partial(_rglru_kernel, n_seq=n_seq, t_len=t_len, row_chunk=min(t_len, 512))
    scan_buf = pltpu.VMEM((n_seq * (t_len + SUBLANE), RNN_BS), F32)
    return pl.pallas_call(
        kern,
        grid=(n_batch // n_seq, nb),
        in_specs=[
            pl.BlockSpec((rows, RNN_BS), lambda s, c: (rb0 + s, c)),
            pl.BlockSpec((rows, RNN_BS), lambda s, c: (rb0 + s, nb + c)),
            pl.BlockSpec((CONV_W, RNN_BS), lambda s, c: (0, c)),
            pl.BlockSpec((1, RNN_BS), lambda s, c: (0, c)),
            pl.BlockSpec((None, RNN_BS, 4 * RNN_BS), lambda s, c: (c, 0, 0)),
            pl.BlockSpec((None, 1, 4 * RNN_BS), lambda s, c: (c, 0, 0)),
            pl.BlockSpec((2, RNN_BS), lambda s, c: (0, c)),
            pl.BlockSpec((2, n_seq, RNN_BS), lambda s, c: (0, s, c)),
        ],
        out_specs=[
            pl.BlockSpec((rows, RNN_BS), lambda s, c: (s, c)),
            pl.BlockSpec((2, n_seq, RNN_BS), lambda s, c: (0, s, c)),
        ],
        out_shape=[
            jax.ShapeDtypeStruct((n_batch * t_len, D_RNN), BF16),
            jax.ShapeDtypeStruct((2, n_batch, D_RNN), F32),
        ],
        scratch_shapes=[scan_buf, scan_buf, scan_buf, scan_buf],
        compiler_params=_cparams(("arbitrary", "arbitrary")),
        name="rglru",
    )(yx, yx, cw, cb.reshape(1, D_RNN), wg_cat, bg_cat, lam, h0)


def _rope(x, cos, sin_signed, first_of_pair):
    partner = jnp.where(first_of_pair, pltpu.roll(x, LANE - 16, 1), pltpu.roll(x, 16, 1))
    return x * cos + partner * sin_signed


def _attn_kernel(*refs, heads, t_len, tq, s_len, rope, lambda_init, mxu_denominator):
    if rope:
        (lam_ref, g_ref, y_ref, ng_ref, mod_ref, wq_ref, wk_ref, wv_ref, ck_ref, cv_ref, cq_ref, sq_ref,
         ckk_ref, skk_ref, o_ref, h_ref, kk_ref, vv_ref) = refs
    else:
        (lam_ref, g_ref, y_ref, ng_ref, mod_ref, wq_ref, wk_ref, wv_ref,
         o_ref, ko_ref, vo_ref, h_ref, kk_ref, vv_ref) = refs
    lane = lax.broadcasted_iota(I32, (1, LANE), 1)
    first_of_pair = (lane & 16) == 0
    map0 = lane < HD

    @pl.when((pl.program_id(1) == 0) & (pl.program_id(2) == 0))
    def _():
        h_ref[...] = _modnorm(y_ref[...], ng_ref[...], mod_ref[0:1, :], mod_ref[1:2, :]).astype(BF16)

    @pl.when(pl.program_id(2) == 0)
    def _():
        k_all = jnp.dot(h_ref[...], wk_ref[...], preferred_element_type=F32)
        v_all = jnp.dot(h_ref[...], wv_ref[...], preferred_element_type=F32)
        if not rope:
            ko_ref[...] = k_all
            vo_ref[...] = v_all
        for j in range(heads):
            cs = slice(j * LANE, (j + 1) * LANE)
            k = k_all[:, cs]
            if rope:
                k = _rope(k, ckk_ref[...], skk_ref[...], first_of_pair)
            kk_ref[j, 0:t_len, :] = k.astype(BF16)
            vv_ref[j, 0:t_len, 0:LANE] = v_all[:, cs].astype(BF16)
            if s_len > t_len:
                ctx_rows = pl.ds(pl.program_id(1) * heads + j, s_len - t_len, stride=HEADS)
                kk_ref[j, t_len:s_len, :] = ck_ref[ctx_rows, :].astype(BF16)
                vv_ref[j, t_len:s_len, 0:LANE] = cv_ref[ctx_rows, :].astype(BF16)
            if mxu_denominator:
                vv_ref[j, :, LANE:2 * LANE] = jnp.ones((s_len, LANE), BF16)

    lp = lam_ref[...]
    lam = (jnp.exp(jnp.sum(lp[0:1] * lp[1:2], axis=-1, keepdims=True))
           - jnp.exp(jnp.sum(lp[2:3] * lp[3:4], axis=-1, keepdims=True)) + lambda_init)
    q_rows = pl.ds(pl.multiple_of(pl.program_id(2) * tq, tq), tq)
    q_all = jnp.dot(h_ref[q_rows, :], wq_ref[...], preferred_element_type=F32)
    for j in range(heads):
        cs = slice(j * LANE, (j + 1) * LANE)
        q = q_all[:, cs]
        if rope:
            q = _rope(q, cq_ref[...], sq_ref[...], first_of_pair)
        q = q * (HD ** -0.5 * LOG2_E)
        q2 = jnp.concatenate([jnp.where(map0, q, 0.0), jnp.where(map0, 0.0, q)], axis=0).astype(BF16)
        s = lax.dot_general(q2, kk_ref[j], (((1,), (1,)), ((), ())), preferred_element_type=F32)
        e = jnp.exp2(s - jnp.max(s, axis=-1, keepdims=True))
        if mxu_denominator:
            nd = jnp.dot(e.astype(BF16), vv_ref[j], preferred_element_type=F32)
            av = nd[:, 0:LANE] * (1.0 / nd[:, LANE:LANE + 1])
            o = av[0:tq] - lam * av[tq:2 * tq]
        else:
            p = e * (1.0 / jnp.sum(e, axis=-1, keepdims=True))
            w = p[0:tq] - lam * p[tq:2 * tq]
            o = jnp.dot(w.astype(BF16), vv_ref[j], preferred_element_type=F32)
        o = o * lax.rsqrt(jnp.mean(o * o, axis=-1, keepdims=True) + 1e-6) * g_ref[...]
        o_ref[:, cs] = (o * (1.0 - lambda_init)).astype(BF16)


def _attention(y, norm_g, mod, w_qkv_bf16, wslot, lam_p, subln_g, lambda_init, *, row0, n_batch, t_len, heads, tq,
               ctx=None, rope_tabs=None):
    rope = rope_tabs is not None
    s_len = t_len + (PAST if rope else 0)
    w = heads * LANE
    nhb = HEADS // heads
    nq = t_len // tq
    rk0 = row0 // t_len
    cond0 = 1 if rope else 0
    mxu_denominator = s_len >= 1024
    kern = functools.partial(_attn_kernel, heads=heads, t_len=t_len, tq=tq, s_len=s_len, rope=rope,
                             lambda_init=lambda_init, mxu_denominator=mxu_denominator)
    v_width = 2 * LANE if mxu_denominator else LANE
    in_specs = [
        pl.BlockSpec((4, HD), lambda b, h, q: (0, 0)),
        pl.BlockSpec((1, VD), lambda b, h, q: (0, 0)),
        pl.BlockSpec((t_len, D), lambda b, h, q: (rk0 + b, 0)),
        pl.BlockSpec((1, D), lambda b, h, q: (0, 0)),
        pl.BlockSpec((None, 6, D), lambda b, h, q: (b * cond0 + cond0, 0, 0)),
        pl.BlockSpec((None, D, w), lambda b, h, q: (wslot, 0, h)),
        pl.BlockSpec((None, D, w), lambda b, h, q: (wslot, 0, nhb + h)),
        pl.BlockSpec((None, D, w), lambda b, h, q: (wslot, 0, 2 * nhb + h)),
    ]
    args = [lam_p, subln_g.reshape(1, VD), y, norm_g.reshape(1, D), mod, w_qkv_bf16, w_qkv_bf16, w_qkv_bf16]
    row_block = pl.BlockSpec((tq, w), lambda b, h, q: (b * nq + q, h))
    out_specs, out_shape = row_block, jax.ShapeDtypeStruct((n_batch * t_len, D), BF16)
    if not rope:
        seq_block = pl.BlockSpec((t_len, w), lambda b, h, q: (b, h))
        out_specs = [row_block, seq_block, seq_block]
        out_shape = [out_shape] + [jax.ShapeDtypeStruct((n_batch * t_len, D), F32)] * 2
    if rope:
        cos_t, sin_t = rope_tabs
        cache_k, cache_v, slot = ctx
        n_slots = cache_k.shape[1]
        ctx_k = cache_k.reshape(n_batch, n_slots, PAST * HEADS, LANE)
        ctx_v = cache_v.reshape(n_batch, n_slots, PAST * HEADS, LANE)
        in_specs += [
            pl.BlockSpec((None, None, PAST * HEADS, LANE), lambda b, h, q: (b, slot, 0, 0)),
            pl.BlockSpec((None, None, PAST * HEADS, LANE), lambda b, h, q: (b, slot, 0, 0)),
            pl.BlockSpec((tq, LANE), lambda b, h, q: (q, 0)),
            pl.BlockSpec((tq, LANE), lambda b, h, q: (q, 0)),
            pl.BlockSpec((t_len, LANE), lambda b, h, q: (0, 0)),
            pl.BlockSpec((t_len, LANE), lambda b, h, q: (0, 0)),
        ]
        args += [ctx_k, ctx_v, cos_t, sin_t, cos_t, sin_t]
    return pl.pallas_call(
        kern,
        grid=(n_batch, nhb, nq),
        in_specs=in_specs,
        out_specs=out_specs,
        out_shape=out_shape,
        scratch_shapes=[pltpu.VMEM((t_len, D), BF16), pltpu.VMEM((heads, s_len, LANE), BF16),
                        pltpu.VMEM((heads, s_len, v_width), BF16)],
        compiler_params=_cparams(("arbitrary", "arbitrary", "arbitrary")),
        name="diff_attn",
    )(*args)


def _rope_tables(t_len):
    t = np.arange(t_len)
    half = HD // 2
    freqs = (ROPE_THETA ** (-np.arange(0, half, 2, dtype=np.float32) / half)).astype(np.float32)
    ang_r = (t // GRID_W).astype(np.float32)[:, None] * freqs[None]
    ang_c = (t % GRID_W).astype(np.float32)[:, None] * freqs[None]
    cr, sr, cc, sc = np.cos(ang_r), np.sin(ang_r), np.cos(ang_c), np.sin(ang_c)
    cos64 = np.concatenate([cr, cr, cc, cc], axis=-1)
    sin64 = np.concatenate([-sr, sr, -sc, sc], axis=-1)
    return (jnp.asarray(np.concatenate([cos64, cos64], axis=-1), F32),
            jnp.asarray(np.concatenate([sin64, sin64], axis=-1), F32))


def _swiglu_rows(h_bf16, wgu_ref, wd_ref, h1_ref):
    tn = 256
    for c in range(D_FF // tn):
        gg = jnp.dot(h_bf16, wgu_ref[:, c * tn:(c + 1) * tn], preferred_element_type=F32)
        uu = jnp.dot(h_bf16, wgu_ref[:, D_FF + c * tn:D_FF + (c + 1) * tn], preferred_element_type=F32)
        h1_ref[:, c * tn:(c + 1) * tn] = (jax.nn.silu(gg) * uu).astype(BF16)
    return jnp.dot(h1_ref[...], wd_ref[...], preferred_element_type=F32)


def _ffn_kernel(*refs, mixer_out):
    if mixer_out:
        ap_ref, al_ref, wout_ref, y_ref, g_ref, mod_ref, wgu_ref, wd_ref, o_ref, h1_ref = refs
    else:
        y_ref, g_ref, mod_ref, wgu_ref, wd_ref, o_ref, h1_ref = refs

    def body(a_ref):
        y = y_ref[...]
        if a_ref is not None:
            y = y + mod_ref[2:3, :] * jnp.dot(a_ref[...], wout_ref[...], preferred_element_type=F32)
        h = _modnorm(y, g_ref[...], mod_ref[3:4, :], mod_ref[4:5, :]).astype(BF16)
        o_ref[...] = y + mod_ref[5:6, :] * _swiglu_rows(h, wgu_ref, wd_ref, h1_ref)

    if mixer_out:
        _for_stream_of_tile(lambda part: body((ap_ref, al_ref)[part]))
    else:
        body(None)


def _ffn(y, g, mod, wgu_bf16, wd_bf16, slot, mixer_out=None):
    pre_specs, pre_args = [], []
    if mixer_out is not None:
        pre_specs = _mixer_out_specs(*mixer_out)
        pre_args = [*mixer_out[0], mixer_out[1]]
    return pl.pallas_call(
        functools.partial(_ffn_kernel, mixer_out=mixer_out is not None),
        grid=(N_TOK // TM,),
        in_specs=pre_specs + [
            pl.BlockSpec((TM, D), lambda i: (i, 0)),
            pl.BlockSpec((1, D), lambda i: (0, 0)),
            pl.BlockSpec((None, 6, D), lambda i: (_cond_of_tile(i, TM), 0, 0)),
            pl.BlockSpec((None, D, 2 * D_FF), lambda i: (slot, 0, 0), pipeline_mode=pl.Buffered(1)),
            pl.BlockSpec((None, D_FF, D), lambda i: (slot, 0, 0), pipeline_mode=pl.Buffered(1)),
        ],
        out_specs=pl.BlockSpec((TM, D), lambda i: (i, 0)),
        out_shape=jax.ShapeDtypeStruct((N_TOK, D), F32),
        scratch_shapes=[pltpu.VMEM((TM, D_FF), BF16)],
        compiler_params=_cparams(("arbitrary",)),
        name="ffn",
    )(*pre_args, y, g.reshape(1, D), mod, wgu_bf16, wd_bf16)


def _router_kernel(*refs, mixer_out):
    if mixer_out:
        (ap_ref, al_ref, wout_ref, y_ref, g_ref, mod_ref, rt_ref,
         ym_ref, hs_ref, ii_ref, wf_ref, cnt_ref, tb_ref, carry_ref, tri_ref) = refs

        def add_mixer_out(part):
            acc = jnp.dot((ap_ref, al_ref)[part][...], wout_ref[...], preferred_element_type=F32)
            ym_ref[...] = y_ref[...] + mod_ref[2:3, :] * acc

        _for_stream_of_tile(add_mixer_out)
        y = ym_ref[...]
    else:
        y_ref, g_ref, mod_ref, rt_ref, hs_ref, ii_ref, wf_ref, cnt_ref, tb_ref, carry_ref, tri_ref = refs
        y = y_ref[...]
    tm = y_ref.shape[0]

    @pl.when(pl.program_id(0) == 0)
    def _():
        carry_ref[...] = jnp.zeros_like(carry_ref)
        before = (lax.broadcasted_iota(I32, (tm, tm), 0) < lax.broadcasted_iota(I32, (tm, tm), 1))
        tri_ref[...] = jnp.where(before, 1.0, 0.0).astype(BF16)

    tb_ref[...] = carry_ref[...].astype(I32)

    h = _modnorm(y, g_ref[...], mod_ref[3:4, :], mod_ref[4:5, :])
    for k in range(SLAB):
        hs_ref[:, k, :] = h[:, k * LANE:(k + 1) * LANE]
    logits = lax.dot_general(rt_ref[...], h.astype(BF16), (((1,), (1,)), ((), ())),
                             preferred_element_type=F32)
    eidx = lax.broadcasted_iota(I32, (N_EXP, tm), 0)
    eidx_f = eidx.astype(F32)
    m1 = jnp.max(logits, axis=0, keepdims=True)
    i1 = jnp.min(jnp.where(logits == m1, eidx_f, float(N_EXP)), axis=0, keepdims=True)
    oh1 = eidx_f == i1
    rest = jnp.where(oh1, -jnp.inf, logits)
    m2 = jnp.max(rest, axis=0, keepdims=True)
    i2 = jnp.min(jnp.where(rest == m2, eidx_f, float(N_EXP)), axis=0, keepdims=True)
    oh2 = eidx_f == i2
    i1 = i1.astype(I32)
    i2 = i2.astype(I32)
    e2 = jnp.exp(m2 - m1)
    w1 = 1.0 / (1.0 + e2)
    w2 = e2 / (1.0 + e2)
    sel = jnp.where(oh1 | oh2, 1.0, 0.0)
    cum = jnp.dot(sel.astype(BF16), tri_ref[...], preferred_element_type=F32)
    cum = cum + carry_ref[:, 0:1]
    r1 = jnp.sum(jnp.where(oh1, cum, 0.0), axis=0, keepdims=True).astype(I32)
    r2 = jnp.sum(jnp.where(oh2, cum, 0.0), axis=0, keepdims=True).astype(I32)
    ii_ref[...] = jnp.where(eidx == 0, i1, jnp.where(eidx == 1, i2, jnp.where(eidx == 2, r1,
                            jnp.where(eidx == 3, r2, 0))))
    wf_ref[...] = jnp.where(eidx == 0, w1, jnp.where(eidx == 1, w2, 0.0))
    carry_ref[...] = carry_ref[...] + jnp.sum(sel, axis=1, keepdims=True)
    cnt_ref[...] = carry_ref[...].astype(I32)


def _router(y, g, mod, router_t_bf16, mixer_out=None):
    pre_specs, pre_args, pre_out_specs, pre_out_shape = [], [], [], []
    if mixer_out is not None:
        pre_specs = _mixer_out_specs(*mixer_out)
        pre_args = [*mixer_out[0], mixer_out[1]]
        pre_out_specs = [pl.BlockSpec((TM, D), lambda i: (i, 0))]
        pre_out_shape = [jax.ShapeDtypeStruct((N_TOK, D), F32)]
    outs = pl.pallas_call(
        functools.partial(_router_kernel, mixer_out=mixer_out is not None),
        grid=(N_TOK // TM,),
        in_specs=pre_specs + [
            pl.BlockSpec((TM, D), lambda i: (i, 0)),
            pl.BlockSpec((1, D), lambda i: (0, 0)),
            pl.BlockSpec((None, 6, D), lambda i: (_cond_of_tile(i, TM), 0, 0)),
            pl.BlockSpec((N_EXP, D), lambda i: (0, 0)),
        ],
        out_specs=pre_out_specs + [
            pl.BlockSpec((TM, SLAB, LANE), lambda i: (i, 0, 0)),
            pl.BlockSpec((N_EXP, TM), lambda i: (0, i)),
            pl.BlockSpec((N_EXP, TM), lambda i: (0, i)),
            pl.BlockSpec((N_EXP, LANE), lambda i: (0, 0)),
            pl.BlockSpec((N_EXP, LANE), lambda i: (i, 0)),
        ],
        out_shape=pre_out_shape + [
            jax.ShapeDtypeStruct((N_TOK, SLAB, LANE), F32),
            jax.ShapeDtypeStruct((N_EXP, N_TOK), I32),
            jax.ShapeDtypeStruct((N_EXP, N_TOK), F32),
            jax.ShapeDtypeStruct((N_EXP, LANE), I32),
            jax.ShapeDtypeStruct((N_TOK // TM * N_EXP, LANE), I32),
        ],
        scratch_shapes=[pltpu.VMEM((N_EXP, LANE), F32), pltpu.VMEM((TM, TM), BF16)],
        compiler_params=_cparams(("arbitrary",)),
        name="router",
    )(*pre_args, y, g.reshape(1, D), mod, router_t_bf16)
    return tuple(outs) if mixer_out is not None else (y, *outs)


class _SegmentCopies:
    def __init__(self, seg_ref, dst_ref, nch_ref, stage_ref, hbm_ref, sem, *, to_hbm):
        self.tabs = (seg_ref, dst_ref, nch_ref)
        self.stage_ref, self.hbm_ref, self.sem, self.to_hbm = stage_ref, hbm_ref, sem, to_hbm

    def _chunk(self, slot, s_row, d_row):
        st = self.stage_ref.at[slot, pl.ds(s_row, CH)]
        hb = self.hbm_ref.at[pl.ds(d_row, CH)]
        sem = self.sem.at[slot]
        return pltpu.make_async_copy(st, hb, sem) if self.to_hbm else pltpu.make_async_copy(hb, st, sem)

    def start(self, tile):
        seg_ref, dst_ref, nch_ref = self.tabs
        slot = tile % 2
        for e in range(N_EXP):
            k = tile * N_EXP + e
            s0, d0 = seg_ref[k], dst_ref[k]

            def start_chunk(c, carry, s0=s0, d0=d0):
                self._chunk(slot, s0 + c * CH, d0 + c * CH).start()
                return carry

            lax.fori_loop(0, nch_ref[k], start_chunk, 0)

    def wait(self, tile):
        nch_ref = self.tabs[2]
        slot = tile % 2
        total = 0
        for e in range(N_EXP):
            total = total + nch_ref[tile * N_EXP + e]

        def wait_chunk(c, carry):
            self._chunk(slot, 0, 0).wait()
            return carry

        lax.fori_loop(0, total, wait_chunk, 0)


def _dispatch_kernel(l1_ref, l2_ref, seg_ref, dst_ref, nch_ref, zs_ref, zn_ref, hs_ref, xs_hbm, stage_ref, sem):
    i = pl.program_id(0)
    n_tiles = pl.num_programs(0)
    copies = _SegmentCopies(seg_ref, dst_ref, nch_ref, stage_ref, xs_hbm, sem, to_hbm=True)

    @pl.when(i == 0)
    def _():
        stage_ref[...] = jnp.zeros_like(stage_ref)

        def zero_chunk(row):
            return pltpu.make_async_copy(stage_ref.at[0, pl.ds(0, CH)], xs_hbm.at[pl.ds(row, CH)], sem.at[0])

        total = 0
        for z in range(N_EXP + 1):
            z0, n = zs_ref[z], zn_ref[z]

            def start(c, carry, z0=z0):
                zero_chunk(z0 + c * CH).start()
                return carry

            lax.fori_loop(0, n, start, 0)
            total = total + n

        def wait(c, carry):
            zero_chunk(0).wait()
            return carry

        lax.fori_loop(0, total, wait, 0)

    slot = i % 2
    base = i * TM

    def place(t8, carry):
        for u in range(SUBLANE):
            t = t8 * SUBLANE + u
            row = hs_ref[t]
            stage_ref[slot, l1_ref[base + t]] = row
            stage_ref[slot, l2_ref[base + t]] = row
        return carry

    lax.fori_loop(0, TM // SUBLANE, place, 0)
    pl.when(i >= 1)(lambda: copies.wait(i - 1))
    copies.start(i)
    pl.when(i == n_tiles - 1)(lambda: copies.wait(i))


def _dispatch(tabs, zero_tabs, hs):
    return pl.pallas_call(
        _dispatch_kernel,
        grid_spec=pltpu.PrefetchScalarGridSpec(
            num_scalar_prefetch=7,
            grid=(N_TOK // TM,),
            in_specs=[pl.BlockSpec((TM, SLAB, LANE), lambda i, *_: (i, 0, 0))],
            out_specs=pl.BlockSpec(memory_space=pl.ANY),
            scratch_shapes=[pltpu.VMEM((2, STAGE_ROWS, SLAB, LANE), F32), pltpu.SemaphoreType.DMA((2,))],
        ),
        out_shape=jax.ShapeDtypeStruct((MAX_TILES * TM_E, SLAB, LANE), F32),
        compiler_params=_cparams(("arbitrary",)),
        name="moe_dispatch",
    )(*tabs, *zero_tabs, hs)


FF_CHUNK = 256
N_FF_CHUNKS = D_FF // FF_CHUNK


def _expert_kernel(te_ref, tv_ref, first_ref, clo_ref, chi_ref, x_ref, wgu_hbm, wd_hbm, o_ref,
                   wg_ref, wu_ref, wd_ref, sg_ref, su_ref, sd_ref, xb_ref, h1_ref, sem, *, layer):
    i = pl.program_id(0)
    e = te_ref[i]
    cur = e % 2

    def copies(expert, c):
        s = c % 2
        cols = slice(c * FF_CHUNK, (c + 1) * FF_CHUNK)
        ucols = slice(D_FF + c * FF_CHUNK, D_FF + (c + 1) * FF_CHUNK)
        return (pltpu.make_async_copy(wgu_hbm.at[layer, expert, :, cols], sg_ref.at[s], sem.at[s]),
                pltpu.make_async_copy(wgu_hbm.at[layer, expert, :, ucols], su_ref.at[s], sem.at[s]),
                pltpu.make_async_copy(wd_hbm.at[layer, expert, cols, :], sd_ref.at[s], sem.at[s]))

    def fetch(expert, c):
        for cp in copies(expert, c):
            cp.start()

    def convert(expert, c, wslot):
        for cp in copies(expert, c):
            cp.wait()
        s = c % 2
        wg_ref[wslot, c] = sg_ref[s].astype(BF16)
        wu_ref[wslot, c] = su_ref[s].astype(BF16)
        wd_ref[wslot, c * FF_CHUNK:(c + 1) * FF_CHUNK, :] = sd_ref[s].astype(BF16)
        if c + 2 < N_FF_CHUNKS:
            fetch(expert, c + 2)

    @pl.when(i == 0)
    def _():
        fetch(e, 0)
        fetch(e, 1)
        for c in range(N_FF_CHUNKS):
            convert(e, c, cur)

    nxt = jnp.minimum(e + 1, N_EXP - 1)

    @pl.when((first_ref[i] == 1) & (e + 1 < N_EXP))
    def _():
        fetch(nxt, 0)
        fetch(nxt, 1)

    valid = tv_ref[i] == 1

    @pl.when(valid)
    def _():
        for k in range(SLAB):
            xb_ref[:, k * LANE:(k + 1) * LANE] = x_ref[pl.ds(k, TM_E, stride=SLAB), :].astype(BF16)
        xb = xb_ref[...]
        for c in range(N_FF_CHUNKS):
            gg = jnp.dot(xb, wg_ref[cur, c], preferred_element_type=F32)
            uu = jnp.dot(xb, wu_ref[cur, c], preferred_element_type=F32)
            h1_ref[:, c * FF_CHUNK:(c + 1) * FF_CHUNK] = (jax.nn.silu(gg) * uu).astype(BF16)
        acc = jnp.dot(h1_ref[...], wd_ref[cur], preferred_element_type=F32)
        for k in range(SLAB):
            o_ref[:, k, :] = acc[:, k * LANE:(k + 1) * LANE]

    @pl.when(jnp.logical_not(valid))
    def _():
        o_ref[...] = jnp.zeros_like(o_ref)

    for c in range(N_FF_CHUNKS):
        pl.when((clo_ref[i] <= c) & (c < chi_ref[i]))(functools.partial(convert, nxt, c, 1 - cur))


def _experts(tile_tabs, xs, w_gu, w_down, layer):
    return pl.pallas_call(
        functools.partial(_expert_kernel, layer=layer),
        grid_spec=pltpu.PrefetchScalarGridSpec(
            num_scalar_prefetch=5,
            grid=(MAX_TILES,),
            in_specs=[
                pl.BlockSpec((TM_E * SLAB, LANE), lambda i, *_: (i, 0)),
                pl.BlockSpec(memory_space=pl.ANY),
                pl.BlockSpec(memory_space=pl.ANY),
            ],
            out_specs=pl.BlockSpec((TM_E, SLAB, LANE), lambda i, *_: (i, 0, 0)),
            scratch_shapes=[
                pltpu.VMEM((2, N_FF_CHUNKS, D, FF_CHUNK), BF16),
                pltpu.VMEM((2, N_FF_CHUNKS, D, FF_CHUNK), BF16),
                pltpu.VMEM((2, D_FF, D), BF16),
                pltpu.VMEM((2, D, FF_CHUNK), F32),
                pltpu.VMEM((2, D, FF_CHUNK), F32),
                pltpu.VMEM((2, FF_CHUNK, D), F32),
                pltpu.VMEM((TM_E, D), BF16),
                pltpu.VMEM((TM_E, D_FF), BF16),
                pltpu.SemaphoreType.DMA((2,)),
            ],
        ),
        out_shape=jax.ShapeDtypeStruct(xs.shape, F32),
        compiler_params=_cparams(("arbitrary",)),
        name="moe_experts",
    )(*tile_tabs, xs.reshape(-1, LANE), w_gu, w_down)


def _combine_kernel(l1_ref, l2_ref, seg_ref, dst_ref, nch_ref, *refs, final):
    if final:
        y_ref, mod_ref, w_ref, fg_ref, ys_hbm, op_ref, ol_ref, stage_ref, g1_ref, g2_ref, sem, o_ref = refs
    else:
        y_ref, mod_ref, w_ref, ys_hbm, o_ref, stage_ref, g1_ref, g2_ref, sem = refs
    i = pl.program_id(0)
    copies = _SegmentCopies(seg_ref, dst_ref, nch_ref, stage_ref, ys_hbm, sem, to_hbm=False)
    pl.when(i == 0)(lambda: copies.start(i))
    pl.when(i + 1 < pl.num_programs(0))(lambda: copies.start(i + 1))
    copies.wait(i)
    slot = i % 2
    base = i * TM

    def pick(t8, carry):
        for u in range(SUBLANE):
            t = t8 * SUBLANE + u
            rows = pl.ds(pl.multiple_of(t * SLAB, SLAB), SLAB)
            g1_ref[rows, :] = stage_ref[slot, l1_ref[base + t]]
            g2_ref[rows, :] = stage_ref[slot, l2_ref[base + t]]
        return carry

    lax.fori_loop(0, TM // SUBLANE, pick, 0)
    w1 = w_ref[:, 0:1]
    w2 = w_ref[:, 1:2]
    for k in range(SLAB):
        cs = slice(k * LANE, (k + 1) * LANE)
        chunk_k = pl.ds(k, TM, stride=SLAB)
        f = w1 * g1_ref[chunk_k, :] + w2 * g2_ref[chunk_k, :]
        o_ref[:, cs] = y_ref[:, cs] + mod_ref[5:6, cs] * f
    if final:
        yn = o_ref[...]
        out = yn * lax.rsqrt(jnp.mean(yn * yn, axis=-1, keepdims=True) + 1e-6) * fg_ref[...]

        def write(part):
            (op_ref, ol_ref)[part][...] = out

        _for_stream_of_tile(write)


def _combine(tabs, y, mod, w_cols, ys, final_g=None):
    final = final_g is not None
    row_spec = pl.BlockSpec((TM, D), lambda i, *_: (i, 0))
    scratch = [
        pltpu.VMEM((2, STAGE_ROWS, SLAB, LANE), F32),
        pltpu.VMEM((TM * SLAB, LANE), F32),
        pltpu.VMEM((TM * SLAB, LANE), F32),
        pltpu.SemaphoreType.DMA((2,)),
    ]
    if final:
        out_specs = [pl.BlockSpec((TM, D), lambda i, *_: (jnp.minimum(i, N_PROMPT_TILES - 1), 0)),
                     pl.BlockSpec((TM, D), lambda i, *_: (jnp.maximum(i - N_PROMPT_TILES, 0), 0))]
        out_shape = [jax.ShapeDtypeStruct((N_P, D), F32), jax.ShapeDtypeStruct((N_L, D), F32)]
        scratch.append(pltpu.VMEM((TM, D), F32))
        extra_specs, extra_args = [pl.BlockSpec((1, D), lambda i, *_: (0, 0))], [final_g.reshape(1, D)]
    else:
        out_specs, out_shape, extra_specs, extra_args = row_spec, jax.ShapeDtypeStruct((N_TOK, D), F32), [], []
    return pl.pallas_call(
        functools.partial(_combine_kernel, final=final),
        grid_spec=pltpu.PrefetchScalarGridSpec(
            num_scalar_prefetch=5,
            grid=(N_TOK // TM,),
            in_specs=[
                row_spec,
                pl.BlockSpec((None, 6, D), lambda i, *_: (_cond_of_tile(i, TM), 0, 0)),
                pl.BlockSpec((TM, 2), lambda i, *_: (i, 0)),
                *extra_specs,
                pl.BlockSpec(memory_space=pl.ANY),
            ],
            out_specs=out_specs,
            scratch_shapes=scratch,
        ),
        out_shape=out_shape,
        compiler_params=_cparams(("arbitrary",)),
        name="moe_combine",
    )(*tabs, y, mod, w_cols, *extra_args, ys)


def _moe(y, g, mod, router, w_gu, w_down, slot, mixer_out=None, final_g=None):
    y, hs, ii, wf, cnt, tbase = _router(y, g, mod, router.T.astype(BF16), mixer_out)
    n_tt = N_TOK // TM
    counts = cnt[:, 0]
    before = tbase[:, 0].reshape(n_tt, N_EXP)
    n_seg = jnp.concatenate([before[1:], counts[None]], axis=0) - before
    region = (counts + CH + TM_E - 1) // TM_E * TM_E
    region_end = jnp.cumsum(region)
    offs = region_end - region
    seg_pad = (n_seg + CH - 1) // CH * CH
    seg = jnp.cumsum(seg_pad, axis=1) - seg_pad
    dst = offs[None, :] + before
    e_ids = jnp.arange(N_EXP, dtype=I32)[:, None]
    shift = jnp.repeat((seg - before).T, TM, axis=1)
    l1 = jnp.sum(jnp.where(ii[0:1] == e_ids, shift, 0), axis=0) + ii[2]
    l2 = jnp.sum(jnp.where(ii[1:2] == e_ids, shift, 0), axis=0) + ii[3]
    tabs = (l1, l2, seg.reshape(-1), dst.reshape(-1), (seg_pad // CH).reshape(-1))
    row0 = jnp.arange(MAX_TILES, dtype=I32) * TM_E
    tile_expert = jnp.minimum(jnp.sum((row0[:, None] >= region_end[None, :]).astype(I32), axis=1), N_EXP - 1)
    tile_valid = (row0 < (offs + counts)[tile_expert]).astype(I32)
    in_region = row0 < region_end[-1]
    j = (row0 - offs[tile_expert]) // TM_E
    n = (region // TM_E)[tile_expert]
    streams = in_region & (tile_expert < N_EXP - 1)
    tile_tabs = (tile_expert, tile_valid, (in_region & (j == 0)).astype(I32),
                 jnp.where(streams, j * N_FF_CHUNKS // n, 0), jnp.where(streams, (j + 1) * N_FF_CHUNKS // n, 0))
    zero_start = offs + counts // CH * CH
    zero_tabs = (jnp.concatenate([zero_start, region_end[-1:]]),
                 jnp.concatenate([region_end - zero_start, MAX_TILES * TM_E - region_end[-1:]]) // CH)
    xs = _dispatch(tabs, zero_tabs, hs)
    ys = _experts(tile_tabs, xs, w_gu, w_down, slot)
    return _combine(tabs, y, mod, wf[0:2].T, ys, final_g)


def kernel(x_prompt, x_sample, state_rglru, cache_k, cache_v, c, c_ctx, norm_g, ada_w, ada_b,
           gm_w_in, gm_ln_g, gm_w_s, gm_b_s, gm_w_out, rg_w_in, rg_conv_w, rg_conv_b, rg_w_gate,
           rg_b_gate, rg_lambda, rg_w_out, att_w_qkv, att_lambda, att_subln_g, att_w_out,
           ff_w_gu, ff_w_down, moe_router, moe_w_gu, moe_w_down, final_g):
    y = (x_prompt.reshape(N_P, D), x_sample.reshape(N_L, D))
    conds = jnp.concatenate([c_ctx[None], c, jnp.zeros((N_COND - 1 - B_L, D), F32)], axis=0)
    mods = _adaln(conds, ada_w, ada_b)
    nb = D_RNN // RNN_BS
    gm_w_in, gm_w_out, rg_w_in, rg_w_out, att_w_qkv, att_w_out, ff_w_gu, ff_w_down = (
        w.astype(BF16) for w in (gm_w_in, gm_w_out, rg_w_in, rg_w_out, att_w_qkv, att_w_out, ff_w_gu, ff_w_down))
    new_rnn = new_k = new_v = None
    for i in range(DEPTH):
        mod = mods[i]
        kind, slot = i % 3, i // 3
        mixer_out = None
        if kind == 0:
            y = _gmlp_mixer(y if i == 0 else (y,), norm_g[i, 0], mod, gm_w_in, gm_ln_g[slot],
                            gm_w_s[slot].astype(BF16), gm_b_s[slot].T, gm_w_out, slot)
        elif kind == 1:
            yx = _mod_matmul(y, norm_g[i, 0], mod, rg_w_in, slot, tn=256, gelu_cols=D_RNN)
            wg_cat = rg_w_gate[slot].transpose(2, 3, 0, 1, 4).reshape(nb, RNN_BS, 4 * RNN_BS).astype(BF16)
            bg_cat = rg_b_gate[slot].reshape(2, 2, nb, RNN_BS).transpose(2, 0, 1, 3).reshape(nb, 1, 4 * RNN_BS)
            rg_args = (yx, rg_conv_w[slot], rg_conv_b[slot], wg_cat, bg_cat, rg_lambda[slot])
            a_p, fin = _rglru(*rg_args, jnp.zeros((2, B_P, D_RNN), F32),
                              row0=0, n_batch=B_P, n_seq=SUBLANE, t_len=T_P)
            a_l, _ = _rglru(*rg_args, state_rglru[:, slot].transpose(1, 0, 2),
                            row0=N_P, n_batch=B_L, n_seq=B_L, t_len=T_L)
            new_rnn = fin.transpose(1, 0, 2)[:, None]
            mixer_out = ((a_p, a_l), rg_w_out, slot)
        else:
            lambda_init = 0.8 - 0.6 * math.exp(-0.3 * i)
            att = (y, norm_g[i, 0], mod, att_w_qkv, slot, att_lambda[slot], att_subln_g[slot], lambda_init)
            a_p, new_k, new_v = _attention(*att, row0=0, n_batch=B_P, t_len=T_P, heads=HEADS, tq=T_P)
            a_l = _attention(*att, row0=N_P, n_batch=B_L, t_len=T_L, heads=4, tq=256,
                             ctx=(cache_k, cache_v, slot), rope_tabs=_rope_tables(T_L))
            new_k = new_k.reshape(B_P, 1, T_P, HEADS, 2 * HD)
            new_v = new_v.reshape(B_P, 1, T_P, HEADS, VD)
            mixer_out = ((a_p, a_l), att_w_out, slot)
        fslot = i // 2
        if i % 2 == 0:
            y = _ffn(y, norm_g[i, 1], mod, ff_w_gu, ff_w_down, fslot, mixer_out)
        else:
            y = _moe(y, norm_g[i, 1], mod, moe_router[fslot], moe_w_gu, moe_w_down, fslot, mixer_out,
                     final_g if i == DEPTH - 1 else None)
    y_prompt, y_sample = y
    return (y_prompt.reshape(B_P, T_P, D), y_sample.reshape(B_L, T_L, D), new_rnn, new_k, new_v)
```

```python
import functools
import math

import jax
import jax.numpy as jnp
import numpy as np
from jax import lax
from jax.experimental import pallas as pl
from jax.experimental.pallas import tpu as pltpu

F32 = jnp.float32
BF16 = jnp.bfloat16
I32 = jnp.int32

D = 1024
DEPTH = 4
B_P, T_P = 32, 256
B_L, T_L = 4, 2048
PAST = 256
N_P = B_P * T_P
N_L = B_L * T_L
N_TOK = N_P + N_L
N_COND = 8
GRID_W = 64
CHUNK = 128
GROUPS = 8
D_RNN = 1280
RNN_BS = 128
CONV_W = 4
RGLRU_C = 8.0
HD = 64
VD = 128
HEADS = 8
ROPE_THETA = 10000.0
LOG2_E = 1.4426950408889634
D_FF = 2816
N_EXP = 8

LANE = 128
SUBLANE = 8
VMEM_LIMIT = 56 * 1024 * 1024
TM = 1024
TM_E = 256
CH = 32
MAX_TILES = -(-(2 * N_TOK + N_EXP * (CH + TM_E - 1)) // TM_E)
STAGE_ROWS = 2 * TM + N_EXP * CH
SLAB = D // LANE


def _cparams(sem):
    return pltpu.CompilerParams(dimension_semantics=sem, vmem_limit_bytes=VMEM_LIMIT)


def _cond_of_tile(i, tm):
    r0 = i * tm
    return jnp.where(r0 < N_P, 0, 1 + (r0 - N_P) // T_L)


def _modnorm(x, g, shift, scale):
    ms = jnp.mean(x * x, axis=-1, keepdims=True)
    h = x * lax.rsqrt(ms + 1e-6) * g
    return h * (1.0 + scale) + shift


def _adaln_kernel(c_ref, w_ref, b_ref, o_ref):
    s = jax.nn.silu(c_ref[...]).astype(BF16)
    o_ref[...] = jnp.dot(s, w_ref[...].astype(BF16), preferred_element_type=F32) + b_ref[...]


def _adaln(conds, ada_w, ada_b):
    tn = 1536
    n6 = 6 * D
    out = pl.pallas_call(
        _adaln_kernel,
        grid=(DEPTH, n6 // tn),
        in_specs=[
            pl.BlockSpec((N_COND, D), lambda l, j: (0, 0)),
            pl.BlockSpec((None, D, tn), lambda l, j: (l, 0, j)),
            pl.BlockSpec((None, 1, tn), lambda l, j: (l, 0, j)),
        ],
        out_specs=pl.BlockSpec((None, N_COND, tn), lambda l, j: (l, 0, j)),
        out_shape=jax.ShapeDtypeStruct((DEPTH, N_COND, n6), F32),
        compiler_params=_cparams(("arbitrary", "arbitrary")),
        name="adaln",
    )(conds, ada_w, ada_b.reshape(DEPTH, 1, n6))
    return out.reshape(DEPTH, N_COND, 6, D)


def _mod_matmul_kernel(x_ref, g_ref, mod_ref, w_ref, o_ref, *, tn, gelu_cols):
    h = _modnorm(x_ref[...], g_ref[...], mod_ref[0:1, :], mod_ref[1:2, :]).astype(BF16)
    for c in range(o_ref.shape[1] // tn):
        acc = jnp.dot(h, w_ref[:, c * tn:(c + 1) * tn], preferred_element_type=F32)
        if c * tn < gelu_cols:
            acc = jax.nn.gelu(acc)
        o_ref[:, c * tn:(c + 1) * tn] = acc


def _mod_matmul(y, g, mod, w_bf16, slot, *, tn, gelu_cols):
    nout = w_bf16.shape[2]
    return pl.pallas_call(
        functools.partial(_mod_matmul_kernel, tn=tn, gelu_cols=gelu_cols),
        grid=(N_TOK // TM,),
        in_specs=[
            pl.BlockSpec((TM, D), lambda i: (i, 0)),
            pl.BlockSpec((1, D), lambda i: (0, 0)),
            pl.BlockSpec((None, 6, D), lambda i: (_cond_of_tile(i, TM), 0, 0)),
            pl.BlockSpec((None, D, nout), lambda i: (slot, 0, 0)),
        ],
        out_specs=pl.BlockSpec((TM, nout), lambda i: (i, 0)),
        out_shape=jax.ShapeDtypeStruct((N_TOK, nout), F32),
        compiler_params=_cparams(("arbitrary",)),
        name="mod_matmul",
    )(y, g.reshape(1, D), mod, w_bf16)


N_PROMPT_TILES = N_P // TM


def _mixer_out_specs(a_parts, w_out_bf16, slot):
    k = a_parts[0].shape[1]
    return [
        pl.BlockSpec((TM, k), lambda i, *_: (jnp.minimum(i, N_PROMPT_TILES - 1), 0)),
        pl.BlockSpec((TM, k), lambda i, *_: (jnp.maximum(i - N_PROMPT_TILES, 0), 0)),
        pl.BlockSpec((None, k, D), lambda i, *_: (slot, 0, 0)),
    ]


def _for_stream_of_tile(body):
    pl.when(pl.program_id(0) < N_PROMPT_TILES)(lambda: body(0))
    pl.when(pl.program_id(0) >= N_PROMPT_TILES)(lambda: body(1))


def _gmlp_kernel(*refs):
    *y_refs, g_ref, mod_ref, win_ref, lng_ref, ws_ref, bs_ref, wout_ref, o_ref, uv_ref, a_ref = refs
    body = functools.partial(_gmlp_tile, g_ref, mod_ref, win_ref, lng_ref, ws_ref, bs_ref, wout_ref,
                             o_ref, uv_ref, a_ref)
    if len(y_refs) == 1:
        body(y_refs[0])
    else:
        _for_stream_of_tile(lambda part: body(y_refs[part]))


def _gmlp_tile(g_ref, mod_ref, win_ref, lng_ref, ws_ref, bs_ref, wout_ref, o_ref, uv_ref, a_ref, y_ref):
    y = y_ref[...]
    h = _modnorm(y, g_ref[...], mod_ref[0:1, :], mod_ref[1:2, :]).astype(BF16)
    tn = 512
    for c in range(2 * D // tn):
        uv = jnp.dot(h, win_ref[:, c * tn:(c + 1) * tn], preferred_element_type=F32)
        uv_ref[:, c * tn:(c + 1) * tn] = jax.nn.gelu(uv)
    v = uv_ref[:, D:]
    vc = v - jnp.mean(v, axis=-1, keepdims=True)
    vn = vc * lax.rsqrt(jnp.mean(vc * vc, axis=-1, keepdims=True) + 1e-5) * lng_ref[...]
    vn = vn.astype(BF16)
    for ci in range(TM // CHUNK):
        r0 = ci * CHUNK
        for g in range(GROUPS):
            c0 = g * LANE
            s = jnp.dot(ws_ref[g], vn[r0:r0 + CHUNK, c0:c0 + LANE], preferred_element_type=F32)
            s = s + bs_ref[:, g:g + 1]
            a_ref[r0:r0 + CHUNK, c0:c0 + LANE] = (uv_ref[r0:r0 + CHUNK, c0:c0 + LANE] * s).astype(BF16)
    o_ref[...] = y + mod_ref[2:3, :] * jnp.dot(a_ref[...], wout_ref[...], preferred_element_type=F32)


def _gmlp_mixer(y_parts, g, mod, w_in_bf16, ln_g, w_s_bf16, b_s_t, w_out_bf16, slot):
    if len(y_parts) == 1:
        y_specs = [pl.BlockSpec((TM, D), lambda i: (i, 0))]
    else:
        y_specs = [pl.BlockSpec((TM, D), lambda i: (jnp.minimum(i, N_PROMPT_TILES - 1), 0)),
                   pl.BlockSpec((TM, D), lambda i: (jnp.maximum(i - N_PROMPT_TILES, 0), 0))]
    return pl.pallas_call(
        _gmlp_kernel,
        grid=(N_TOK // TM,),
        in_specs=y_specs + [
            pl.BlockSpec((1, D), lambda i: (0, 0)),
            pl.BlockSpec((None, 6, D), lambda i: (_cond_of_tile(i, TM), 0, 0)),
            pl.BlockSpec((None, D, 2 * D), lambda i: (slot, 0, 0)),
            pl.BlockSpec((1, D), lambda i: (0, 0)),
            pl.BlockSpec((GROUPS, CHUNK, CHUNK), lambda i: (0, 0, 0)),
            pl.BlockSpec((CHUNK, GROUPS), lambda i: (0, 0)),
            pl.BlockSpec((None, D, D), lambda i: (slot, 0, 0)),
        ],
        out_specs=pl.BlockSpec((TM, D), lambda i: (i, 0)),
        out_shape=jax.ShapeDtypeStruct((N_TOK, D), F32),
        scratch_shapes=[pltpu.VMEM((TM, 2 * D), F32), pltpu.VMEM((TM, D), BF16)],
        compiler_params=_cparams(("arbitrary",)),
        name="gmlp_mixer",
    )(*y_parts, g.reshape(1, D), mod, w_in_bf16, ln_g.reshape(1, D), w_s_bf16, b_s_t, w_out_bf16)


SCAN_UNROLL = 8


def _rglru_kernel(yb_ref, xb_ref, cw_ref, cb_ref, wg_ref, bg_ref, lam_ref, h0_ref,
                  o_ref, fin_ref, af_ref, bf_ref, ab_ref, bb_ref, *, n_seq, t_len, row_chunk):
    stride = t_len + SUBLANE
    tix = lax.broadcasted_iota(I32, (t_len, 1), 0)
    sp_f = jax.nn.softplus(-lam_ref[0:1, :])
    sp_b = jax.nn.softplus(-lam_ref[1:2, :])
    for g in range(n_seq):
        x = xb_ref[g * t_len:(g + 1) * t_len, :]
        xm1 = jnp.where(tix >= 1, pltpu.roll(x, 1, 0), 0.0)
        xp1 = jnp.where(tix <= t_len - 2, pltpu.roll(x, t_len - 1, 0), 0.0)
        xp2 = jnp.where(tix <= t_len - 3, pltpu.roll(x, t_len - 2, 0), 0.0)
        xc = cb_ref[...] + xm1 * cw_ref[0:1, :]
        xc = xc + x * cw_ref[1:2, :]
        xc = xc + xp1 * cw_ref[2:3, :]
        xc = xc + xp2 * cw_ref[3:4, :]
        for r0 in range(0, t_len, row_chunk):
            xcc = xc[r0:r0 + row_chunk]
            gates = jnp.dot(xcc.astype(BF16), wg_ref[...], preferred_element_type=F32) + bg_ref[...]
            d0 = g * stride + r0
            for d, (sp, a_ref, b_ref) in enumerate(((sp_f, af_ref, bf_ref), (sp_b, ab_ref, bb_ref))):
                r = jax.nn.sigmoid(gates[:, (2 * d) * LANE:(2 * d + 1) * LANE])
                i = jax.nn.sigmoid(gates[:, (2 * d + 1) * LANE:(2 * d + 2) * LANE])
                log_a = (-RGLRU_C * r) * sp
                a = jnp.exp(log_a)
                one_minus_a2 = jnp.tanh(-log_a) * (a * a + 1.0)
                a_ref[d0:d0 + row_chunk, :] = a
                b_ref[d0:d0 + row_chunk, :] = jnp.sqrt(one_minus_a2) * (i * xcc)

    def block_scan(a_ref, b_ref, rows, h_in):
        a = [a_ref[r, :] for r in rows]
        b = [b_ref[r, :] for r in rows]
        d = 1
        while d < SCAN_UNROLL:
            a, b = ([a[u] if u < d else a[u] * a[u - d] for u in range(SCAN_UNROLL)],
                    [b[u] if u < d else a[u] * b[u - d] + b[u] for u in range(SCAN_UNROLL)])
            d *= 2
        h = [a[u] * h_in + b[u] for u in range(SCAN_UNROLL)]
        for r, hu in zip(rows, h):
            b_ref[r, :] = hu
        return h[-1]

    def steps(s8, carry):
        hf, hb = carry
        s0 = s8 * SCAN_UNROLL
        hf = block_scan(af_ref, bf_ref, [pl.ds(s0 + u, n_seq, stride=stride) for u in range(SCAN_UNROLL)], hf)
        hb = block_scan(ab_ref, bb_ref,
                        [pl.ds(t_len - 1 - s0 - u, n_seq, stride=stride) for u in range(SCAN_UNROLL)], hb)
        return hf, hb

    hf, hb = lax.fori_loop(0, t_len // SCAN_UNROLL, steps, (h0_ref[0], h0_ref[1]))
    fin_ref[0] = hf
    fin_ref[1] = hb
    for g in range(n_seq):
        rs = slice(g * stride, g * stride + t_len)
        hsum = bf_ref[rs, :] + bb_ref[rs, :]
        o_ref[g * t_len:(g + 1) * t_len, :] = (hsum * yb_ref[g * t_len:(g + 1) * t_len, :]).astype(BF16)


def _rglru(yx, cw, cb, wg_cat, bg_cat, lam, h0, *, row0, n_batch, n_seq, t_len):
    rows = n_seq * t_len
    nb = D_RNN // RNN_BS
    rb0 = row0 // rows
    kern = functools.partial(_rglru_kernel, n_seq=n_seq, t_len=t_len, row_chunk=min(t_len, 512))
    scan_buf = pltpu.VMEM((n_seq * (t_len + SUBLANE), RNN_BS), F32)
    return pl.pallas_call(
        kern,
        grid=(n_batch // n_seq, nb),
        in_specs=[
            pl.BlockSpec((rows, RNN_BS), lambda s, c: (rb0 + s, c)),
            pl.BlockSpec((rows, RNN_BS), lambda s, c: (rb0 + s, nb + c)),
            pl.BlockSpec((CONV_W, RNN_BS), lambda s, c: (0, c)),
            pl.BlockSpec((1, RNN_BS), lambda s, c: (0, c)),
            pl.BlockSpec((None, RNN_BS, 4 * RNN_BS), lambda s, c: (c, 0, 0)),
            pl.BlockSpec((None, 1, 4 * RNN_BS), lambda s, c: (c, 0, 0)),
            pl.BlockSpec((2, RNN_BS), lambda s, c: (0, c)),
            pl.BlockSpec((2, n_seq, RNN_BS), lambda s, c: (0, s, c)),
        ],
        out_specs=[
            pl.BlockSpec((rows, RNN_BS), lambda s, c: (s, c)),
            pl.BlockSpec((2, n_seq, RNN_BS), lambda s, c: (0, s, c)),
        ],
        out_shape=[
            jax.ShapeDtypeStruct((n_batch * t_len, D_RNN), BF16),
            jax.ShapeDtypeStruct((2, n_batch, D_RNN), F32),
        ],
        scratch_shapes=[scan_buf, scan_buf, scan_buf, scan_buf],
        compiler_params=_cparams(("arbitrary", "arbitrary")),
        name="rglru",
    )(yx, yx, cw, cb.reshape(1, D_RNN), wg_cat, bg_cat, lam, h0)


def _rope(x, cos, sin_signed, first_of_pair):
    partner = jnp.where(first_of_pair, pltpu.roll(x, LANE - 16, 1), pltpu.roll(x, 16, 1))
    return x * cos + partner * sin_signed


def _attn_kernel(*refs, heads, t_len, tq, s_len, rope, lambda_init, mxu_denominator):
    if rope:
        (lam_ref, g_ref, y_ref, ng_ref, mod_ref, wq_ref, wk_ref, wv_ref, ck_ref, cv_ref, cq_ref, sq_ref,
         ckk_ref, skk_ref, o_ref, h_ref, kk_ref, vv_ref) = refs
    else:
        (lam_ref, g_ref, y_ref, ng_ref, mod_ref, wq_ref, wk_ref, wv_ref,
         o_ref, ko_ref, vo_ref, h_ref, kk_ref, vv_ref) = refs
    lane = lax.broadcasted_iota(I32, (1, LANE), 1)
    first_of_pair = (lane & 16) == 0
    map0 = lane < HD

    @pl.when((pl.program_id(1) == 0) & (pl.program_id(2) == 0))
    def _():
        h_ref[...] = _modnorm(y_ref[...], ng_ref[...], mod_ref[0:1, :], mod_ref[1:2, :]).astype(BF16)

    @pl.when(pl.program_id(2) == 0)
    def _():
        k_all = jnp.dot(h_ref[...], wk_ref[...], preferred_element_type=F32)
        v_all = jnp.dot(h_ref[...], wv_ref[...], preferred_element_type=F32)
        if not rope:
            ko_ref[...] = k_all
            vo_ref[...] = v_all
        for j in range(heads):
            cs = slice(j * LANE, (j + 1) * LANE)
            k = k_all[:, cs]
            if rope:
                k = _rope(k, ckk_ref[...], skk_ref[...], first_of_pair)
            kk_ref[j, 0:t_len, :] = k.astype(BF16)
            vv_ref[j, 0:t_len, 0:LANE] = v_all[:, cs].astype(BF16)
            if s_len > t_len:
                ctx_rows = pl.ds(pl.program_id(1) * heads + j, s_len - t_len, stride=HEADS)
                kk_ref[j, t_len:s_len, :] = ck_ref[ctx_rows, :].astype(BF16)
                vv_ref[j, t_len:s_len, 0:LANE] = cv_ref[ctx_rows, :].astype(BF16)
            if mxu_denominator:
                vv_ref[j, :, LANE:2 * LANE] = jnp.ones((s_len, LANE), BF16)

    lp = lam_ref[...]
    lam = (jnp.exp(jnp.sum(lp[0:1] * lp[1:2], axis=-1, keepdims=True))
           - jnp.exp(jnp.sum(lp[2:3] * lp[3:4], axis=-1, keepdims=True)) + lambda_init)
    q_rows = pl.ds(pl.multiple_of(pl.program_id(2) * tq, tq), tq)
    q_all = jnp.dot(h_ref[q_rows, :], wq_ref[...], preferred_element_type=F32)
    for j in range(heads):
        cs = slice(j * LANE, (j + 1) * LANE)
        q = q_all[:, cs]
        if rope:
            q = _rope(q, cq_ref[...], sq_ref[...], first_of_pair)
        q = q * (HD ** -0.5 * LOG2_E)
        q2 = jnp.concatenate([jnp.where(map0, q, 0.0), jnp.where(map0, 0.0, q)], axis=0).astype(BF16)
        s = lax.dot_general(q2, kk_ref[j], (((1,), (1,)), ((), ())), preferred_element_type=F32)
        e = jnp.exp2(s - jnp.max(s, axis=-1, keepdims=True))
        if mxu_denominator:
            nd = jnp.dot(e.astype(BF16), vv_ref[j], preferred_element_type=F32)
            av = nd[:, 0:LANE] * (1.0 / nd[:, LANE:LANE + 1])
            o = av[0:tq] - lam * av[tq:2 * tq]
        else:
            p = e * (1.0 / jnp.sum(e, axis=-1, keepdims=True))
            w = p[0:tq] - lam * p[tq:2 * tq]
            o = jnp.dot(w.astype(BF16), vv_ref[j], preferred_element_type=F32)
        o = o * lax.rsqrt(jnp.mean(o * o, axis=-1, keepdims=True) + 1e-6) * g_ref[...]
        o_ref[:, cs] = (o * (1.0 - lambda_init)).astype(BF16)


def _attention(y, norm_g, mod, w_qkv_bf16, wslot, lam_p, subln_g, lambda_init, *, row0, n_batch, t_len, heads, tq,
               ctx=None, rope_tabs=None):
    rope = rope_tabs is not None
    s_len = t_len + (PAST if rope else 0)
    w = heads * LANE
    nhb = HEADS // heads
    nq = t_len // tq
    rk0 = row0 // t_len
    cond0 = 1 if rope else 0
    mxu_denominator = s_len >= 1024
    kern = functools.partial(_attn_kernel, heads=heads, t_len=t_len, tq=tq, s_len=s_len, rope=rope,
                             lambda_init=lambda_init, mxu_denominator=mxu_denominator)
    v_width = 2 * LANE if mxu_denominator else LANE
    in_specs = [
        pl.BlockSpec((4, HD), lambda b, h, q: (0, 0)),
        pl.BlockSpec((1, VD), lambda b, h, q: (0, 0)),
        pl.BlockSpec((t_len, D), lambda b, h, q: (rk0 + b, 0)),
        pl.BlockSpec((1, D), lambda b, h, q: (0, 0)),
        pl.BlockSpec((None, 6, D), lambda b, h, q: (b * cond0 + cond0, 0, 0)),
        pl.BlockSpec((None, D, w), lambda b, h, q: (wslot, 0, h)),
        pl.BlockSpec((None, D, w), lambda b, h, q: (wslot, 0, nhb + h)),
        pl.BlockSpec((None, D, w), lambda b, h, q: (wslot, 0, 2 * nhb + h)),
    ]
    args = [lam_p, subln_g.reshape(1, VD), y, norm_g.reshape(1, D), mod, w_qkv_bf16, w_qkv_bf16, w_qkv_bf16]
    row_block = pl.BlockSpec((tq, w), lambda b, h, q: (b * nq + q, h))
    out_specs, out_shape = row_block, jax.ShapeDtypeStruct((n_batch * t_len, D), BF16)
    if not rope:
        seq_block = pl.BlockSpec((t_len, w), lambda b, h, q: (b, h))
        out_specs = [row_block, seq_block, seq_block]
        out_shape = [out_shape] + [jax.ShapeDtypeStruct((n_batch * t_len, D), F32)] * 2
    if rope:
        cos_t, sin_t = rope_tabs
        cache_k, cache_v, slot = ctx
        n_slots = cache_k.shape[1]
        ctx_k = cache_k.reshape(n_batch, n_slots, PAST * HEADS, LANE)
        ctx_v = cache_v.reshape(n_batch, n_slots, PAST * HEADS, LANE)
        in_specs += [
            pl.BlockSpec((None, None, PAST * HEADS, LANE), lambda b, h, q: (b, slot, 0, 0)),
            pl.BlockSpec((None, None, PAST * HEADS, LANE), lambda b, h, q: (b, slot, 0, 0)),
            pl.BlockSpec((tq, LANE), lambda b, h, q: (q, 0)),
            pl.BlockSpec((tq, LANE), lambda b, h, q: (q, 0)),
            pl.BlockSpec((t_len, LANE), lambda b, h, q: (0, 0)),
            pl.BlockSpec((t_len, LANE), lambda b, h, q: (0, 0)),
        ]
        args += [ctx_k, ctx_v, cos_t, sin_t, cos_t, sin_t]
    return pl.pallas_call(
        kern,
        grid=(n_batch, nhb, nq),
        in_specs=in_specs,
        out_specs=out_specs,
        out_shape=out_shape,
        scratch_shapes=[pltpu.VMEM((t_len, D), BF16), pltpu.VMEM((heads, s_len, LANE), BF16),
                        pltpu.VMEM((heads, s_len, v_width), BF16)],
        compiler_params=_cparams(("arbitrary", "arbitrary", "arbitrary")),
        name="diff_attn",
    )(*args)


def _rope_tables(t_len):
    t = np.arange(t_len)
    half = HD // 2
    freqs = (ROPE_THETA ** (-np.arange(0, half, 2, dtype=np.float32) / half)).astype(np.float32)
    ang_r = (t // GRID_W).astype(np.float32)[:, None] * freqs[None]
    ang_c = (t % GRID_W).astype(np.float32)[:, None] * freqs[None]
    cr, sr, cc, sc = np.cos(ang_r), np.sin(ang_r), np.cos(ang_c), np.sin(ang_c)
    cos64 = np.concatenate([cr, cr, cc, cc], axis=-1)
    sin64 = np.concatenate([-sr, sr, -sc, sc], axis=-1)
    return (jnp.asarray(np.concatenate([cos64, cos64], axis=-1), F32),
            jnp.asarray(np.concatenate([sin64, sin64], axis=-1), F32))


def _swiglu_rows(h_bf16, wgu_ref, wd_ref, h1_ref):
    tn = 256
    for c in range(D_FF // tn):
        gg = jnp.dot(h_bf16, wgu_ref[:, c * tn:(c + 1) * tn], preferred_element_type=F32)
        uu = jnp.dot(h_bf16, wgu_ref[:, D_FF + c * tn:D_FF + (c + 1) * tn], preferred_element_type=F32)
        h1_ref[:, c * tn:(c + 1) * tn] = (jax.nn.silu(gg) * uu).astype(BF16)
    return jnp.dot(h1_ref[...], wd_ref[...], preferred_element_type=F32)


def _ffn_kernel(*refs, mixer_out):
    if mixer_out:
        ap_ref, al_ref, wout_ref, y_ref, g_ref, mod_ref, wgu_ref, wd_ref, o_ref, h1_ref = refs
    else:
        y_ref, g_ref, mod_ref, wgu_ref, wd_ref, o_ref, h1_ref = refs

    def body(a_ref):
        y = y_ref[...]
        if a_ref is not None:
            y = y + mod_ref[2:3, :] * jnp.dot(a_ref[...], wout_ref[...], preferred_element_type=F32)
        h = _modnorm(y, g_ref[...], mod_ref[3:4, :], mod_ref[4:5, :]).astype(BF16)
        o_ref[...] = y + mod_ref[5:6, :] * _swiglu_rows(h, wgu_ref, wd_ref, h1_ref)

    if mixer_out:
        _for_stream_of_tile(lambda part: body((ap_ref, al_ref)[part]))
    else:
        body(None)


def _ffn(y, g, mod, wgu_bf16, wd_bf16, slot, mixer_out=None):
    pre_specs, pre_args = [], []
    if mixer_out is not None:
        pre_specs = _mixer_out_specs(*mixer_out)
        pre_args = [*mixer_out[0], mixer_out[1]]
    return pl.pallas_call(
        functools.partial(_ffn_kernel, mixer_out=mixer_out is not None),
        grid=(N_TOK // TM,),
        in_specs=pre_specs + [
            pl.BlockSpec((TM, D), lambda i: (i, 0)),
            pl.BlockSpec((1, D), lambda i: (0, 0)),
            pl.BlockSpec((None, 6, D), lambda i: (_cond_of_tile(i, TM), 0, 0)),
            pl.BlockSpec((None, D, 2 * D_FF), lambda i: (slot, 0, 0), pipeline_mode=pl.Buffered(1)),
            pl.BlockSpec((None, D_FF, D), lambda i: (slot, 0, 0), pipeline_mode=pl.Buffered(1)),
        ],
        out_specs=pl.BlockSpec((TM, D), lambda i: (i, 0)),
        out_shape=jax.ShapeDtypeStruct((N_TOK, D), F32),
        scratch_shapes=[pltpu.VMEM((TM, D_FF), BF16)],
        compiler_params=_cparams(("arbitrary",)),
        name="ffn",
    )(*pre_args, y, g.reshape(1, D), mod, wgu_bf16, wd_bf16)


def _router_kernel(*refs, mixer_out):
    if mixer_out:
        (ap_ref, al_ref, wout_ref, y_ref, g_ref, mod_ref, rt_ref,
         ym_ref, hs_ref, ii_ref, wf_ref, cnt_ref, tb_ref, carry_ref, tri_ref) = refs

        def add_mixer_out(part):
            acc = jnp.dot((ap_ref, al_ref)[part][...], wout_ref[...], preferred_element_type=F32)
            ym_ref[...] = y_ref[...] + mod_ref[2:3, :] * acc

        _for_stream_of_tile(add_mixer_out)
        y = ym_ref[...]
    else:
        y_ref, g_ref, mod_ref, rt_ref, hs_ref, ii_ref, wf_ref, cnt_ref, tb_ref, carry_ref, tri_ref = refs
        y = y_ref[...]
    tm = y_ref.shape[0]

    @pl.when(pl.program_id(0) == 0)
    def _():
        carry_ref[...] = jnp.zeros_like(carry_ref)
        before = (lax.broadcasted_iota(I32, (tm, tm), 0) < lax.broadcasted_iota(I32, (tm, tm), 1))
        tri_ref[...] = jnp.where(before, 1.0, 0.0).astype(BF16)

    tb_ref[...] = carry_ref[...].astype(I32)

    h = _modnorm(y, g_ref[...], mod_ref[3:4, :], mod_ref[4:5, :])
    for k in range(SLAB):
        hs_ref[:, k, :] = h[:, k * LANE:(k + 1) * LANE]
    logits = lax.dot_general(rt_ref[...], h.astype(BF16), (((1,), (1,)), ((), ())),
                             preferred_element_type=F32)
    eidx = lax.broadcasted_iota(I32, (N_EXP, tm), 0)
    eidx_f = eidx.astype(F32)
    m1 = jnp.max(logits, axis=0, keepdims=True)
    i1 = jnp.min(jnp.where(logits == m1, eidx_f, float(N_EXP)), axis=0, keepdims=True)
    oh1 = eidx_f == i1
    rest = jnp.where(oh1, -jnp.inf, logits)
    m2 = jnp.max(rest, axis=0, keepdims=True)
    i2 = jnp.min(jnp.where(rest == m2, eidx_f, float(N_EXP)), axis=0, keepdims=True)
    oh2 = eidx_f == i2
    i1 = i1.astype(I32)
    i2 = i2.astype(I32)
    e2 = jnp.exp(m2 - m1)
    w1 = 1.0 / (1.0 + e2)
    w2 = e2 / (1.0 + e2)
    sel = jnp.where(oh1 | oh2, 1.0, 0.0)
    cum = jnp.dot(sel.astype(BF16), tri_ref[...], preferred_element_type=F32)
    cum = cum + carry_ref[:, 0:1]
    r1 = jnp.sum(jnp.where(oh1, cum, 0.0), axis=0, keepdims=True).astype(I32)
    r2 = jnp.sum(jnp.where(oh2, cum, 0.0), axis=0, keepdims=True).astype(I32)
    ii_ref[...] = jnp.where(eidx == 0, i1, jnp.where(eidx == 1, i2, jnp.where(eidx == 2, r1,
                            jnp.where(eidx == 3, r2, 0))))
    wf_ref[...] = jnp.where(eidx == 0, w1, jnp.where(eidx == 1, w2, 0.0))
    carry_ref[...] = carry_ref[...] + jnp.sum(sel, axis=1, keepdims=True)
    cnt_ref[...] = carry_ref[...].astype(I32)


def _router(y, g, mod, router_t_bf16, mixer_out=None):
    pre_specs, pre_args, pre_out_specs, pre_out_shape = [], [], [], []
    if mixer_out is not None:
        pre_specs = _mixer_out_specs(*mixer_out)
        pre_args = [*mixer_out[0], mixer_out[1]]
        pre_out_specs = [pl.BlockSpec((TM, D), lambda i: (i, 0))]
        pre_out_shape = [jax.ShapeDtypeStruct((N_TOK, D), F32)]
    outs = pl.pallas_call(
        functools.partial(_router_kernel, mixer_out=mixer_out is not None),
        grid=(N_TOK // TM,),
        in_specs=pre_specs + [
            pl.BlockSpec((TM, D), lambda i: (i, 0)),
            pl.BlockSpec((1, D), lambda i: (0, 0)),
            pl.BlockSpec((None, 6, D), lambda i: (_cond_of_tile(i, TM), 0, 0)),
            pl.BlockSpec((N_EXP, D), lambda i: (0, 0)),
        ],
        out_specs=pre_out_specs + [
            pl.BlockSpec((TM, SLAB, LANE), lambda i: (i, 0, 0)),
            pl.BlockSpec((N_EXP, TM), lambda i: (0, i)),
            pl.BlockSpec((N_EXP, TM), lambda i: (0, i)),
            pl.BlockSpec((N_EXP, LANE), lambda i: (0, 0)),
            pl.BlockSpec((N_EXP, LANE), lambda i: (i, 0)),
        ],
        out_shape=pre_out_shape + [
            jax.ShapeDtypeStruct((N_TOK, SLAB, LANE), F32),
            jax.ShapeDtypeStruct((N_EXP, N_TOK), I32),
            jax.ShapeDtypeStruct((N_EXP, N_TOK), F32),
            jax.ShapeDtypeStruct((N_EXP, LANE), I32),
            jax.ShapeDtypeStruct((N_TOK // TM * N_EXP, LANE), I32),
        ],
        scratch_shapes=[pltpu.VMEM((N_EXP, LANE), F32), pltpu.VMEM((TM, TM), BF16)],
        compiler_params=_cparams(("arbitrary",)),
        name="router",
    )(*pre_args, y, g.reshape(1, D), mod, router_t_bf16)
    return tuple(outs) if mixer_out is not None else (y, *outs)


class _SegmentCopies:
    def __init__(self, seg_ref, dst_ref, nch_ref, stage_ref, hbm_ref, sem, *, to_hbm):
        self.tabs = (seg_ref, dst_ref, nch_ref)
        self.stage_ref, self.hbm_ref, self.sem, self.to_hbm = stage_ref, hbm_ref, sem, to_hbm

    def _chunk(self, slot, s_row, d_row):
        st = self.stage_ref.at[slot, pl.ds(s_row, CH)]
        hb = self.hbm_ref.at[pl.ds(d_row, CH)]
        sem = self.sem.at[slot]
        return pltpu.make_async_copy(st, hb, sem) if self.to_hbm else pltpu.make_async_copy(hb, st, sem)

    def start(self, tile):
        seg_ref, dst_ref, nch_ref = self.tabs
        slot = tile % 2
        for e in range(N_EXP):
            k = tile * N_EXP + e
            s0, d0 = seg_ref[k], dst_ref[k]

            def start_chunk(c, carry, s0=s0, d0=d0):
                self._chunk(slot, s0 + c * CH, d0 + c * CH).start()
                return carry

            lax.fori_loop(0, nch_ref[k], start_chunk, 0)

    def wait(self, tile):
        nch_ref = self.tabs[2]
        slot = tile % 2
        total = 0
        for e in range(N_EXP):
            total = total + nch_ref[tile * N_EXP + e]

        def wait_chunk(c, carry):
            self._chunk(slot, 0, 0).wait()
            return carry

        lax.fori_loop(0, total, wait_chunk, 0)


def _dispatch_kernel(l1_ref, l2_ref, seg_ref, dst_ref, nch_ref, zs_ref, zn_ref, hs_ref, xs_hbm, stage_ref, sem):
    i = pl.program_id(0)
    n_tiles = pl.num_programs(0)
    copies = _SegmentCopies(seg_ref, dst_ref, nch_ref, stage_ref, xs_hbm, sem, to_hbm=True)

    @pl.when(i == 0)
    def _():
        stage_ref[...] = jnp.zeros_like(stage_ref)

        def zero_chunk(row):
            return pltpu.make_async_copy(stage_ref.at[0, pl.ds(0, CH)], xs_hbm.at[pl.ds(row, CH)], sem.at[0])

        total = 0
        for z in range(N_EXP + 1):
            z0, n = zs_ref[z], zn_ref[z]

            def start(c, carry, z0=z0):
                zero_chunk(z0 + c * CH).start()
                return carry

            lax.fori_loop(0, n, start, 0)
            total = total + n

        def wait(c, carry):
            zero_chunk(0).wait()
            return carry

        lax.fori_loop(0, total, wait, 0)

    slot = i % 2
    base = i * TM

    def place(t8, carry):
        for u in range(SUBLANE):
            t = t8 * SUBLANE + u
            row = hs_ref[t]
            stage_ref[slot, l1_ref[base + t]] = row
            stage_ref[slot, l2_ref[base + t]] = row
        return carry

    lax.fori_loop(0, TM // SUBLANE, place, 0)
    pl.when(i >= 1)(lambda: copies.wait(i - 1))
    copies.start(i)
    pl.when(i == n_tiles - 1)(lambda: copies.wait(i))


def _dispatch(tabs, zero_tabs, hs):
    return pl.pallas_call(
        _dispatch_kernel,
        grid_spec=pltpu.PrefetchScalarGridSpec(
            num_scalar_prefetch=7,
            grid=(N_TOK // TM,),
            in_specs=[pl.BlockSpec((TM, SLAB, LANE), lambda i, *_: (i, 0, 0))],
            out_specs=pl.BlockSpec(memory_space=pl.ANY),
            scratch_shapes=[pltpu.VMEM((2, STAGE_ROWS, SLAB, LANE), F32), pltpu.SemaphoreType.DMA((2,))],
        ),
        out_shape=jax.ShapeDtypeStruct((MAX_TILES * TM_E, SLAB, LANE), F32),
        compiler_params=_cparams(("arbitrary",)),
        name="moe_dispatch",
    )(*tabs, *zero_tabs, hs)


FF_CHUNK = 256
N_FF_CHUNKS = D_FF // FF_CHUNK


def _expert_kernel(te_ref, tv_ref, first_ref, clo_ref, chi_ref, x_ref, wgu_hbm, wd_hbm, o_ref,
                   wg_ref, wu_ref, wd_ref, sg_ref, su_ref, sd_ref, xb_ref, h1_ref, sem, *, layer):
    i = pl.program_id(0)
    e = te_ref[i]
    cur = e % 2

    def copies(expert, c):
        s = c % 2
        cols = slice(c * FF_CHUNK, (c + 1) * FF_CHUNK)
        ucols = slice(D_FF + c * FF_CHUNK, D_FF + (c + 1) * FF_CHUNK)
        return (pltpu.make_async_copy(wgu_hbm.at[layer, expert, :, cols], sg_ref.at[s], sem.at[s]),
                pltpu.make_async_copy(wgu_hbm.at[layer, expert, :, ucols], su_ref.at[s], sem.at[s]),
                pltpu.make_async_copy(wd_hbm.at[layer, expert, cols, :], sd_ref.at[s], sem.at[s]))

    def fetch(expert, c):
        for cp in copies(expert, c):
            cp.start()

    def convert(expert, c, wslot):
        for cp in copies(expert, c):
            cp.wait()
        s = c % 2
        wg_ref[wslot, c] = sg_ref[s].astype(BF16)
        wu_ref[wslot, c] = su_ref[s].astype(BF16)
        wd_ref[wslot, c * FF_CHUNK:(c + 1) * FF_CHUNK, :] = sd_ref[s].astype(BF16)
        if c + 2 < N_FF_CHUNKS:
            fetch(expert, c + 2)

    @pl.when(i == 0)
    def _():
        fetch(e, 0)
        fetch(e, 1)
        for c in range(N_FF_CHUNKS):
            convert(e, c, cur)

    nxt = jnp.minimum(e + 1, N_EXP - 1)

    @pl.when((first_ref[i] == 1) & (e + 1 < N_EXP))
    def _():
        fetch(nxt, 0)
        fetch(nxt, 1)

    valid = tv_ref[i] == 1

    @pl.when(valid)
    def _():
        for k in range(SLAB):
            xb_ref[:, k * LANE:(k + 1) * LANE] = x_ref[pl.ds(k, TM_E, stride=SLAB), :].astype(BF16)
        xb = xb_ref[...]
        for c in range(N_FF_CHUNKS):
            gg = jnp.dot(xb, wg_ref[cur, c], preferred_element_type=F32)
            uu = jnp.dot(xb, wu_ref[cur, c], preferred_element_type=F32)
            h1_ref[:, c * FF_CHUNK:(c + 1) * FF_CHUNK] = (jax.nn.silu(gg) * uu).astype(BF16)
        acc = jnp.dot(h1_ref[...], wd_ref[cur], preferred_element_type=F32)
        for k in range(SLAB):
            o_ref[:, k, :] = acc[:, k * LANE:(k + 1) * LANE]

    @pl.when(jnp.logical_not(valid))
    def _():
        o_ref[...] = jnp.zeros_like(o_ref)

    for c in range(N_FF_CHUNKS):
        pl.when((clo_ref[i] <= c) & (c < chi_ref[i]))(functools.partial(convert, nxt, c, 1 - cur))


def _experts(tile_tabs, xs, w_gu, w_down, layer):
    return pl.pallas_call(
        functools.partial(_expert_kernel, layer=layer),
        grid_spec=pltpu.PrefetchScalarGridSpec(
            num_scalar_prefetch=5,
            grid=(MAX_TILES,),
            in_specs=[
                pl.BlockSpec((TM_E * SLAB, LANE), lambda i, *_: (i, 0)),
                pl.BlockSpec(memory_space=pl.ANY),
                pl.BlockSpec(memory_space=pl.ANY),
            ],
            out_specs=pl.BlockSpec((TM_E, SLAB, LANE), lambda i, *_: (i, 0, 0)),
            scratch_shapes=[
                pltpu.VMEM((2, N_FF_CHUNKS, D, FF_CHUNK), BF16),
                pltpu.VMEM((2, N_FF_CHUNKS, D, FF_CHUNK), BF16),
                pltpu.VMEM((2, D_FF, D), BF16),
                pltpu.VMEM((2, D, FF_CHUNK), F32),
                pltpu.VMEM((2, D, FF_CHUNK), F32),
                pltpu.VMEM((2, FF_CHUNK, D), F32),
                pltpu.VMEM((TM_E, D), BF16),
                pltpu.VMEM((TM_E, D_FF), BF16),
                pltpu.SemaphoreType.DMA((2,)),
            ],
        ),
        out_shape=jax.ShapeDtypeStruct(xs.shape, F32),
        compiler_params=_cparams(("arbitrary",)),
        name="moe_experts",
    )(*tile_tabs, xs.reshape(-1, LANE), w_gu, w_down)


def _combine_kernel(l1_ref, l2_ref, seg_ref, dst_ref, nch_ref, *refs, final):
    if final:
        y_ref, mod_ref, w_ref, fg_ref, ys_hbm, op_ref, ol_ref, stage_ref, g1_ref, g2_ref, sem, o_ref = refs
    else:
        y_ref, mod_ref, w_ref, ys_hbm, o_ref, stage_ref, g1_ref, g2_ref, sem = refs
    i = pl.program_id(0)
    copies = _SegmentCopies(seg_ref, dst_ref, nch_ref, stage_ref, ys_hbm, sem, to_hbm=False)
    pl.when(i == 0)(lambda: copies.start(i))
    pl.when(i + 1 < pl.num_programs(0))(lambda: copies.start(i + 1))
    copies.wait(i)
    slot = i % 2
    base = i * TM

    def pick(t8, carry):
        for u in range(SUBLANE):
            t = t8 * SUBLANE + u
            rows = pl.ds(pl.multiple_of(t * SLAB, SLAB), SLAB)
            g1_ref[rows, :] = stage_ref[slot, l1_ref[base + t]]
            g2_ref[rows, :] = stage_ref[slot, l2_ref[base + t]]
        return carry

    lax.fori_loop(0, TM // SUBLANE, pick, 0)
    w1 = w_ref[:, 0:1]
    w2 = w_ref[:, 1:2]
    for k in range(SLAB):
        cs = slice(k * LANE, (k + 1) * LANE)
        chunk_k = pl.ds(k, TM, stride=SLAB)
        f = w1 * g1_ref[chunk_k, :] + w2 * g2_ref[chunk_k, :]
        o_ref[:, cs] = y_ref[:, cs] + mod_ref[5:6, cs] * f
    if final:
        yn = o_ref[...]
        out = yn * lax.rsqrt(jnp.mean(yn * yn, axis=-1, keepdims=True) + 1e-6) * fg_ref[...]

        def write(part):
            (op_ref, ol_ref)[part][...] = out

        _for_stream_of_tile(write)


def _combine(tabs, y, mod, w_cols, ys, final_g=None):
    final = final_g is not None
    row_spec = pl.BlockSpec((TM, D), lambda i, *_: (i, 0))
    scratch = [
        pltpu.VMEM((2, STAGE_ROWS, SLAB, LANE), F32),
        pltpu.VMEM((TM * SLAB, LANE), F32),
        pltpu.VMEM((TM * SLAB, LANE), F32),
        pltpu.SemaphoreType.DMA((2,)),
    ]
    if final:
        out_specs = [pl.BlockSpec((TM, D), lambda i, *_: (jnp.minimum(i, N_PROMPT_TILES - 1), 0)),
                     pl.BlockSpec((TM, D), lambda i, *_: (jnp.maximum(i - N_PROMPT_TILES, 0), 0))]
        out_shape = [jax.ShapeDtypeStruct((N_P, D), F32), jax.ShapeDtypeStruct((N_L, D), F32)]
        scratch.append(pltpu.VMEM((TM, D), F32))
        extra_specs, extra_args = [pl.BlockSpec((1, D), lambda i, *_: (0, 0))], [final_g.reshape(1, D)]
    else:
        out_specs, out_shape, extra_specs, extra_args = row_spec, jax.ShapeDtypeStruct((N_TOK, D), F32), [], []
    return pl.pallas_call(
        functools.partial(_combine_kernel, final=final),
        grid_spec=pltpu.PrefetchScalarGridSpec(
            num_scalar_prefetch=5,
            grid=(N_TOK // TM,),
            in_specs=[
                row_spec,
                pl.BlockSpec((None, 6, D), lambda i, *_: (_cond_of_tile(i, TM), 0, 0)),
                pl.BlockSpec((TM, 2), lambda i, *_: (i, 0)),
                *extra_specs,
                pl.BlockSpec(memory_space=pl.ANY),
            ],
            out_specs=out_specs,
            scratch_shapes=scratch,
        ),
        out_shape=out_shape,
        compiler_params=_cparams(("arbitrary",)),
        name="moe_combine",
    )(*tabs, y, mod, w_cols, *extra_args, ys)


def _moe(y, g, mod, router, w_gu, w_down, slot, mixer_out=None, final_g=None):
    y, hs, ii, wf, cnt, tbase = _router(y, g, mod, router.T.astype(BF16), mixer_out)
    n_tt = N_TOK // TM
    counts = cnt[:, 0]
    before = tbase[:, 0].reshape(n_tt, N_EXP)
    n_seg = jnp.concatenate([before[1:], counts[None]], axis=0) - before
    region = (counts + CH + TM_E - 1) // TM_E * TM_E
    region_end = jnp.cumsum(region)
    offs = region_end - region
    seg_pad = (n_seg + CH - 1) // CH * CH
    seg = jnp.cumsum(seg_pad, axis=1) - seg_pad
    dst = offs[None, :] + before
    e_ids = jnp.arange(N_EXP, dtype=I32)[:, None]
    shift = jnp.repeat((seg - before).T, TM, axis=1)
    l1 = jnp.sum(jnp.where(ii[0:1] == e_ids, shift, 0), axis=0) + ii[2]
    l2 = jnp.sum(jnp.where(ii[1:2] == e_ids, shift, 0), axis=0) + ii[3]
    tabs = (l1, l2, seg.reshape(-1), dst.reshape(-1), (seg_pad // CH).reshape(-1))
    row0 = jnp.arange(MAX_TILES, dtype=I32) * TM_E
    tile_expert = jnp.minimum(jnp.sum((row0[:, None] >= region_end[None, :]).astype(I32), axis=1), N_EXP - 1)
    tile_valid = (row0 < (offs + counts)[tile_expert]).astype(I32)
    in_region = row0 < region_end[-1]
    j = (row0 - offs[tile_expert]) // TM_E
    n = (region // TM_E)[tile_expert]
    streams = in_region & (tile_expert < N_EXP - 1)
    tile_tabs = (tile_expert, tile_valid, (in_region & (j == 0)).astype(I32),
                 jnp.where(streams, j * N_FF_CHUNKS // n, 0), jnp.where(streams, (j + 1) * N_FF_CHUNKS // n, 0))
    zero_start = offs + counts // CH * CH
    zero_tabs = (jnp.concatenate([zero_start, region_end[-1:]]),
                 jnp.concatenate([region_end - zero_start, MAX_TILES * TM_E - region_end[-1:]]) // CH)
    xs = _dispatch(tabs, zero_tabs, hs)
    ys = _experts(tile_tabs, xs, w_gu, w_down, slot)
    return _combine(tabs, y, mod, wf[0:2].T, ys, final_g)


def kernel(x_prompt, x_sample, state_rglru, cache_k, cache_v, c, c_ctx, norm_g, ada_w, ada_b,
           gm_w_in, gm_ln_g, gm_w_s, gm_b_s, gm_w_out, rg_w_in, rg_conv_w, rg_conv_b, rg_w_gate,
           rg_b_gate, rg_lambda, rg_w_out, att_w_qkv, att_lambda, att_subln_g, att_w_out,
           ff_w_gu, ff_w_down, moe_router, moe_w_gu, moe_w_down, final_g):
    y = (x_prompt.reshape(N_P, D), x_sample.reshape(N_L, D))
    conds = jnp.concatenate([c_ctx[None], c, jnp.zeros((N_COND - 1 - B_L, D), F32)], axis=0)
    mods = _adaln(conds, ada_w, ada_b)
    nb = D_RNN // RNN_BS
    gm_w_in, gm_w_out, rg_w_in, rg_w_out, att_w_qkv, att_w_out, ff_w_gu, ff_w_down = (
        w.astype(BF16) for w in (gm_w_in, gm_w_out, rg_w_in, rg_w_out, att_w_qkv, att_w_out, ff_w_gu, ff_w_down))
    new_rnn = new_k = new_v = None
    for i in range(DEPTH):
        mod = mods[i]
        kind, slot = i % 3, i // 3
        mixer_out = None
        if kind == 0:
            y = _gmlp_mixer(y if i == 0 else (y,), norm_g[i, 0], mod, gm_w_in, gm_ln_g[slot],
                            gm_w_s[slot].astype(BF16), gm_b_s[slot].T, gm_w_out, slot)
        elif kind == 1:
            yx = _mod_matmul(y, norm_g[i, 0], mod, rg_w_in, slot, tn=256, gelu_cols=D_RNN)
            wg_cat = rg_w_gate[slot].transpose(2, 3, 0, 1, 4).reshape(nb, RNN_BS, 4 * RNN_BS).astype(BF16)
            bg_cat = rg_b_gate[slot].reshape(2, 2, nb, RNN_BS).transpose(2, 0, 1, 3).reshape(nb, 1, 4 * RNN_BS)
            rg_args = (yx, rg_conv_w[slot], rg_conv_b[slot], wg_cat, bg_cat, rg_lambda[slot])
            a_p, fin = _rglru(*rg_args, jnp.zeros((2, B_P, D_RNN), F32),
                              row0=0, n_batch=B_P, n_seq=SUBLANE, t_len=T_P)
            a_l, _ = _rglru(*rg_args, state_rglru[:, slot].transpose(1, 0, 2),
                            row0=N_P, n_batch=B_L, n_seq=B_L, t_len=T_L)
            new_rnn = fin.transpose(1, 0, 2)[:, None]
            mixer_out = ((a_p, a_l), rg_w_out, slot)
        else:
            lambda_init = 0.8 - 0.6 * math.exp(-0.3 * i)
            att = (y, norm_g[i, 0], mod, att_w_qkv, slot, att_lambda[slot], att_subln_g[slot], lambda_init)
            a_p, new_k, new_v = _attention(*att, row0=0, n_batch=B_P, t_len=T_P, heads=HEADS, tq=T_P)
            a_l = _attention(*att, row0=N_P, n_batch=B_L, t_len=T_L, heads=4, tq=256,
                             ctx=(cache_k, cache_v, slot), rope_tabs=_rope_tables(T_L))
            new_k = new_k.reshape(B_P, 1, T_P, HEADS, 2 * HD)
            new_v = new_v.reshape(B_P, 1, T_P, HEADS, VD)
            mixer_out = ((a_p, a_l), att_w_out, slot)
        fslot = i // 2
        if i % 2 == 0:
            y = _ffn(y, norm_g[i, 1], mod, ff_w_gu, ff_w_down, fslot, mixer_out)
        else:
            y = _moe(y, norm_g[i, 1], mod, moe_router[fslot], moe_w_gu, moe_w_down, fslot, mixer_out,
                     final_g if i == DEPTH - 1 else None)
    y_prompt, y_sample = y
    return (y_prompt.reshape(B_P, T_P, D), y_sample.reshape(B_L, T_L, D), new_rnn, new_k, new_v)
```

```python
import functools
import math

import jax
import jax.numpy as jnp
import numpy as np
from jax import lax
from jax.experimental import pallas as pl
from jax.experimental.pallas import tpu as pltpu

F32 = jnp.float32
BF16 = jnp.bfloat16
I32 = jnp.int32

D = 1024
DEPTH = 4
B_P, T_P = 32, 256
B_L, T_L = 4, 2048
PAST = 256
N_P = B_P * T_P
N_L = B_L * T_L
N_TOK = N_P + N_L
N_COND = 8
GRID_W = 64
CHUNK = 128
GROUPS = 8
D_RNN = 1280
RNN_BS = 128
CONV_W = 4
RGLRU_C = 8.0
HD = 64
VD = 128
HEADS = 8
ROPE_THETA = 10000.0
LOG2_E = 1.4426950408889634
D_FF = 2816
N_EXP = 8

LANE = 128
SUBLANE = 8
VMEM_LIMIT = 56 * 1024 * 1024
TM = 1024
TM_E = 256
CH = 32
MAX_TILES = -(-(2 * N_TOK + N_EXP * (CH + TM_E - 1)) // TM_E)
STAGE_ROWS = 2 * TM + N_EXP * CH
SLAB = D // LANE


def _cparams(sem):
    return pltpu.CompilerParams(dimension_semantics=sem, vmem_limit_bytes=VMEM_LIMIT)


def _cond_of_tile(i, tm):
    r0 = i * tm
    return jnp.where(r0 < N_P, 0, 1 + (r0 - N_P) // T_L)


def _modnorm(x, g, shift, scale):
    ms = jnp.mean(x * x, axis=-1, keepdims=True)
    h = x * lax.rsqrt(ms + 1e-6) * g
    return h * (1.0 + scale) + shift


def _adaln_kernel(c_ref, w_ref, b_ref, o_ref):
    s = jax.nn.silu(c_ref[...]).astype(BF16)
    o_ref[...] = jnp.dot(s, w_ref[...].astype(BF16), preferred_element_type=F32) + b_ref[...]


def _adaln(conds, ada_w, ada_b):
    tn = 1536
    n6 = 6 * D
    out = pl.pallas_call(
        _adaln_kernel,
        grid=(DEPTH, n6 // tn),
        in_specs=[
            pl.BlockSpec((N_COND, D), lambda l, j: (0, 0)),
            pl.BlockSpec((None, D, tn), lambda l, j: (l, 0, j)),
            pl.BlockSpec((None, 1, tn), lambda l, j: (l, 0, j)),
        ],
        out_specs=pl.BlockSpec((None, N_COND, tn), lambda l, j: (l, 0, j)),
        out_shape=jax.ShapeDtypeStruct((DEPTH, N_COND, n6), F32),
        compiler_params=_cparams(("arbitrary", "arbitrary")),
        name="adaln",
    )(conds, ada_w, ada_b.reshape(DEPTH, 1, n6))
    return out.reshape(DEPTH, N_COND, 6, D)


def _mod_matmul_kernel(x_ref, g_ref, mod_ref, w_ref, o_ref, *, tn, gelu_cols):
    h = _modnorm(x_ref[...], g_ref[...], mod_ref[0:1, :], mod_ref[1:2, :]).astype(BF16)
    for c in range(o_ref.shape[1] // tn):
        acc = jnp.dot(h, w_ref[:, c * tn:(c + 1) * tn], preferred_element_type=F32)
        if c * tn < gelu_cols:
            acc = jax.nn.gelu(acc)
        o_ref[:, c * tn:(c + 1) * tn] = acc


def _mod_matmul(y, g, mod, w_bf16, slot, *, tn, gelu_cols):
    nout = w_bf16.shape[2]
    return pl.pallas_call(
        functools.partial(_mod_matmul_kernel, tn=tn, gelu_cols=gelu_cols),
        grid=(N_TOK // TM,),
        in_specs=[
            pl.BlockSpec((TM, D), lambda i: (i, 0)),
            pl.BlockSpec((1, D), lambda i: (0, 0)),
            pl.BlockSpec((None, 6, D), lambda i: (_cond_of_tile(i, TM), 0, 0)),
            pl.BlockSpec((None, D, nout), lambda i: (slot, 0, 0)),
        ],
        out_specs=pl.BlockSpec((TM, nout), lambda i: (i, 0)),
        out_shape=jax.ShapeDtypeStruct((N_TOK, nout), F32),
        compiler_params=_cparams(("arbitrary",)),
        name="mod_matmul",
    )(y, g.reshape(1, D), mod, w_bf16)


N_PROMPT_TILES = N_P // TM


def _mixer_out_specs(a_parts, w_out_bf16, slot):
    k = a_parts[0].shape[1]
    return [
        pl.BlockSpec((TM, k), lambda i, *_: (jnp.minimum(i, N_PROMPT_TILES - 1), 0)),
        pl.BlockSpec((TM, k), lambda i, *_: (jnp.maximum(i - N_PROMPT_TILES, 0), 0)),
        pl.BlockSpec((None, k, D), lambda i, *_: (slot, 0, 0)),
    ]


def _for_stream_of_tile(body):
    pl.when(pl.program_id(0) < N_PROMPT_TILES)(lambda: body(0))
    pl.when(pl.program_id(0) >= N_PROMPT_TILES)(lambda: body(1))


def _gmlp_kernel(*refs):
    *y_refs, g_ref, mod_ref, win_ref, lng_ref, ws_ref, bs_ref, wout_ref, o_ref, uv_ref, a_ref = refs
    body = functools.partial(_gmlp_tile, g_ref, mod_ref, win_ref, lng_ref, ws_ref, bs_ref, wout_ref,
                             o_ref, uv_ref, a_ref)
    if len(y_refs) == 1:
        body(y_refs[0])
    else:
        _for_stream_of_tile(lambda part: body(y_refs[part]))


def _gmlp_tile(g_ref, mod_ref, win_ref, lng_ref, ws_ref, bs_ref, wout_ref, o_ref, uv_ref, a_ref, y_ref):
    y = y_ref[...]
    h = _modnorm(y, g_ref[...], mod_ref[0:1, :], mod_ref[1:2, :]).astype(BF16)
    tn = 512
    for c in range(2 * D // tn):
        uv = jnp.dot(h, win_ref[:, c * tn:(c + 1) * tn], preferred_element_type=F32)
        uv_ref[:, c * tn:(c + 1) * tn] = jax.nn.gelu(uv)
    v = uv_ref[:, D:]
    vc = v - jnp.mean(v, axis=-1, keepdims=True)
    vn = vc * lax.rsqrt(jnp.mean(vc * vc, axis=-1, keepdims=True) + 1e-5) * lng_ref[...]
    vn = vn.astype(BF16)
    for ci in range(TM // CHUNK):
        r0 = ci * CHUNK
        for g in range(GROUPS):
            c0 = g * LANE
            s = jnp.dot(ws_ref[g], vn[r0:r0 + CHUNK, c0:c0 + LANE], preferred_element_type=F32)
            s = s + bs_ref[:, g:g + 1]
            a_ref[r0:r0 + CHUNK, c0:c0 + LANE] = (uv_ref[r0:r0 + CHUNK, c0:c0 + LANE] * s).astype(BF16)
    o_ref[...] = y + mod_ref[2:3, :] * jnp.dot(a_ref[...], wout_ref[...], preferred_element_type=F32)


def _gmlp_mixer(y_parts, g, mod, w_in_bf16, ln_g, w_s_bf16, b_s_t, w_out_bf16, slot):
    if len(y_parts) == 1:
        y_specs = [pl.BlockSpec((TM, D), lambda i: (i, 0))]
    else:
        y_specs = [pl.BlockSpec((TM, D), lambda i: (jnp.minimum(i, N_PROMPT_TILES - 1), 0)),
                   pl.BlockSpec((TM, D), lambda i: (jnp.maximum(i - N_PROMPT_TILES, 0), 0))]
    return pl.pallas_call(
        _gmlp_kernel,
        grid=(N_TOK // TM,),
        in_specs=y_specs + [
            pl.BlockSpec((1, D), lambda i: (0, 0)),
            pl.BlockSpec((None, 6, D), lambda i: (_cond_of_tile(i, TM), 0, 0)),
            pl.BlockSpec((None, D, 2 * D), lambda i: (slot, 0, 0)),
            pl.BlockSpec((1, D), lambda i: (0, 0)),
            pl.BlockSpec((GROUPS, CHUNK, CHUNK), lambda i: (0, 0, 0)),
            pl.BlockSpec((CHUNK, GROUPS), lambda i: (0, 0)),
            pl.BlockSpec((None, D, D), lambda i: (slot, 0, 0)),
        ],
        out_specs=pl.BlockSpec((TM, D), lambda i: (i, 0)),
        out_shape=jax.ShapeDtypeStruct((N_TOK, D), F32),
        scratch_shapes=[pltpu.VMEM((TM, 2 * D), F32), pltpu.VMEM((TM, D), BF16)],
        compiler_params=_cparams(("arbitrary",)),
        name="gmlp_mixer",
    )(*y_parts, g.reshape(1, D), mod, w_in_bf16, ln_g.reshape(1, D), w_s_bf16, b_s_t, w_out_bf16)


SCAN_UNROLL = 8


def _rglru_kernel(yb_ref, xb_ref, cw_ref, cb_ref, wg_ref, bg_ref, lam_ref, h0_ref,
                  o_ref, fin_ref, af_ref, bf_ref, ab_ref, bb_ref, *, n_seq, t_len, row_chunk):
    stride = t_len + SUBLANE
    tix = lax.broadcasted_iota(I32, (t_len, 1), 0)
    sp_f = jax.nn.softplus(-lam_ref[0:1, :])
    sp_b = jax.nn.softplus(-lam_ref[1:2, :])
    for g in range(n_seq):
        x = xb_ref[g * t_len:(g + 1) * t_len, :]
        xm1 = jnp.where(tix >= 1, pltpu.roll(x, 1, 0), 0.0)
        xp1 = jnp.where(tix <= t_len - 2, pltpu.roll(x, t_len - 1, 0), 0.0)
        xp2 = jnp.where(tix <= t_len - 3, pltpu.roll(x, t_len - 2, 0), 0.0)
        xc = cb_ref[...] + xm1 * cw_ref[0:1, :]
        xc = xc + x * cw_ref[1:2, :]
        xc = xc + xp1 * cw_ref[2:3, :]
        xc = xc + xp2 * cw_ref[3:4, :]
        for r0 in range(0, t_len, row_chunk):
            xcc = xc[r0:r0 + row_chunk]
            gates = jnp.dot(xcc.astype(BF16), wg_ref[...], preferred_element_type=F32) + bg_ref[...]
            d0 = g * stride + r0
            for d, (sp, a_ref, b_ref) in enumerate(((sp_f, af_ref, bf_ref), (sp_b, ab_ref, bb_ref))):
                r = jax.nn.sigmoid(gates[:, (2 * d) * LANE:(2 * d + 1) * LANE])
                i = jax.nn.sigmoid(gates[:, (2 * d + 1) * LANE:(2 * d + 2) * LANE])
                log_a = (-RGLRU_C * r) * sp
                a = jnp.exp(log_a)
                one_minus_a2 = jnp.tanh(-log_a) * (a * a + 1.0)
                a_ref[d0:d0 + row_chunk, :] = a
                b_ref[d0:d0 + row_chunk, :] = jnp.sqrt(one_minus_a2) * (i * xcc)

    def block_scan(a_ref, b_ref, rows, h_in):
        a = [a_ref[r, :] for r in rows]
        b = [b_ref[r, :] for r in rows]
        d = 1
        while d < SCAN_UNROLL:
            a, b = ([a[u] if u < d else a[u] * a[u - d] for u in range(SCAN_UNROLL)],
                    [b[u] if u < d else a[u] * b[u - d] + b[u] for u in range(SCAN_UNROLL)])
            d *= 2
        h = [a[u] * h_in + b[u] for u in range(SCAN_UNROLL)]
        for r, hu in zip(rows, h):
            b_ref[r, :] = hu
        return h[-1]

    def steps(s8, carry):
        hf, hb = carry
        s0 = s8 * SCAN_UNROLL
        hf = block_scan(af_ref, bf_ref, [pl.ds(s0 + u, n_seq, stride=stride) for u in range(SCAN_UNROLL)], hf)
        hb = block_scan(ab_ref, bb_ref,
                        [pl.ds(t_len - 1 - s0 - u, n_seq, stride=stride) for u in range(SCAN_UNROLL)], hb)
        return hf, hb

    hf, hb = lax.fori_loop(0, t_len // SCAN_UNROLL, steps, (h0_ref[0], h0_ref[1]))
    fin_ref[0] = hf
    fin_ref[1] = hb
    for g in range(n_seq):
        rs = slice(g * stride, g * stride + t_len)
        hsum = bf_ref[rs, :] + bb_ref[rs, :]
        o_ref[g * t_len:(g + 1) * t_len, :] = (hsum * yb_ref[g * t_len:(g + 1) * t_len, :]).astype(BF16)


def _rglru(yx, cw, cb, wg_cat, bg_cat, lam, h0, *, row0, n_batch, n_seq, t_len):
    rows = n_seq * t_len
    nb = D_RNN // RNN_BS
    rb0 = row0 // rows
    kern = functools.partial(_rglru_kernel, n_seq=n_seq, t_len=t_len, row_chunk=min(t_len, 512))
    scan_buf = pltpu.VMEM((n_seq * (t_len + SUBLANE), RNN_BS), F32)
    return pl.pallas_call(
        kern,
        grid=(n_batch // n_seq, nb),
        in_specs=[
            pl.BlockSpec((rows, RNN_BS), lambda s, c: (rb0 + s, c)),
            pl.BlockSpec((rows, RNN_BS), lambda s, c: (rb0 + s, nb + c)),
            pl.BlockSpec((CONV_W, RNN_BS), lambda s, c: (0, c)),
            pl.BlockSpec((1, RNN_BS), lambda s, c: (0, c)),
            pl.BlockSpec((None, RNN_BS, 4 * RNN_BS), lambda s, c: (c, 0, 0)),
            pl.BlockSpec((None, 1, 4 * RNN_BS), lambda s, c: (c, 0, 0)),
            pl.BlockSpec((2, RNN_BS), lambda s, c: (0, c)),
            pl.BlockSpec((2, n_seq, RNN_BS), lambda s, c: (0, s, c)),
        ],
        out_specs=[
            pl.BlockSpec((rows, RNN_BS), lambda s, c: (s, c)),
            pl.BlockSpec((2, n_seq, RNN_BS), lambda s, c: (0, s, c)),
        ],
        out_shape=[
            jax.ShapeDtypeStruct((n_batch * t_len, D_RNN), BF16),
            jax.ShapeDtypeStruct((2, n_batch, D_RNN), F32),
        ],
        scratch_shapes=[scan_buf, scan_buf, scan_buf, scan_buf],
        compiler_params=_cparams(("arbitrary", "arbitrary")),
        name="rglru",
    )(yx, yx, cw, cb.reshape(1, D_RNN), wg_cat, bg_cat, lam, h0)


def _rope(x, cos, sin_signed, first_of_pair):
    partner = jnp.where(first_of_pair, pltpu.roll(x, LANE - 16, 1), pltpu.roll(x, 16, 1))
    return x * cos + partner * sin_signed


def _attn_kernel(*refs, heads, t_len, tq, s_len, rope, lambda_init, mxu_denominator):
    if rope:
        (lam_ref, g_ref, y_ref, ng_ref, mod_ref, wq_ref, wk_ref, wv_ref, ck_ref, cv_ref, cq_ref, sq_ref,
         ckk_ref, skk_ref, o_ref, h_ref, kk_ref, vv_ref) = refs
    else:
        (lam_ref, g_ref, y_ref, ng_ref, mod_ref, wq_ref, wk_ref, wv_ref,
         o_ref, ko_ref, vo_ref, h_ref, kk_ref, vv_ref) = refs
    lane = lax.broadcasted_iota(I32, (1, LANE), 1)
    first_of_pair = (lane & 16) == 0
    map0 = lane < HD

    @pl.when((pl.program_id(1) == 0) & (pl.program_id(2) == 0))
    def _():
        h_ref[...] = _modnorm(y_ref[...], ng_ref[...], mod_ref[0:1, :], mod_ref[1:2, :]).astype(BF16)

    @pl.when(pl.program_id(2) == 0)
    def _():
        k_all = jnp.dot(h_ref[...], wk_ref[...], preferred_element_type=F32)
        v_all = jnp.dot(h_ref[...], wv_ref[...], preferred_element_type=F32)
        if not rope:
            ko_ref[...] = k_all
            vo_ref[...] = v_all
        for j in range(heads):
            cs = slice(j * LANE, (j + 1) * LANE)
            k = k_all[:, cs]
            if rope:
                k = _rope(k, ckk_ref[...], skk_ref[...], first_of_pair)
            kk_ref[j, 0:t_len, :] = k.astype(BF16)
            vv_ref[j, 0:t_len, 0:LANE] = v_all[:, cs].astype(BF16)
            if s_len > t_len:
                ctx_rows = pl.ds(pl.program_id(1) * heads + j, s_len - t_len, stride=HEADS)
                kk_ref[j, t_len:s_len, :] = ck_ref[ctx_rows, :].astype(BF16)
                vv_ref[j, t_len:s_len, 0:LANE] = cv_ref[ctx_rows, :].astype(BF16)
            if mxu_denominator:
                vv_ref[j, :, LANE:2 * LANE] = jnp.ones((s_len, LANE), BF16)

    lp = lam_ref[...]
    lam = (jnp.exp(jnp.sum(lp[0:1] * lp[1:2], axis=-1, keepdims=True))
           - jnp.exp(jnp.sum(lp[2:3] * lp[3:4], axis=-1, keepdims=True)) + lambda_init)
    q_rows = pl.ds(pl.multiple_of(pl.program_id(2) * tq, tq), tq)
    q_all = jnp.dot(h_ref[q_rows, :], wq_ref[...], preferred_element_type=F32)
    for j in range(heads):
        cs = slice(j * LANE, (j + 1) * LANE)
        q = q_all[:, cs]
        if rope:
            q = _rope(q, cq_ref[...], sq_ref[...], first_of_pair)
        q = q * (HD ** -0.5 * LOG2_E)
        q2 = jnp.concatenate([jnp.where(map0, q, 0.0), jnp.where(map0, 0.0, q)], axis=0).astype(BF16)
        s = lax.dot_general(q2, kk_ref[j], (((1,), (1,)), ((), ())), preferred_element_type=F32)
        e = jnp.exp2(s - jnp.max(s, axis=-1, keepdims=True))
        if mxu_denominator:
            nd = jnp.dot(e.astype(BF16), vv_ref[j], preferred_element_type=F32)
            av = nd[:, 0:LANE] * (1.0 / nd[:, LANE:LANE + 1])
            o = av[0:tq] - lam * av[tq:2 * tq]
        else:
            p = e * (1.0 / jnp.sum(e, axis=-1, keepdims=True))
            w = p[0:tq] - lam * p[tq:2 * tq]
            o = jnp.dot(w.astype(BF16), vv_ref[j], preferred_element_type=F32)
        o = o * lax.rsqrt(jnp.mean(o * o, axis=-1, keepdims=True) + 1e-6) * g_ref[...]
        o_ref[:, cs] = (o * (1.0 - lambda_init)).astype(BF16)


def _attention(y, norm_g, mod, w_qkv_bf16, wslot, lam_p, subln_g, lambda_init, *, row0, n_batch, t_len, heads, tq,
               ctx=None, rope_tabs=None):
    rope = rope_tabs is not None
    s_len = t_len + (PAST if rope else 0)
    w = heads * LANE
    nhb = HEADS // heads
    nq = t_len // tq
    rk0 = row0 // t_len
    cond0 = 1 if rope else 0
    mxu_denominator = s_len >= 1024
    kern = functools.partial(_attn_kernel, heads=heads, t_len=t_len, tq=tq, s_len=s_len, rope=rope,
                             lambda_init=lambda_init, mxu_denominator=mxu_denominator)
    v_width = 2 * LANE if mxu_denominator else LANE
    in_specs = [
        pl.BlockSpec((4, HD), lambda b, h, q: (0, 0)),
        pl.BlockSpec((1, VD), lambda b, h, q: (0, 0)),
        pl.BlockSpec((t_len, D), lambda b, h, q: (rk0 + b, 0), pipeline_mode=pl.Buffered(1 if rope else 2)),
        pl.BlockSpec((1, D), lambda b, h, q: (0, 0)),
        pl.BlockSpec((None, 6, D), lambda b, h, q: (b * cond0 + cond0, 0, 0)),
        pl.BlockSpec((None, D, w), lambda b, h, q: (wslot, 0, h)),
        pl.BlockSpec((None, D, w), lambda b, h, q: (wslot, 0, nhb + h)),
        pl.BlockSpec((None, D, w), lambda b, h, q: (wslot, 0, 2 * nhb + h)),
    ]
    args = [lam_p, subln_g.reshape(1, VD), y, norm_g.reshape(1, D), mod, w_qkv_bf16, w_qkv_bf16, w_qkv_bf16]
    row_block = pl.BlockSpec((tq, w), lambda b, h, q: (b * nq + q, h))
    out_specs, out_shape = row_block, jax.ShapeDtypeStruct((n_batch * t_len, D), BF16)
    if not rope:
        seq_block = pl.BlockSpec((t_len, w), lambda b, h, q: (b, h))
        out_specs = [row_block, seq_block, seq_block]
        out_shape = [out_shape] + [jax.ShapeDtypeStruct((n_batch * t_len, D), F32)] * 2
    if rope:
        cos_t, sin_t = rope_tabs
        cache_k, cache_v, slot = ctx
        n_slots = cache_k.shape[1]
        ctx_k = cache_k.reshape(n_batch, n_slots, PAST * HEADS, LANE)
        ctx_v = cache_v.reshape(n_batch, n_slots, PAST * HEADS, LANE)
        in_specs += [
            pl.BlockSpec((None, None, PAST * HEADS, LANE), lambda b, h, q: (b, slot, 0, 0)),
            pl.BlockSpec((None, None, PAST * HEADS, LANE), lambda b, h, q: (b, slot, 0, 0)),
            pl.BlockSpec((tq, LANE), lambda b, h, q: (q, 0)),
            pl.BlockSpec((tq, LANE), lambda b, h, q: (q, 0)),
            pl.BlockSpec((t_len, LANE), lambda b, h, q: (0, 0)),
            pl.BlockSpec((t_len, LANE), lambda b, h, q: (0, 0)),
        ]
        args += [ctx_k, ctx_v, cos_t, sin_t, cos_t, sin_t]
    return pl.pallas_call(
        kern,
        grid=(n_batch, nhb, nq),
        in_specs=in_specs,
        out_specs=out_specs,
        out_shape=out_shape,
        scratch_shapes=[pltpu.VMEM((t_len, D), BF16), pltpu.VMEM((heads, s_len, LANE), BF16),
                        pltpu.VMEM((heads, s_len, v_width), BF16)],
        compiler_params=_cparams(("arbitrary", "arbitrary", "arbitrary")),
        name="diff_attn",
    )(*args)


def _rope_tables(t_len):
    t = np.arange(t_len)
    half = HD // 2
    freqs = (ROPE_THETA ** (-np.arange(0, half, 2, dtype=np.float32) / half)).astype(np.float32)
    ang_r = (t // GRID_W).astype(np.float32)[:, None] * freqs[None]
    ang_c = (t % GRID_W).astype(np.float32)[:, None] * freqs[None]
    cr, sr, cc, sc = np.cos(ang_r), np.sin(ang_r), np.cos(ang_c), np.sin(ang_c)
    cos64 = np.concatenate([cr, cr, cc, cc], axis=-1)
    sin64 = np.concatenate([-sr, sr, -sc, sc], axis=-1)
    return (jnp.asarray(np.concatenate([cos64, cos64], axis=-1), F32),
            jnp.asarray(np.concatenate([sin64, sin64], axis=-1), F32))


def _swiglu_rows(h_bf16, wgu_ref, wd_ref, h1_ref):
    tn = 256
    for c in range(D_FF // tn):
        gg = jnp.dot(h_bf16, wgu_ref[:, c * tn:(c + 1) * tn], preferred_element_type=F32)
        uu = jnp.dot(h_bf16, wgu_ref[:, D_FF + c * tn:D_FF + (c + 1) * tn], preferred_element_type=F32)
        h1_ref[:, c * tn:(c + 1) * tn] = (jax.nn.silu(gg) * uu).astype(BF16)
    return jnp.dot(h1_ref[...], wd_ref[...], preferred_element_type=F32)


def _ffn_kernel(*refs, mixer_out):
    if mixer_out:
        ap_ref, al_ref, wout_ref, y_ref, g_ref, mod_ref, wgu_ref, wd_ref, o_ref, h1_ref = refs
    else:
        y_ref, g_ref, mod_ref, wgu_ref, wd_ref, o_ref, h1_ref = refs

    def body(a_ref):
        y = y_ref[...]
        if a_ref is not None:
            y = y + mod_ref[2:3, :] * jnp.dot(a_ref[...], wout_ref[...], preferred_element_type=F32)
        h = _modnorm(y, g_ref[...], mod_ref[3:4, :], mod_ref[4:5, :]).astype(BF16)
        o_ref[...] = y + mod_ref[5:6, :] * _swiglu_rows(h, wgu_ref, wd_ref, h1_ref)

    if mixer_out:
        _for_stream_of_tile(lambda part: body((ap_ref, al_ref)[part]))
    else:
        body(None)


def _ffn(y, g, mod, wgu_bf16, wd_bf16, slot, mixer_out=None):
    pre_specs, pre_args = [], []
    if mixer_out is not None:
        pre_specs = _mixer_out_specs(*mixer_out)
        pre_args = [*mixer_out[0], mixer_out[1]]
    return pl.pallas_call(
        functools.partial(_ffn_kernel, mixer_out=mixer_out is not None),
        grid=(N_TOK // TM,),
        in_specs=pre_specs + [
            pl.BlockSpec((TM, D), lambda i: (i, 0)),
            pl.BlockSpec((1, D), lambda i: (0, 0)),
            pl.BlockSpec((None, 6, D), lambda i: (_cond_of_tile(i, TM), 0, 0)),
            pl.BlockSpec((None, D, 2 * D_FF), lambda i: (slot, 0, 0), pipeline_mode=pl.Buffered(1)),
            pl.BlockSpec((None, D_FF, D), lambda i: (slot, 0, 0), pipeline_mode=pl.Buffered(1)),
        ],
        out_specs=pl.BlockSpec((TM, D), lambda i: (i, 0)),
        out_shape=jax.ShapeDtypeStruct((N_TOK, D), F32),
        scratch_shapes=[pltpu.VMEM((TM, D_FF), BF16)],
        compiler_params=_cparams(("arbitrary",)),
        name="ffn",
    )(*pre_args, y, g.reshape(1, D), mod, wgu_bf16, wd_bf16)


def _router_kernel(*refs, mixer_out):
    if mixer_out:
        (ap_ref, al_ref, wout_ref, y_ref, g_ref, mod_ref, rt_ref,
         ym_ref, hs_ref, ii_ref, wf_ref, cnt_ref, tb_ref, carry_ref, tri_ref) = refs

        def add_mixer_out(part):
            acc = jnp.dot((ap_ref, al_ref)[part][...], wout_ref[...], preferred_element_type=F32)
            ym_ref[...] = y_ref[...] + mod_ref[2:3, :] * acc

        _for_stream_of_tile(add_mixer_out)
        y = ym_ref[...]
    else:
        y_ref, g_ref, mod_ref, rt_ref, hs_ref, ii_ref, wf_ref, cnt_ref, tb_ref, carry_ref, tri_ref = refs
        y = y_ref[...]
    tm = y_ref.shape[0]

    @pl.when(pl.program_id(0) == 0)
    def _():
        carry_ref[...] = jnp.zeros_like(carry_ref)
        before = (lax.broadcasted_iota(I32, (tm, tm), 0) < lax.broadcasted_iota(I32, (tm, tm), 1))
        tri_ref[...] = jnp.where(before, 1.0, 0.0).astype(BF16)

    tb_ref[...] = carry_ref[...].astype(I32)

    h = _modnorm(y, g_ref[...], mod_ref[3:4, :], mod_ref[4:5, :])
    for k in range(SLAB):
        hs_ref[:, k, :] = h[:, k * LANE:(k + 1) * LANE]
    logits = lax.dot_general(rt_ref[...], h.astype(BF16), (((1,), (1,)), ((), ())),
                             preferred_element_type=F32)
    eidx = lax.broadcasted_iota(I32, (N_EXP, tm), 0)
    eidx_f = eidx.astype(F32)
    m1 = jnp.max(logits, axis=0, keepdims=True)
    i1 = jnp.min(jnp.where(logits == m1, eidx_f, float(N_EXP)), axis=0, keepdims=True)
    oh1 = eidx_f == i1
    rest = jnp.where(oh1, -jnp.inf, logits)
    m2 = jnp.max(rest, axis=0, keepdims=True)
    i2 = jnp.min(jnp.where(rest == m2, eidx_f, float(N_EXP)), axis=0, keepdims=True)
    oh2 = eidx_f == i2
    i1 = i1.astype(I32)
    i2 = i2.astype(I32)
    e2 = jnp.exp(m2 - m1)
    w1 = 1.0 / (1.0 + e2)
    w2 = e2 / (1.0 + e2)
    sel = jnp.where(oh1 | oh2, 1.0, 0.0)
    cum = jnp.dot(sel.astype(BF16), tri_ref[...], preferred_element_type=F32)
    cum = cum + carry_ref[:, 0:1]
    r1 = jnp.sum(jnp.where(oh1, cum, 0.0), axis=0, keepdims=True).astype(I32)
    r2 = jnp.sum(jnp.where(oh2, cum, 0.0), axis=0, keepdims=True).astype(I32)
    ii_ref[...] = jnp.where(eidx == 0, i1, jnp.where(eidx == 1, i2, jnp.where(eidx == 2, r1,
                            jnp.where(eidx == 3, r2, 0))))
    wf_ref[...] = jnp.where(eidx == 0, w1, jnp.where(eidx == 1, w2, 0.0))
    carry_ref[...] = carry_ref[...] + jnp.sum(sel, axis=1, keepdims=True)
    cnt_ref[...] = carry_ref[...].astype(I32)


def _router(y, g, mod, router_t_bf16, mixer_out=None):
    pre_specs, pre_args, pre_out_specs, pre_out_shape = [], [], [], []
    if mixer_out is not None:
        pre_specs = _mixer_out_specs(*mixer_out)
        pre_args = [*mixer_out[0], mixer_out[1]]
        pre_out_specs = [pl.BlockSpec((TM, D), lambda i: (i, 0))]
        pre_out_shape = [jax.ShapeDtypeStruct((N_TOK, D), F32)]
    outs = pl.pallas_call(
        functools.partial(_router_kernel, mixer_out=mixer_out is not None),
        grid=(N_TOK // TM,),
        in_specs=pre_specs + [
            pl.BlockSpec((TM, D), lambda i: (i, 0)),
            pl.BlockSpec((1, D), lambda i: (0, 0)),
            pl.BlockSpec((None, 6, D), lambda i: (_cond_of_tile(i, TM), 0, 0)),
            pl.BlockSpec((N_EXP, D), lambda i: (0, 0)),
        ],
        out_specs=pre_out_specs + [
            pl.BlockSpec((TM, SLAB, LANE), lambda i: (i, 0, 0)),
            pl.BlockSpec((N_EXP, TM), lambda i: (0, i)),
            pl.BlockSpec((N_EXP, TM), lambda i: (0, i)),
            pl.BlockSpec((N_EXP, LANE), lambda i: (0, 0)),
            pl.BlockSpec((N_EXP, LANE), lambda i: (i, 0)),
        ],
        out_shape=pre_out_shape + [
            jax.ShapeDtypeStruct((N_TOK, SLAB, LANE), F32),
            jax.ShapeDtypeStruct((N_EXP, N_TOK), I32),
            jax.ShapeDtypeStruct((N_EXP, N_TOK), F32),
            jax.ShapeDtypeStruct((N_EXP, LANE), I32),
            jax.ShapeDtypeStruct((N_TOK // TM * N_EXP, LANE), I32),
        ],
        scratch_shapes=[pltpu.VMEM((N_EXP, LANE), F32), pltpu.VMEM((TM, TM), BF16)],
        compiler_params=_cparams(("arbitrary",)),
        name="router",
    )(*pre_args, y, g.reshape(1, D), mod, router_t_bf16)
    return tuple(outs) if mixer_out is not None else (y, *outs)


class _SegmentCopies:
    def __init__(self, seg_ref, dst_ref, nch_ref, stage_ref, hbm_ref, sem, *, to_hbm):
        self.tabs = (seg_ref, dst_ref, nch_ref)
        self.stage_ref, self.hbm_ref, self.sem, self.to_hbm = stage_ref, hbm_ref, sem, to_hbm

    def _chunk(self, slot, s_row, d_row):
        st = self.stage_ref.at[slot, pl.ds(s_row, CH)]
        hb = self.hbm_ref.at[pl.ds(d_row, CH)]
        sem = self.sem.at[slot]
        return pltpu.make_async_copy(st, hb, sem) if self.to_hbm else pltpu.make_async_copy(hb, st, sem)

    def start(self, tile):
        seg_ref, dst_ref, nch_ref = self.tabs
        slot = tile % 2
        for e in range(N_EXP):
            k = tile * N_EXP + e
            s0, d0 = seg_ref[k], dst_ref[k]

            def start_chunk(c, carry, s0=s0, d0=d0):
                self._chunk(slot, s0 + c * CH, d0 + c * CH).start()
                return carry

            lax.fori_loop(0, nch_ref[k], start_chunk, 0)

    def wait(self, tile):
        nch_ref = self.tabs[2]
        slot = tile % 2
        total = 0
        for e in range(N_EXP):
            total = total + nch_ref[tile * N_EXP + e]

        def wait_chunk(c, carry):
            self._chunk(slot, 0, 0).wait()
            return carry

        lax.fori_loop(0, total, wait_chunk, 0)


def _dispatch_kernel(l1_ref, l2_ref, seg_ref, dst_ref, nch_ref, zs_ref, zn_ref, hs_ref, xs_hbm, stage_ref, sem):
    i = pl.program_id(0)
    n_tiles = pl.num_programs(0)
    copies = _SegmentCopies(seg_ref, dst_ref, nch_ref, stage_ref, xs_hbm, sem, to_hbm=True)

    @pl.when(i == 0)
    def _():
        stage_ref[...] = jnp.zeros_like(stage_ref)

        def zero_chunk(row):
            return pltpu.make_async_copy(stage_ref.at[0, pl.ds(0, CH)], xs_hbm.at[pl.ds(row, CH)], sem.at[0])

        total = 0
        for z in range(N_EXP + 1):
            z0, n = zs_ref[z], zn_ref[z]

            def start(c, carry, z0=z0):
                zero_chunk(z0 + c * CH).start()
                return carry

            lax.fori_loop(0, n, start, 0)
            total = total + n

        def wait(c, carry):
            zero_chunk(0).wait()
            return carry

        lax.fori_loop(0, total, wait, 0)

    slot = i % 2
    base = i * TM

    def place(t8, carry):
        for u in range(SUBLANE):
            t = t8 * SUBLANE + u
            row = hs_ref[t]
            stage_ref[slot, l1_ref[base + t]] = row
            stage_ref[slot, l2_ref[base + t]] = row
        return carry

    lax.fori_loop(0, TM // SUBLANE, place, 0)
    pl.when(i >= 1)(lambda: copies.wait(i - 1))
    copies.start(i)
    pl.when(i == n_tiles - 1)(lambda: copies.wait(i))


def _dispatch(tabs, zero_tabs, hs):
    return pl.pallas_call(
        _dispatch_kernel,
        grid_spec=pltpu.PrefetchScalarGridSpec(
            num_scalar_prefetch=7,
            grid=(N_TOK // TM,),
            in_specs=[pl.BlockSpec((TM, SLAB, LANE), lambda i, *_: (i, 0, 0))],
            out_specs=pl.BlockSpec(memory_space=pl.ANY),
            scratch_shapes=[pltpu.VMEM((2, STAGE_ROWS, SLAB, LANE), F32), pltpu.SemaphoreType.DMA((2,))],
        ),
        out_shape=jax.ShapeDtypeStruct((MAX_TILES * TM_E, SLAB, LANE), F32),
        compiler_params=_cparams(("arbitrary",)),
        name="moe_dispatch",
    )(*tabs, *zero_tabs, hs)


FF_CHUNK = 256
N_FF_CHUNKS = D_FF // FF_CHUNK


def _expert_kernel(te_ref, tv_ref, first_ref, clo_ref, chi_ref, x_ref, wgu_hbm, wd_hbm, o_ref,
                   wg_ref, wu_ref, wd_ref, sg_ref, su_ref, sd_ref, xb_ref, h1_ref, sem, *, layer):
    i = pl.program_id(0)
    e = te_ref[i]
    cur = e % 2

    def copies(expert, c):
        s = c % 2
        cols = slice(c * FF_CHUNK, (c + 1) * FF_CHUNK)
        ucols = slice(D_FF + c * FF_CHUNK, D_FF + (c + 1) * FF_CHUNK)
        return (pltpu.make_async_copy(wgu_hbm.at[layer, expert, :, cols], sg_ref.at[s], sem.at[s]),
                pltpu.make_async_copy(wgu_hbm.at[layer, expert, :, ucols], su_ref.at[s], sem.at[s]),
                pltpu.make_async_copy(wd_hbm.at[layer, expert, cols, :], sd_ref.at[s], sem.at[s]))

    def fetch(expert, c):
        for cp in copies(expert, c):
            cp.start()

    def convert(expert, c, wslot):
        for cp in copies(expert, c):
            cp.wait()
        s = c % 2
        wg_ref[wslot, c] = sg_ref[s].astype(BF16)
        wu_ref[wslot, c] = su_ref[s].astype(BF16)
        wd_ref[wslot, c * FF_CHUNK:(c + 1) * FF_CHUNK, :] = sd_ref[s].astype(BF16)
        if c + 2 < N_FF_CHUNKS:
            fetch(expert, c + 2)

    @pl.when(i == 0)
    def _():
        fetch(e, 0)
        fetch(e, 1)
        for c in range(N_FF_CHUNKS):
            convert(e, c, cur)

    nxt = jnp.minimum(e + 1, N_EXP - 1)

    @pl.when((first_ref[i] == 1) & (e + 1 < N_EXP))
    def _():
        fetch(nxt, 0)
        fetch(nxt, 1)

    valid = tv_ref[i] == 1

    @pl.when(valid)
    def _():
        for k in range(SLAB):
            xb_ref[:, k * LANE:(k + 1) * LANE] = x_ref[pl.ds(k, TM_E, stride=SLAB), :].astype(BF16)
        xb = xb_ref[...]
        for c in range(N_FF_CHUNKS):
            gg = jnp.dot(xb, wg_ref[cur, c], preferred_element_type=F32)
            uu = jnp.dot(xb, wu_ref[cur, c], preferred_element_type=F32)
            h1_ref[:, c * FF_CHUNK:(c + 1) * FF_CHUNK] = (jax.nn.silu(gg) * uu).astype(BF16)
        acc = jnp.dot(h1_ref[...], wd_ref[cur], preferred_element_type=F32)
        for k in range(SLAB):
            o_ref[:, k, :] = acc[:, k * LANE:(k + 1) * LANE]

    @pl.when(jnp.logical_not(valid))
    def _():
        o_ref[...] = jnp.zeros_like(o_ref)

    for c in range(N_FF_CHUNKS):
        pl.when((clo_ref[i] <= c) & (c < chi_ref[i]))(functools.partial(convert, nxt, c, 1 - cur))


def _experts(tile_tabs, xs, w_gu, w_down, layer):
    return pl.pallas_call(
        functools.partial(_expert_kernel, layer=layer),
        grid_spec=pltpu.PrefetchScalarGridSpec(
            num_scalar_prefetch=5,
            grid=(MAX_TILES,),
            in_specs=[
                pl.BlockSpec((TM_E * SLAB, LANE), lambda i, *_: (i, 0)),
                pl.BlockSpec(memory_space=pl.ANY),
                pl.BlockSpec(memory_space=pl.ANY),
            ],
            out_specs=pl.BlockSpec((TM_E, SLAB, LANE), lambda i, *_: (i, 0, 0)),
            scratch_shapes=[
                pltpu.VMEM((2, N_FF_CHUNKS, D, FF_CHUNK), BF16),
                pltpu.VMEM((2, N_FF_CHUNKS, D, FF_CHUNK), BF16),
                pltpu.VMEM((2, D_FF, D), BF16),
                pltpu.VMEM((2, D, FF_CHUNK), F32),
                pltpu.VMEM((2, D, FF_CHUNK), F32),
                pltpu.VMEM((2, FF_CHUNK, D), F32),
                pltpu.VMEM((TM_E, D), BF16),
                pltpu.VMEM((TM_E, D_FF), BF16),
                pltpu.SemaphoreType.DMA((2,)),
            ],
        ),
        out_shape=jax.ShapeDtypeStruct(xs.shape, F32),
        compiler_params=_cparams(("arbitrary",)),
        name="moe_experts",
    )(*tile_tabs, xs.reshape(-1, LANE), w_gu, w_down)


def _combine_kernel(l1_ref, l2_ref, seg_ref, dst_ref, nch_ref, *refs, final):
    if final:
        y_ref, mod_ref, w_ref, fg_ref, ys_hbm, op_ref, ol_ref, stage_ref, g1_ref, g2_ref, sem, o_ref = refs
    else:
        y_ref, mod_ref, w_ref, ys_hbm, o_ref, stage_ref, g1_ref, g2_ref, sem = refs
    i = pl.program_id(0)
    copies = _SegmentCopies(seg_ref, dst_ref, nch_ref, stage_ref, ys_hbm, sem, to_hbm=False)
    pl.when(i == 0)(lambda: copies.start(i))
    pl.when(i + 1 < pl.num_programs(0))(lambda: copies.start(i + 1))
    copies.wait(i)
    slot = i % 2
    base = i * TM

    def pick(t8, carry):
        for u in range(SUBLANE):
            t = t8 * SUBLANE + u
            rows = pl.ds(pl.multiple_of(t * SLAB, SLAB), SLAB)
            g1_ref[rows, :] = stage_ref[slot, l1_ref[base + t]]
            g2_ref[rows, :] = stage_ref[slot, l2_ref[base + t]]
        return carry

    lax.fori_loop(0, TM // SUBLANE, pick, 0)
    w1 = w_ref[:, 0:1]
    w2 = w_ref[:, 1:2]
    for k in range(SLAB):
        cs = slice(k * LANE, (k + 1) * LANE)
        chunk_k = pl.ds(k, TM, stride=SLAB)
        f = w1 * g1_ref[chunk_k, :] + w2 * g2_ref[chunk_k, :]
        o_ref[:, cs] = y_ref[:, cs] + mod_ref[5:6, cs] * f
    if final:
        yn = o_ref[...]
        out = yn * lax.rsqrt(jnp.mean(yn * yn, axis=-1, keepdims=True) + 1e-6) * fg_ref[...]

        def write(part):
            (op_ref, ol_ref)[part][...] = out

        _for_stream_of_tile(write)


def _combine(tabs, y, mod, w_cols, ys, final_g=None):
    final = final_g is not None
    row_spec = pl.BlockSpec((TM, D), lambda i, *_: (i, 0))
    scratch = [
        pltpu.VMEM((2, STAGE_ROWS, SLAB, LANE), F32),
        pltpu.VMEM((TM * SLAB, LANE), F32),
        pltpu.VMEM((TM * SLAB, LANE), F32),
        pltpu.SemaphoreType.DMA((2,)),
    ]
    if final:
        out_specs = [pl.BlockSpec((TM, D), lambda i, *_: (jnp.minimum(i, N_PROMPT_TILES - 1), 0)),
                     pl.BlockSpec((TM, D), lambda i, *_: (jnp.maximum(i - N_PROMPT_TILES, 0), 0))]
        out_shape = [jax.ShapeDtypeStruct((N_P, D), F32), jax.ShapeDtypeStruct((N_L, D), F32)]
        scratch.append(pltpu.VMEM((TM, D), F32))
        extra_specs, extra_args = [pl.BlockSpec((1, D), lambda i, *_: (0, 0))], [final_g.reshape(1, D)]
    else:
        out_specs, out_shape, extra_specs, extra_args = row_spec, jax.ShapeDtypeStruct((N_TOK, D), F32), [], []
    return pl.pallas_call(
        functools.partial(_combine_kernel, final=final),
        grid_spec=pltpu.PrefetchScalarGridSpec(
            num_scalar_prefetch=5,
            grid=(N_TOK // TM,),
            in_specs=[
                row_spec,
                pl.BlockSpec((None, 6, D), lambda i, *_: (_cond_of_tile(i, TM), 0, 0)),
                pl.BlockSpec((TM, 2), lambda i, *_: (i, 0)),
                *extra_specs,
                pl.BlockSpec(memory_space=pl.ANY),
            ],
            out_specs=out_specs,
            scratch_shapes=scratch,
        ),
        out_shape=out_shape,
        compiler_params=_cparams(("arbitrary",)),
        name="moe_combine",
    )(*tabs, y, mod, w_cols, *extra_args, ys)


def _moe(y, g, mod, router, w_gu, w_down, slot, mixer_out=None, final_g=None):
    y, hs, ii, wf, cnt, tbase = _router(y, g, mod, router.T.astype(BF16), mixer_out)
    n_tt = N_TOK // TM
    counts = cnt[:, 0]
    before = tbase[:, 0].reshape(n_tt, N_EXP)
    n_seg = jnp.concatenate([before[1:], counts[None]], axis=0) - before
    region = (counts + CH + TM_E - 1) // TM_E * TM_E
    region_end = jnp.cumsum(region)
    offs = region_end - region
    seg_pad = (n_seg + CH - 1) // CH * CH
    seg = jnp.cumsum(seg_pad, axis=1) - seg_pad
    dst = offs[None, :] + before
    e_ids = jnp.arange(N_EXP, dtype=I32)[:, None]
    shift = jnp.repeat((seg - before).T, TM, axis=1)
    l1 = jnp.sum(jnp.where(ii[0:1] == e_ids, shift, 0), axis=0) + ii[2]
    l2 = jnp.sum(jnp.where(ii[1:2] == e_ids, shift, 0), axis=0) + ii[3]
    tabs = (l1, l2, seg.reshape(-1), dst.reshape(-1), (seg_pad // CH).reshape(-1))
    row0 = jnp.arange(MAX_TILES, dtype=I32) * TM_E
    tile_expert = jnp.minimum(jnp.sum((row0[:, None] >= region_end[None, :]).astype(I32), axis=1), N_EXP - 1)
    tile_valid = (row0 < (offs + counts)[tile_expert]).astype(I32)
    in_region = row0 < region_end[-1]
    j = (row0 - offs[tile_expert]) // TM_E
    n = (region // TM_E)[tile_expert]
    streams = in_region & (tile_expert < N_EXP - 1)
    tile_tabs = (tile_expert, tile_valid, (in_region & (j == 0)).astype(I32),
                 jnp.where(streams, j * N_FF_CHUNKS // n, 0), jnp.where(streams, (j + 1) * N_FF_CHUNKS // n, 0))
    zero_start = offs + counts // CH * CH
    zero_tabs = (jnp.concatenate([zero_start, region_end[-1:]]),
                 jnp.concatenate([region_end - zero_start, MAX_TILES * TM_E - region_end[-1:]]) // CH)
    xs = _dispatch(tabs, zero_tabs, hs)
    ys = _experts(tile_tabs, xs, w_gu, w_down, slot)
    return _combine(tabs, y, mod, wf[0:2].T, ys, final_g)


def kernel(x_prompt, x_sample, state_rglru, cache_k, cache_v, c, c_ctx, norm_g, ada_w, ada_b,
           gm_w_in, gm_ln_g, gm_w_s, gm_b_s, gm_w_out, rg_w_in, rg_conv_w, rg_conv_b, rg_w_gate,
           rg_b_gate, rg_lambda, rg_w_out, att_w_qkv, att_lambda, att_subln_g, att_w_out,
           ff_w_gu, ff_w_down, moe_router, moe_w_gu, moe_w_down, final_g):
    y = (x_prompt.reshape(N_P, D), x_sample.reshape(N_L, D))
    conds = jnp.concatenate([c_ctx[None], c, jnp.zeros((N_COND - 1 - B_L, D), F32)], axis=0)
    mods = _adaln(conds, ada_w, ada_b)
    nb = D_RNN // RNN_BS
    gm_w_in, gm_w_out, rg_w_in, rg_w_out, att_w_qkv, att_w_out, ff_w_gu, ff_w_down = (
        w.astype(BF16) for w in (gm_w_in, gm_w_out, rg_w_in, rg_w_out, att_w_qkv, att_w_out, ff_w_gu, ff_w_down))
    new_rnn = new_k = new_v = None
    for i in range(DEPTH):
        mod = mods[i]
        kind, slot = i % 3, i // 3
        mixer_out = None
        if kind == 0:
            y = _gmlp_mixer(y if i == 0 else (y,), norm_g[i, 0], mod, gm_w_in, gm_ln_g[slot],
                            gm_w_s[slot].astype(BF16), gm_b_s[slot].T, gm_w_out, slot)
        elif kind == 1:
            yx = _mod_matmul(y, norm_g[i, 0], mod, rg_w_in, slot, tn=256, gelu_cols=D_RNN)
            wg_cat = rg_w_gate[slot].transpose(2, 3, 0, 1, 4).reshape(nb, RNN_BS, 4 * RNN_BS).astype(BF16)
            bg_cat = rg_b_gate[slot].reshape(2, 2, nb, RNN_BS).transpose(2, 0, 1, 3).reshape(nb, 1, 4 * RNN_BS)
            rg_args = (yx, rg_conv_w[slot], rg_conv_b[slot], wg_cat, bg_cat, rg_lambda[slot])
            a_p, fin = _rglru(*rg_args, jnp.zeros((2, B_P, D_RNN), F32),
                              row0=0, n_batch=B_P, n_seq=SUBLANE, t_len=T_P)
            a_l, _ = _rglru(*rg_args, state_rglru[:, slot].transpose(1, 0, 2),
                            row0=N_P, n_batch=B_L, n_seq=B_L, t_len=T_L)
            new_rnn = fin.transpose(1, 0, 2)[:, None]
            mixer_out = ((a_p, a_l), rg_w_out, slot)
        else:
            lambda_init = 0.8 - 0.6 * math.exp(-0.3 * i)
            att = (y, norm_g[i, 0], mod, att_w_qkv, slot, att_lambda[slot], att_subln_g[slot], lambda_init)
            a_p, new_k, new_v = _attention(*att, row0=0, n_batch=B_P, t_len=T_P, heads=HEADS, tq=T_P)
            a_l = _attention(*att, row0=N_P, n_batch=B_L, t_len=T_L, heads=HEADS, tq=256,
                             ctx=(cache_k, cache_v, slot), rope_tabs=_rope_tables(T_L))
            new_k = new_k.reshape(B_P, 1, T_P, HEADS, 2 * HD)
            new_v = new_v.reshape(B_P, 1, T_P, HEADS, VD)
            mixer_out = ((a_p, a_l), att_w_out, slot)
        fslot = i // 2
        if i % 2 == 0:
            y = _ffn(y, norm_g[i, 1], mod, ff_w_gu, ff_w_down, fslot, mixer_out)
        else:
            y = _moe(y, norm_g[i, 1], mod, moe_router[fslot], moe_w_gu, moe_w_down, fslot, mixer_out,
                     final_g if i == DEPTH - 1 else None)
    y_prompt, y_sample = y
    return (y_prompt.reshape(B_P, T_P, D), y_sample.reshape(B_L, T_L, D), new_rnn, new_k, new_v)
```

```python
import functools
import math

import jax
import jax.numpy as jnp
import numpy as np
from jax import lax
from jax.experimental import pallas as pl
from jax.experimental.pallas import tpu as pltpu

F32 = jnp.float32
BF16 = jnp.bfloat16
I32 = jnp.int32

D = 1024
DEPTH = 4
B_P, T_P = 32, 256
B_L, T_L = 4, 2048
PAST = 256
N_P = B_P * T_P
N_L = B_L * T_L
N_TOK = N_P + N_L
N_COND = 8
GRID_W = 64
CHUNK = 128
GROUPS = 8
D_RNN = 1280
RNN_BS = 128
CONV_W = 4
RGLRU_C = 8.0
HD = 64
VD = 128
HEADS = 8
ROPE_THETA = 10000.0
LOG2_E = 1.4426950408889634
D_FF = 2816
N_EXP = 8

LANE = 128
SUBLANE = 8
VMEM_LIMIT = 56 * 1024 * 1024
TM = 1024
TM_E = 256
CH = 32
MAX_TILES = -(-(2 * N_TOK + N_EXP * (CH + TM_E - 1)) // TM_E)
STAGE_ROWS = 2 * TM + N_EXP * CH
SLAB = D // LANE


def _cparams(sem):
    return pltpu.CompilerParams(dimension_semantics=sem, vmem_limit_bytes=VMEM_LIMIT)


def _cond_of_tile(i, tm):
    r0 = i * tm
    return jnp.where(r0 < N_P, 0, 1 + (r0 - N_P) // T_L)


def _modnorm(x, g, shift, scale):
    ms = jnp.mean(x * x, axis=-1, keepdims=True)
    return (x * lax.rsqrt(ms + 1e-6)) * (g * (1.0 + scale)) + shift


def _gelu_tanh(x):
    c = math.sqrt(2.0 / math.pi)
    half_x = 0.5 * x
    return half_x + half_x * jnp.tanh(x * (c + (c * 0.044715) * (x * x)))


def _adaln_kernel(c_ref, w_ref, b_ref, o_ref):
    s = jax.nn.silu(c_ref[...]).astype(BF16)
    o_ref[...] = jnp.dot(s, w_ref[...].astype(BF16), preferred_element_type=F32) + b_ref[...]


def _adaln(conds, ada_w, ada_b):
    tn = 1536
    n6 = 6 * D
    out = pl.pallas_call(
        _adaln_kernel,
        grid=(DEPTH, n6 // tn),
        in_specs=[
            pl.BlockSpec((N_COND, D), lambda l, j: (0, 0)),
            pl.BlockSpec((None, D, tn), lambda l, j: (l, 0, j)),
            pl.BlockSpec((None, 1, tn), lambda l, j: (l, 0, j)),
        ],
        out_specs=pl.BlockSpec((None, N_COND, tn), lambda l, j: (l, 0, j)),
        out_shape=jax.ShapeDtypeStruct((DEPTH, N_COND, n6), F32),
        compiler_params=_cparams(("arbitrary", "arbitrary")),
        name="adaln",
    )(conds, ada_w, ada_b.reshape(DEPTH, 1, n6))
    return out.reshape(DEPTH, N_COND, 6, D)


def _mod_matmul_kernel(x_ref, g_ref, mod_ref, w_ref, o_ref, *, tn, gelu_cols):
    h = _modnorm(x_ref[...], g_ref[...], mod_ref[0:1, :], mod_ref[1:2, :]).astype(BF16)
    for c in range(o_ref.shape[1] // tn):
        acc = jnp.dot(h, w_ref[:, c * tn:(c + 1) * tn], preferred_element_type=F32)
        if c * tn < gelu_cols:
            acc = _gelu_tanh(acc)
        o_ref[:, c * tn:(c + 1) * tn] = acc


def _mod_matmul(y, g, mod, w_bf16, slot, *, tn, gelu_cols):
    nout = w_bf16.shape[2]
    return pl.pallas_call(
        functools.partial(_mod_matmul_kernel, tn=tn, gelu_cols=gelu_cols),
        grid=(N_TOK // TM,),
        in_specs=[
            pl.BlockSpec((TM, D), lambda i: (i, 0)),
            pl.BlockSpec((1, D), lambda i: (0, 0)),
            pl.BlockSpec((None, 6, D), lambda i: (_cond_of_tile(i, TM), 0, 0)),
            pl.BlockSpec((None, D, nout), lambda i: (slot, 0, 0)),
        ],
        out_specs=pl.BlockSpec((TM, nout), lambda i: (i, 0)),
        out_shape=jax.ShapeDtypeStruct((N_TOK, nout), F32),
        compiler_params=_cparams(("arbitrary",)),
        name="mod_matmul",
    )(y, g.reshape(1, D), mod, w_bf16)


N_PROMPT_TILES = N_P // TM


def _mixer_out_specs(a_parts, w_out_bf16, slot):
    k = a_parts[0].shape[1]
    return [
        pl.BlockSpec((TM, k), lambda i, *_: (jnp.minimum(i, N_PROMPT_TILES - 1), 0)),
        pl.BlockSpec((TM, k), lambda i, *_: (jnp.maximum(i - N_PROMPT_TILES, 0), 0)),
        pl.BlockSpec((None, k, D), lambda i, *_: (slot, 0, 0)),
    ]


def _for_stream_of_tile(body):
    pl.when(pl.program_id(0) < N_PROMPT_TILES)(lambda: body(0))
    pl.when(pl.program_id(0) >= N_PROMPT_TILES)(lambda: body(1))


def _gmlp_kernel(*refs):
    *y_refs, g_ref, mod_ref, win_ref, lng_ref, ws_ref, bs_ref, wout_ref, o_ref, uv_ref, a_ref = refs
    body = functools.partial(_gmlp_tile, g_ref, mod_ref, win_ref, lng_ref, ws_ref, bs_ref, wout_ref,
                             o_ref, uv_ref, a_ref)
    if len(y_refs) == 1:
        body(y_refs[0])
    else:
        _for_stream_of_tile(lambda part: body(y_refs[part]))


def _gmlp_tile(g_ref, mod_ref, win_ref, lng_ref, ws_ref, bs_ref, wout_ref, o_ref, uv_ref, a_ref, y_ref):
    y = y_ref[...]
    h = _modnorm(y, g_ref[...], mod_ref[0:1, :], mod_ref[1:2, :]).astype(BF16)
    tn = 512
    for c in range(2 * D // tn):
        uv = jnp.dot(h, win_ref[:, c * tn:(c + 1) * tn], preferred_element_type=F32)
        uv_ref[:, c * tn:(c + 1) * tn] = _gelu_tanh(uv)
    v = uv_ref[:, D:]
    vc = v - jnp.mean(v, axis=-1, keepdims=True)
    vn = vc * lax.rsqrt(jnp.mean(vc * vc, axis=-1, keepdims=True) + 1e-5) * lng_ref[...]
    vn = vn.astype(BF16)
    for ci in range(TM // CHUNK):
        r0 = ci * CHUNK
        for g in range(GROUPS):
            c0 = g * LANE
            s = jnp.dot(ws_ref[g], vn[r0:r0 + CHUNK, c0:c0 + LANE], preferred_element_type=F32)
            s = s + bs_ref[:, g:g + 1]
            a_ref[r0:r0 + CHUNK, c0:c0 + LANE] = (uv_ref[r0:r0 + CHUNK, c0:c0 + LANE] * s).astype(BF16)
    o_ref[...] = y + mod_ref[2:3, :] * jnp.dot(a_ref[...], wout_ref[...], preferred_element_type=F32)


def _gmlp_mixer(y_parts, g, mod, w_in_bf16, ln_g, w_s_bf16, b_s_t, w_out_bf16, slot):
    if len(y_parts) == 1:
        y_specs = [pl.BlockSpec((TM, D), lambda i: (i, 0))]
    else:
        y_specs = [pl.BlockSpec((TM, D), lambda i: (jnp.minimum(i, N_PROMPT_TILES - 1), 0)),
                   pl.BlockSpec((TM, D), lambda i: (jnp.maximum(i - N_PROMPT_TILES, 0), 0))]
    return pl.pallas_call(
        _gmlp_kernel,
        grid=(N_TOK // TM,),
        in_specs=y_specs + [
            pl.BlockSpec((1, D), lambda i: (0, 0)),
            pl.BlockSpec((None, 6, D), lambda i: (_cond_of_tile(i, TM), 0, 0)),
            pl.BlockSpec((None, D, 2 * D), lambda i: (slot, 0, 0)),
            pl.BlockSpec((1, D), lambda i: (0, 0)),
            pl.BlockSpec((GROUPS, CHUNK, CHUNK), lambda i: (0, 0, 0)),
            pl.BlockSpec((CHUNK, GROUPS), lambda i: (0, 0)),
            pl.BlockSpec((None, D, D), lambda i: (slot, 0, 0)),
        ],
        out_specs=pl.BlockSpec((TM, D), lambda i: (i, 0)),
        out_shape=jax.ShapeDtypeStruct((N_TOK, D), F32),
        scratch_shapes=[pltpu.VMEM((TM, 2 * D), F32), pltpu.VMEM((TM, D), BF16)],
        compiler_params=_cparams(("arbitrary",)),
        name="gmlp_mixer",
    )(*y_parts, g.reshape(1, D), mod, w_in_bf16, ln_g.reshape(1, D), w_s_bf16, b_s_t, w_out_bf16)


SCAN_UNROLL = 8


def _rglru_kernel(yb_ref, xb_ref, cw_ref, cb_ref, wg_ref, bg_ref, lam_ref, h0_ref,
                  o_ref, fin_ref, af_ref, bf_ref, ab_ref, bb_ref, *, n_seq, t_len, row_chunk):
    stride = t_len + SUBLANE
    tix = lax.broadcasted_iota(I32, (t_len, 1), 0)
    sp_f = jax.nn.softplus(-lam_ref[0:1, :])
    sp_b = jax.nn.softplus(-lam_ref[1:2, :])
    for g in range(n_seq):
        x = xb_ref[g * t_len:(g + 1) * t_len, :]
        xm1 = jnp.where(tix >= 1, pltpu.roll(x, 1, 0), 0.0)
        xp1 = jnp.where(tix <= t_len - 2, pltpu.roll(x, t_len - 1, 0), 0.0)
        xp2 = jnp.where(tix <= t_len - 3, pltpu.roll(x, t_len - 2, 0), 0.0)
        xc = cb_ref[...] + xm1 * cw_ref[0:1, :]
        xc = xc + x * cw_ref[1:2, :]
        xc = xc + xp1 * cw_ref[2:3, :]
        xc = xc + xp2 * cw_ref[3:4, :]
        for r0 in range(0, t_len, row_chunk):
            xcc = xc[r0:r0 + row_chunk]
            gates = jnp.dot(xcc.astype(BF16), wg_ref[...], preferred_element_type=F32) + bg_ref[...]
            d0 = g * stride + r0
            for d, (sp, a_ref, b_ref) in enumerate(((sp_f, af_ref, bf_ref), (sp_b, ab_ref, bb_ref))):
                r = jax.nn.sigmoid(gates[:, (2 * d) * LANE:(2 * d + 1) * LANE])
                i = jax.nn.sigmoid(gates[:, (2 * d + 1) * LANE:(2 * d + 2) * LANE])
                neg_log_a = r * (RGLRU_C * sp)
                a = jnp.exp2(r * ((-RGLRU_C * LOG2_E) * sp))
                one_minus_a2 = jnp.tanh(neg_log_a) * (a * a + 1.0)
                a_ref[d0:d0 + row_chunk, :] = a
                b_ref[d0:d0 + row_chunk, :] = jnp.sqrt(one_minus_a2) * (i * xcc)

    def block_scan(a_ref, b_ref, rows, h_in):
        a = [a_ref[r, :] for r in rows]
        b = [b_ref[r, :] for r in rows]
        d = 1
        while d < SCAN_UNROLL:
            a, b = ([a[u] if u < d else a[u] * a[u - d] for u in range(SCAN_UNROLL)],
                    [b[u] if u < d else a[u] * b[u - d] + b[u] for u in range(SCAN_UNROLL)])
            d *= 2
        h = [a[u] * h_in + b[u] for u in range(SCAN_UNROLL)]
        for r, hu in zip(rows, h):
            b_ref[r, :] = hu
        return h[-1]

    def steps(s8, carry):
        hf, hb = carry
        s0 = s8 * SCAN_UNROLL
        hf = block_scan(af_ref, bf_ref, [pl.ds(s0 + u, n_seq, stride=stride) for u in range(SCAN_UNROLL)], hf)
        hb = block_scan(ab_ref, bb_ref,
                        [pl.ds(t_len - 1 - s0 - u, n_seq, stride=stride) for u in range(SCAN_UNROLL)], hb)
        return hf, hb

    hf, hb = lax.fori_loop(0, t_len // SCAN_UNROLL, steps, (h0_ref[0], h0_ref[1]))
    fin_ref[0] = hf
    fin_ref[1] = hb
    for g in range(n_seq):
        rs = slice(g * stride, g * stride + t_len)
        hsum = bf_ref[rs, :] + bb_ref[rs, :]
        o_ref[g * t_len:(g + 1) * t_len, :] = (hsum * yb_ref[g * t_len:(g + 1) * t_len, :]).astype(BF16)


def _rglru(yx, cw, cb, wg_cat, bg_cat, lam, h0, *, row0, n_batch, n_seq, t_len):
    rows = n_seq * t_len
    nb = D_RNN // RNN_BS
    rb0 = row0 // rows
    kern = functools.partial(_rglru_kernel, n_seq=n_seq, t_len=t_len, row_chunk=min(t_len, 512))
    scan_buf = pltpu.VMEM((n_seq * (t_len + SUBLANE), RNN_BS), F32)
    return pl.pallas_call(
        kern,
        grid=(n_batch // n_seq, nb),
        in_specs=[
            pl.BlockSpec((rows, RNN_BS), lambda s, c: (rb0 + s, c)),
            pl.BlockSpec((rows, RNN_BS), lambda s, c: (rb0 + s, nb + c)),
            pl.BlockSpec((CONV_W, RNN_BS), lambda s, c: (0, c)),
            pl.BlockSpec((1, RNN_BS), lambda s, c: (0, c)),
            pl.BlockSpec((None, RNN_BS, 4 * RNN_BS), lambda s, c: (c, 0, 0)),
            pl.BlockSpec((None, 1, 4 * RNN_BS), lambda s, c: (c, 0, 0)),
            pl.BlockSpec((2, RNN_BS), lambda s, c: (0, c)),
            pl.BlockSpec((2, n_seq, RNN_BS), lambda s, c: (0, s, c)),
        ],
        out_specs=[
            pl.BlockSpec((rows, RNN_BS), lambda s, c: (s, c)),
            pl.BlockSpec((2, n_seq, RNN_BS), lambda s, c: (0, s, c)),
        ],
        out_shape=[
            jax.ShapeDtypeStruct((n_batch * t_len, D_RNN), BF16),
            jax.ShapeDtypeStruct((2, n_batch, D_RNN), F32),
        ],
        scratch_shapes=[scan_buf, scan_buf, scan_buf, scan_buf],
        compiler_params=_cparams(("arbitrary", "arbitrary")),
        name="rglru",
    )(yx, yx, cw, cb.reshape(1, D_RNN), wg_cat, bg_cat, lam, h0)


def _rope(x, cos, sin_signed, first_of_pair):
    partner = jnp.where(first_of_pair, pltpu.roll(x, LANE - 16, 1), pltpu.roll(x, 16, 1))
    return x * cos + partner * sin_signed


def _attn_kernel(*refs, heads, t_len, tq, s_len, rope, lambda_init, mxu_denominator):
    if rope:
        (lam_ref, g_ref, y_ref, ng_ref, mod_ref, wq_ref, wk_ref, wv_ref, ck_ref, cv_ref, cq_ref, sq_ref,
         ckk_ref, skk_ref, o_ref, h_ref, kk_ref, vv_ref) = refs
    else:
        (lam_ref, g_ref, y_ref, ng_ref, mod_ref, wq_ref, wk_ref, wv_ref,
         o_ref, ko_ref, vo_ref, h_ref, kk_ref, vv_ref) = refs
    lane = lax.broadcasted_iota(I32, (1, LANE), 1)
    first_of_pair = (lane & 16) == 0
    map0 = lane < HD

    @pl.when((pl.program_id(1) == 0) & (pl.program_id(2) == 0))
    def _():
        h_ref[...] = _modnorm(y_ref[...], ng_ref[...], mod_ref[0:1, :], mod_ref[1:2, :]).astype(BF16)

    @pl.when(pl.program_id(2) == 0)
    def _():
        k_all = jnp.dot(h_ref[...], wk_ref[...], preferred_element_type=F32)
        v_all = jnp.dot(h_ref[...], wv_ref[...], preferred_element_type=F32)
        if not rope:
            ko_ref[...] = k_all
            vo_ref[...] = v_all
        for j in range(heads):
            cs = slice(j * LANE, (j + 1) * LANE)
            k = k_all[:, cs]
            if rope:
                k = _rope(k, ckk_ref[...], skk_ref[...], first_of_pair)
            kk_ref[j, 0:t_len, :] = k.astype(BF16)
            vv_ref[j, 0:t_len, 0:LANE] = v_all[:, cs].astype(BF16)
            if s_len > t_len:
                ctx_rows = pl.ds(pl.program_id(1) * heads + j, s_len - t_len, stride=HEADS)
                kk_ref[j, t_len:s_len, :] = ck_ref[ctx_rows, :].astype(BF16)
                vv_ref[j, t_len:s_len, 0:LANE] = cv_ref[ctx_rows, :].astype(BF16)
            if mxu_denominator:
                vv_ref[j, :, LANE:2 * LANE] = jnp.ones((s_len, LANE), BF16)

    lp = lam_ref[...]
    lam = (jnp.exp(jnp.sum(lp[0:1] * lp[1:2], axis=-1, keepdims=True))
           - jnp.exp(jnp.sum(lp[2:3] * lp[3:4], axis=-1, keepdims=True)) + lambda_init)
    q_rows = pl.ds(pl.multiple_of(pl.program_id(2) * tq, tq), tq)
    q_all = jnp.dot(h_ref[q_rows, :], wq_ref[...], preferred_element_type=F32)
    for j in range(heads):
        cs = slice(j * LANE, (j + 1) * LANE)
        q = q_all[:, cs]
        if rope:
            q = _rope(q, cq_ref[...], sq_ref[...], first_of_pair)
        q = q * (HD ** -0.5 * LOG2_E)
        q2 = jnp.concatenate([jnp.where(map0, q, 0.0), jnp.where(map0, 0.0, q)], axis=0).astype(BF16)
        s = lax.dot_general(q2, kk_ref[j], (((1,), (1,)), ((), ())), preferred_element_type=F32)
        e = jnp.exp2(s - jnp.max(s, axis=-1, keepdims=True))
        if mxu_denominator:
            nd = jnp.dot(e.astype(BF16), vv_ref[j], preferred_element_type=F32)
            av = nd[:, 0:LANE] * (1.0 / nd[:, LANE:LANE + 1])
            o = av[0:tq] - lam * av[tq:2 * tq]
        else:
            p = e * (1.0 / jnp.sum(e, axis=-1, keepdims=True))
            w = p[0:tq] - lam * p[tq:2 * tq]
            o = jnp.dot(w.astype(BF16), vv_ref[j], preferred_element_type=F32)
        o = o * lax.rsqrt(jnp.mean(o * o, axis=-1, keepdims=True) + 1e-6) * g_ref[...]
        o_ref[:, cs] = (o * (1.0 - lambda_init)).astype(BF16)


def _attention(y, norm_g, mod, w_qkv_bf16, wslot, lam_p, subln_g, lambda_init, *, row0, n_batch, t_len, heads, tq,
               ctx=None, rope_tabs=None):
    rope = rope_tabs is not None
    s_len = t_len + (PAST if rope else 0)
    w = heads * LANE
    nhb = HEADS // heads
    nq = t_len // tq
    rk0 = row0 // t_len
    cond0 = 1 if rope else 0
    mxu_denominator = s_len >= 1024
    kern = functools.partial(_attn_kernel, heads=heads, t_len=t_len, tq=tq, s_len=s_len, rope=rope,
                             lambda_init=lambda_init, mxu_denominator=mxu_denominator)
    v_width = 2 * LANE if mxu_denominator else LANE
    in_specs = [
        pl.BlockSpec((4, HD), lambda b, h, q: (0, 0)),
        pl.BlockSpec((1, VD), lambda b, h, q: (0, 0)),
        pl.BlockSpec((t_len, D), lambda b, h, q: (rk0 + b, 0), pipeline_mode=pl.Buffered(1 if rope else 2)),
        pl.BlockSpec((1, D), lambda b, h, q: (0, 0)),
        pl.BlockSpec((None, 6, D), lambda b, h, q: (b * cond0 + cond0, 0, 0)),
        pl.BlockSpec((None, D, w), lambda b, h, q: (wslot, 0, h)),
        pl.BlockSpec((None, D, w), lambda b, h, q: (wslot, 0, nhb + h)),
        pl.BlockSpec((None, D, w), lambda b, h, q: (wslot, 0, 2 * nhb + h)),
    ]
    args = [lam_p, subln_g.reshape(1, VD), y, norm_g.reshape(1, D), mod, w_qkv_bf16, w_qkv_bf16, w_qkv_bf16]
    row_block = pl.BlockSpec((tq, w), lambda b, h, q: (b * nq + q, h))
    out_specs, out_shape = row_block, jax.ShapeDtypeStruct((n_batch * t_len, D), BF16)
    if not rope:
        seq_block = pl.BlockSpec((t_len, w), lambda b, h, q: (b, h))
        out_specs = [row_block, seq_block, seq_block]
        out_shape = [out_shape] + [jax.ShapeDtypeStruct((n_batch * t_len, D), F32)] * 2
    if rope:
        cos_t, sin_t = rope_tabs
        cache_k, cache_v, slot = ctx
        n_slots = cache_k.shape[1]
        ctx_k = cache_k.reshape(n_batch, n_slots, PAST * HEADS, LANE)
        ctx_v = cache_v.reshape(n_batch, n_slots, PAST * HEADS, LANE)
        in_specs += [
            pl.BlockSpec((None, None, PAST * HEADS, LANE), lambda b, h, q: (b, slot, 0, 0)),
            pl.BlockSpec((None, None, PAST * HEADS, LANE), lambda b, h, q: (b, slot, 0, 0)),
            pl.BlockSpec((tq, LANE), lambda b, h, q: (q, 0)),
            pl.BlockSpec((tq, LANE), lambda b, h, q: (q, 0)),
            pl.BlockSpec((t_len, LANE), lambda b, h, q: (0, 0)),
            pl.BlockSpec((t_len, LANE), lambda b, h, q: (0, 0)),
        ]
        args += [ctx_k, ctx_v, cos_t, sin_t, cos_t, sin_t]
    return pl.pallas_call(
        kern,
        grid=(n_batch, nhb, nq),
        in_specs=in_specs,
        out_specs=out_specs,
        out_shape=out_shape,
        scratch_shapes=[pltpu.VMEM((t_len, D), BF16), pltpu.VMEM((heads, s_len, LANE), BF16),
                        pltpu.VMEM((heads, s_len, v_width), BF16)],
        compiler_params=_cparams(("arbitrary", "arbitrary", "arbitrary")),
        name="diff_attn",
    )(*args)


def _rope_tables(t_len):
    t = np.arange(t_len)
    half = HD // 2
    freqs = (ROPE_THETA ** (-np.arange(0, half, 2, dtype=np.float32) / half)).astype(np.float32)
    ang_r = (t // GRID_W).astype(np.float32)[:, None] * freqs[None]
    ang_c = (t % GRID_W).astype(np.float32)[:, None] * freqs[None]
    cr, sr, cc, sc = np.cos(ang_r), np.sin(ang_r), np.cos(ang_c), np.sin(ang_c)
    cos64 = np.concatenate([cr, cr, cc, cc], axis=-1)
    sin64 = np.concatenate([-sr, sr, -sc, sc], axis=-1)
    return (jnp.asarray(np.concatenate([cos64, cos64], axis=-1), F32),
            jnp.asarray(np.concatenate([sin64, sin64], axis=-1), F32))


def _swiglu_rows(h_bf16, wgu_ref, wd_ref, h1_ref):
    tn = 256
    for c in range(D_FF // tn):
        gg = jnp.dot(h_bf16, wgu_ref[:, c * tn:(c + 1) * tn], preferred_element_type=F32)
        uu = jnp.dot(h_bf16, wgu_ref[:, D_FF + c * tn:D_FF + (c + 1) * tn], preferred_element_type=F32)
        h1_ref[:, c * tn:(c + 1) * tn] = (jax.nn.silu(gg) * uu).astype(BF16)
    return jnp.dot(h1_ref[...], wd_ref[...], preferred_element_type=F32)


def _ffn_kernel(*refs, mixer_out):
    if mixer_out:
        ap_ref, al_ref, wout_ref, y_ref, g_ref, mod_ref, wgu_ref, wd_ref, o_ref, h1_ref = refs
    else:
        y_ref, g_ref, mod_ref, wgu_ref, wd_ref, o_ref, h1_ref = refs

    def body(a_ref):
        y = y_ref[...]
        if a_ref is not None:
            y = y + mod_ref[2:3, :] * jnp.dot(a_ref[...], wout_ref[...], preferred_element_type=F32)
        h = _modnorm(y, g_ref[...], mod_ref[3:4, :], mod_ref[4:5, :]).astype(BF16)
        o_ref[...] = y + mod_ref[5:6, :] * _swiglu_rows(h, wgu_ref, wd_ref, h1_ref)

    if mixer_out:
        _for_stream_of_tile(lambda part: body((ap_ref, al_ref)[part]))
    else:
        body(None)


def _ffn(y, g, mod, wgu_bf16, wd_bf16, slot, mixer_out=None):
    pre_specs, pre_args = [], []
    if mixer_out is not None:
        pre_specs = _mixer_out_specs(*mixer_out)
        pre_args = [*mixer_out[0], mixer_out[1]]
    return pl.pallas_call(
        functools.partial(_ffn_kernel, mixer_out=mixer_out is not None),
        grid=(N_TOK // TM,),
        in_specs=pre_specs + [
            pl.BlockSpec((TM, D), lambda i: (i, 0)),
            pl.BlockSpec((1, D), lambda i: (0, 0)),
            pl.BlockSpec((None, 6, D), lambda i: (_cond_of_tile(i, TM), 0, 0)),
            pl.BlockSpec((None, D, 2 * D_FF), lambda i: (slot, 0, 0), pipeline_mode=pl.Buffered(1)),
            pl.BlockSpec((None, D_FF, D), lambda i: (slot, 0, 0), pipeline_mode=pl.Buffered(1)),
        ],
        out_specs=pl.BlockSpec((TM, D), lambda i: (i, 0)),
        out_shape=jax.ShapeDtypeStruct((N_TOK, D), F32),
        scratch_shapes=[pltpu.VMEM((TM, D_FF), BF16)],
        compiler_params=_cparams(("arbitrary",)),
        name="ffn",
    )(*pre_args, y, g.reshape(1, D), mod, wgu_bf16, wd_bf16)


def _router_kernel(*refs, mixer_out):
    if mixer_out:
        (ap_ref, al_ref, wout_ref, y_ref, g_ref, mod_ref, rt_ref,
         ym_ref, hs_ref, ii_ref, wf_ref, cnt_ref, tb_ref, carry_ref, tri_ref) = refs

        def add_mixer_out(part):
            acc = jnp.dot((ap_ref, al_ref)[part][...], wout_ref[...], preferred_element_type=F32)
            ym_ref[...] = y_ref[...] + mod_ref[2:3, :] * acc

        _for_stream_of_tile(add_mixer_out)
        y = ym_ref[...]
    else:
        y_ref, g_ref, mod_ref, rt_ref, hs_ref, ii_ref, wf_ref, cnt_ref, tb_ref, carry_ref, tri_ref = refs
        y = y_ref[...]
    tm = y_ref.shape[0]

    @pl.when(pl.program_id(0) == 0)
    def _():
        carry_ref[...] = jnp.zeros_like(carry_ref)
        before = (lax.broadcasted_iota(I32, (tm, tm), 0) < lax.broadcasted_iota(I32, (tm, tm), 1))
        tri_ref[...] = jnp.where(before, 1.0, 0.0).astype(BF16)

    tb_ref[...] = carry_ref[...].astype(I32)

    h = _modnorm(y, g_ref[...], mod_ref[3:4, :], mod_ref[4:5, :])
    for k in range(SLAB):
        hs_ref[:, k, :] = h[:, k * LANE:(k + 1) * LANE]
    logits = lax.dot_general(rt_ref[...], h.astype(BF16), (((1,), (1,)), ((), ())),
                             preferred_element_type=F32)
    eidx = lax.broadcasted_iota(I32, (N_EXP, tm), 0)
    eidx_f = eidx.astype(F32)
    m1 = jnp.max(logits, axis=0, keepdims=True)
    i1 = jnp.min(jnp.where(logits == m1, eidx_f, float(N_EXP)), axis=0, keepdims=True)
    oh1 = eidx_f == i1
    rest = jnp.where(oh1, -jnp.inf, logits)
    m2 = jnp.max(rest, axis=0, keepdims=True)
    i2 = jnp.min(jnp.where(rest == m2, eidx_f, float(N_EXP)), axis=0, keepdims=True)
    oh2 = eidx_f == i2
    i1 = i1.astype(I32)
    i2 = i2.astype(I32)
    e2 = jnp.exp(m2 - m1)
    w1 = 1.0 / (1.0 + e2)
    w2 = e2 / (1.0 + e2)
    sel = jnp.where(oh1 | oh2, 1.0, 0.0)
    cum = jnp.dot(sel.astype(BF16), tri_ref[...], preferred_element_type=F32)
    cum = cum + carry_ref[:, 0:1]
    r1 = jnp.sum(jnp.where(oh1, cum, 0.0), axis=0, keepdims=True).astype(I32)
    r2 = jnp.sum(jnp.where(oh2, cum, 0.0), axis=0, keepdims=True).astype(I32)
    ii_ref[...] = jnp.where(eidx == 0, i1, jnp.where(eidx == 1, i2, jnp.where(eidx == 2, r1,
                            jnp.where(eidx == 3, r2, 0))))
    wf_ref[...] = jnp.where(eidx == 0, w1, jnp.where(eidx == 1, w2, 0.0))
    carry_ref[...] = carry_ref[...] + jnp.sum(sel, axis=1, keepdims=True)
    cnt_ref[...] = carry_ref[...].astype(I32)


def _router(y, g, mod, router_t_bf16, mixer_out=None):
    pre_specs, pre_args, pre_out_specs, pre_out_shape = [], [], [], []
    if mixer_out is not None:
        pre_specs = _mixer_out_specs(*mixer_out)
        pre_args = [*mixer_out[0], mixer_out[1]]
        pre_out_specs = [pl.BlockSpec((TM, D), lambda i: (i, 0))]
        pre_out_shape = [jax.ShapeDtypeStruct((N_TOK, D), F32)]
    outs = pl.pallas_call(
        functools.partial(_router_kernel, mixer_out=mixer_out is not None),
        grid=(N_TOK // TM,),
        in_specs=pre_specs + [
            pl.BlockSpec((TM, D), lambda i: (i, 0)),
            pl.BlockSpec((1, D), lambda i: (0, 0)),
            pl.BlockSpec((None, 6, D), lambda i: (_cond_of_tile(i, TM), 0, 0)),
            pl.BlockSpec((N_EXP, D), lambda i: (0, 0)),
        ],
        out_specs=pre_out_specs + [
            pl.BlockSpec((TM, SLAB, LANE), lambda i: (i, 0, 0)),
            pl.BlockSpec((N_EXP, TM), lambda i: (0, i)),
            pl.BlockSpec((N_EXP, TM), lambda i: (0, i)),
            pl.BlockSpec((N_EXP, LANE), lambda i: (0, 0)),
            pl.BlockSpec((N_EXP, LANE), lambda i: (i, 0)),
        ],
        out_shape=pre_out_shape + [
            jax.ShapeDtypeStruct((N_TOK, SLAB, LANE), F32),
            jax.ShapeDtypeStruct((N_EXP, N_TOK), I32),
            jax.ShapeDtypeStruct((N_EXP, N_TOK), F32),
            jax.ShapeDtypeStruct((N_EXP, LANE), I32),
            jax.ShapeDtypeStruct((N_TOK // TM * N_EXP, LANE), I32),
        ],
        scratch_shapes=[pltpu.VMEM((N_EXP, LANE), F32), pltpu.VMEM((TM, TM), BF16)],
        compiler_params=_cparams(("arbitrary",)),
        name="router",
    )(*pre_args, y, g.reshape(1, D), mod, router_t_bf16)
    return tuple(outs) if mixer_out is not None else (y, *outs)


class _SegmentCopies:
    def __init__(self, seg_ref, dst_ref, nch_ref, stage_ref, hbm_ref, sem, *, to_hbm):
        self.tabs = (seg_ref, dst_ref, nch_ref)
        self.stage_ref, self.hbm_ref, self.sem, self.to_hbm = stage_ref, hbm_ref, sem, to_hbm

    def _chunk(self, slot, s_row, d_row):
        st = self.stage_ref.at[slot, pl.ds(s_row, CH)]
        hb = self.hbm_ref.at[pl.ds(d_row, CH)]
        sem = self.sem.at[slot]
        return pltpu.make_async_copy(st, hb, sem) if self.to_hbm else pltpu.make_async_copy(hb, st, sem)

    def start(self, tile):
        seg_ref, dst_ref, nch_ref = self.tabs
        slot = tile % 2
        for e in range(N_EXP):
            k = tile * N_EXP + e
            s0, d0 = seg_ref[k], dst_ref[k]

            def start_chunk(c, carry, s0=s0, d0=d0):
                self._chunk(slot, s0 + c * CH, d0 + c * CH).start()
                return carry

            lax.fori_loop(0, nch_ref[k], start_chunk, 0)

    def wait(self, tile):
        nch_ref = self.tabs[2]
        slot = tile % 2
        total = 0
        for e in range(N_EXP):
            total = total + nch_ref[tile * N_EXP + e]

        def wait_chunk(c, carry):
            self._chunk(slot, 0, 0).wait()
            return carry

        lax.fori_loop(0, total, wait_chunk, 0)


def _dispatch_kernel(l1_ref, l2_ref, seg_ref, dst_ref, nch_ref, zs_ref, zn_ref, hs_ref, xs_hbm, stage_ref, sem):
    i = pl.program_id(0)
    n_tiles = pl.num_programs(0)
    copies = _SegmentCopies(seg_ref, dst_ref, nch_ref, stage_ref, xs_hbm, sem, to_hbm=True)

    @pl.when(i == 0)
    def _():
        stage_ref[...] = jnp.zeros_like(stage_ref)

        def zero_chunk(row):
            return pltpu.make_async_copy(stage_ref.at[0, pl.ds(0, CH)], xs_hbm.at[pl.ds(row, CH)], sem.at[0])

        total = 0
        for z in range(N_EXP + 1):
            z0, n = zs_ref[z], zn_ref[z]

            def start(c, carry, z0=z0):
                zero_chunk(z0 + c * CH).start()
                return carry

            lax.fori_loop(0, n, start, 0)
            total = total + n

        def wait(c, carry):
            zero_chunk(0).wait()
            return carry

        lax.fori_loop(0, total, wait, 0)

    slot = i % 2
    base = i * TM

    def place(t8, carry):
        for u in range(SUBLANE):
            t = t8 * SUBLANE + u
            row = hs_ref[t]
            stage_ref[slot, l1_ref[base + t]] = row
            stage_ref[slot, l2_ref[base + t]] = row
        return carry

    lax.fori_loop(0, TM // SUBLANE, place, 0)
    pl.when(i >= 1)(lambda: copies.wait(i - 1))
    copies.start(i)
    pl.when(i == n_tiles - 1)(lambda: copies.wait(i))


def _dispatch(tabs, zero_tabs, hs):
    return pl.pallas_call(
        _dispatch_kernel,
        grid_spec=pltpu.PrefetchScalarGridSpec(
            num_scalar_prefetch=7,
            grid=(N_TOK // TM,),
            in_specs=[pl.BlockSpec((TM, SLAB, LANE), lambda i, *_: (i, 0, 0))],
            out_specs=pl.BlockSpec(memory_space=pl.ANY),
            scratch_shapes=[pltpu.VMEM((2, STAGE_ROWS, SLAB, LANE), F32), pltpu.SemaphoreType.DMA((2,))],
        ),
        out_shape=jax.ShapeDtypeStruct((MAX_TILES * TM_E, SLAB, LANE), F32),
        compiler_params=_cparams(("arbitrary",)),
        name="moe_dispatch",
    )(*tabs, *zero_tabs, hs)


FF_CHUNK = 256
N_FF_CHUNKS = D_FF // FF_CHUNK


def _expert_kernel(te_ref, tv_ref, first_ref, clo_ref, chi_ref, x_ref, wgu_hbm, wd_hbm, o_ref,
                   wg_ref, wu_ref, wd_ref, sg_ref, su_ref, sd_ref, xb_ref, h1_ref, sem, *, layer):
    i = pl.program_id(0)
    e = te_ref[i]
    cur = e % 2

    def copies(expert, c):
        s = c % 2
        cols = slice(c * FF_CHUNK, (c + 1) * FF_CHUNK)
        ucols = slice(D_FF + c * FF_CHUNK, D_FF + (c + 1) * FF_CHUNK)
        return (pltpu.make_async_copy(wgu_hbm.at[layer, expert, :, cols], sg_ref.at[s], sem.at[s]),
                pltpu.make_async_copy(wgu_hbm.at[layer, expert, :, ucols], su_ref.at[s], sem.at[s]),
                pltpu.make_async_copy(wd_hbm.at[layer, expert, cols, :], sd_ref.at[s], sem.at[s]))

    def fetch(expert, c):
        for cp in copies(expert, c):
            cp.start()

    def convert(expert, c, wslot):
        for cp in copies(expert, c):
            cp.wait()
        s = c % 2
        wg_ref[wslot, c] = sg_ref[s].astype(BF16)
        wu_ref[wslot, c] = su_ref[s].astype(BF16)
        wd_ref[wslot, c * FF_CHUNK:(c + 1) * FF_CHUNK, :] = sd_ref[s].astype(BF16)
        if c + 2 < N_FF_CHUNKS:
            fetch(expert, c + 2)

    @pl.when(i == 0)
    def _():
        fetch(e, 0)
        fetch(e, 1)
        for c in range(N_FF_CHUNKS):
            convert(e, c, cur)

    nxt = jnp.minimum(e + 1, N_EXP - 1)

    @pl.when((first_ref[i] == 1) & (e + 1 < N_EXP))
    def _():
        fetch(nxt, 0)
        fetch(nxt, 1)

    valid = tv_ref[i] == 1

    @pl.when(valid)
    def _():
        for k in range(SLAB):
            xb_ref[:, k * LANE:(k + 1) * LANE] = x_ref[pl.ds(k, TM_E, stride=SLAB), :].astype(BF16)
        xb = xb_ref[...]
        for c in range(N_FF_CHUNKS):
            gg = jnp.dot(xb, wg_ref[cur, c], preferred_element_type=F32)
            uu = jnp.dot(xb, wu_ref[cur, c], preferred_element_type=F32)
            h1_ref[:, c * FF_CHUNK:(c + 1) * FF_CHUNK] = (jax.nn.silu(gg) * uu).astype(BF16)
        acc = jnp.dot(h1_ref[...], wd_ref[cur], preferred_element_type=F32)
        for k in range(SLAB):
            o_ref[:, k, :] = acc[:, k * LANE:(k + 1) * LANE]

    @pl.when(jnp.logical_not(valid))
    def _():
        o_ref[...] = jnp.zeros_like(o_ref)

    for c in range(N_FF_CHUNKS):
        pl.when((clo_ref[i] <= c) & (c < chi_ref[i]))(functools.partial(convert, nxt, c, 1 - cur))


def _experts(tile_tabs, xs, w_gu, w_down, layer):
    return pl.pallas_call(
        functools.partial(_expert_kernel, layer=layer),
        grid_spec=pltpu.PrefetchScalarGridSpec(
            num_scalar_prefetch=5,
            grid=(MAX_TILES,),
            in_specs=[
                pl.BlockSpec((TM_E * SLAB, LANE), lambda i, *_: (i, 0)),
                pl.BlockSpec(memory_space=pl.ANY),
                pl.BlockSpec(memory_space=pl.ANY),
            ],
            out_specs=pl.BlockSpec((TM_E, SLAB, LANE), lambda i, *_: (i, 0, 0)),
            scratch_shapes=[
                pltpu.VMEM((2, N_FF_CHUNKS, D, FF_CHUNK), BF16),
                pltpu.VMEM((2, N_FF_CHUNKS, D, FF_CHUNK), BF16),
                pltpu.VMEM((2, D_FF, D), BF16),
                pltpu.VMEM((2, D, FF_CHUNK), F32),
                pltpu.VMEM((2, D, FF_CHUNK), F32),
                pltpu.VMEM((2, FF_CHUNK, D), F32),
                pltpu.VMEM((TM_E, D), BF16),
                pltpu.VMEM((TM_E, D_FF), BF16),
                pltpu.SemaphoreType.DMA((2,)),
            ],
        ),
        out_shape=jax.ShapeDtypeStruct(xs.shape, F32),
        compiler_params=_cparams(("arbitrary",)),
        name="moe_experts",
    )(*tile_tabs, xs.reshape(-1, LANE), w_gu, w_down)


def _combine_kernel(l1_ref, l2_ref, seg_ref, dst_ref, nch_ref, *refs, final):
    if final:
        y_ref, mod_ref, w_ref, fg_ref, ys_hbm, op_ref, ol_ref, stage_ref, g1_ref, g2_ref, sem, o_ref = refs
    else:
        y_ref, mod_ref, w_ref, ys_hbm, o_ref, stage_ref, g1_ref, g2_ref, sem = refs
    i = pl.program_id(0)
    copies = _SegmentCopies(seg_ref, dst_ref, nch_ref, stage_ref, ys_hbm, sem, to_hbm=False)
    pl.when(i == 0)(lambda: copies.start(i))
    pl.when(i + 1 < pl.num_programs(0))(lambda: copies.start(i + 1))
    copies.wait(i)
    slot = i % 2
    base = i * TM

    def pick(t8, carry):
        for u in range(SUBLANE):
            t = t8 * SUBLANE + u
            rows = pl.ds(pl.multiple_of(t * SLAB, SLAB), SLAB)
            g1_ref[rows, :] = stage_ref[slot, l1_ref[base + t]]
            g2_ref[rows, :] = stage_ref[slot, l2_ref[base + t]]
        return carry

    lax.fori_loop(0, TM // SUBLANE, pick, 0)
    w1 = w_ref[:, 0:1]
    w2 = w_ref[:, 1:2]
    for k in range(SLAB):
        cs = slice(k * LANE, (k + 1) * LANE)
        chunk_k = pl.ds(k, TM, stride=SLAB)
        f = w1 * g1_ref[chunk_k, :] + w2 * g2_ref[chunk_k, :]
        o_ref[:, cs] = y_ref[:, cs] + mod_ref[5:6, cs] * f
    if final:
        yn = o_ref[...]
        out = yn * lax.rsqrt(jnp.mean(yn * yn, axis=-1, keepdims=True) + 1e-6) * fg_ref[...]

        def write(part):
            (op_ref, ol_ref)[part][...] = out

        _for_stream_of_tile(write)


def _combine(tabs, y, mod, w_cols, ys, final_g=None):
    final = final_g is not None
    row_spec = pl.BlockSpec((TM, D), lambda i, *_: (i, 0))
    scratch = [
        pltpu.VMEM((2, STAGE_ROWS, SLAB, LANE), F32),
        pltpu.VMEM((TM * SLAB, LANE), F32),
        pltpu.VMEM((TM * SLAB, LANE), F32),
        pltpu.SemaphoreType.DMA((2,)),
    ]
    if final:
        out_specs = [pl.BlockSpec((TM, D), lambda i, *_: (jnp.minimum(i, N_PROMPT_TILES - 1), 0)),
                     pl.BlockSpec((TM, D), lambda i, *_: (jnp.maximum(i - N_PROMPT_TILES, 0), 0))]
        out_shape = [jax.ShapeDtypeStruct((N_P, D), F32), jax.ShapeDtypeStruct((N_L, D), F32)]
        scratch.append(pltpu.VMEM((TM, D), F32))
        extra_specs, extra_args = [pl.BlockSpec((1, D), lambda i, *_: (0, 0))], [final_g.reshape(1, D)]
    else:
        out_specs, out_shape, extra_specs, extra_args = row_spec, jax.ShapeDtypeStruct((N_TOK, D), F32), [], []
    return pl.pallas_call(
        functools.partial(_combine_kernel, final=final),
        grid_spec=pltpu.PrefetchScalarGridSpec(
            num_scalar_prefetch=5,
            grid=(N_TOK // TM,),
            in_specs=[
                row_spec,
                pl.BlockSpec((None, 6, D), lambda i, *_: (_cond_of_tile(i, TM), 0, 0)),
                pl.BlockSpec((TM, 2), lambda i, *_: (i, 0)),
                *extra_specs,
                pl.BlockSpec(memory_space=pl.ANY),
            ],
            out_specs=out_specs,
            scratch_shapes=scratch,
        ),
        out_shape=out_shape,
        compiler_params=_cparams(("arbitrary",)),
        name="moe_combine",
    )(*tabs, y, mod, w_cols, *extra_args, ys)


def _moe(y, g, mod, router, w_gu, w_down, slot, mixer_out=None, final_g=None):
    y, hs, ii, wf, cnt, tbase = _router(y, g, mod, router.T.astype(BF16), mixer_out)
    n_tt = N_TOK // TM
    counts = cnt[:, 0]
    before = tbase[:, 0].reshape(n_tt, N_EXP)
    n_seg = jnp.concatenate([before[1:], counts[None]], axis=0) - before
    region = (counts + CH + TM_E - 1) // TM_E * TM_E
    region_end = jnp.cumsum(region)
    offs = region_end - region
    seg_pad = (n_seg + CH - 1) // CH * CH
    seg = jnp.cumsum(seg_pad, axis=1) - seg_pad
    dst = offs[None, :] + before
    e_ids = jnp.arange(N_EXP, dtype=I32)[:, None]
    shift = jnp.repeat((seg - before).T, TM, axis=1)
    l1 = jnp.sum(jnp.where(ii[0:1] == e_ids, shift, 0), axis=0) + ii[2]
    l2 = jnp.sum(jnp.where(ii[1:2] == e_ids, shift, 0), axis=0) + ii[3]
    tabs = (l1, l2, seg.reshape(-1), dst.reshape(-1), (seg_pad // CH).reshape(-1))
    row0 = jnp.arange(MAX_TILES, dtype=I32) * TM_E
    tile_expert = jnp.minimum(jnp.sum((row0[:, None] >= region_end[None, :]).astype(I32), axis=1), N_EXP - 1)
    tile_valid = (row0 < (offs + counts)[tile_expert]).astype(I32)
    in_region = row0 < region_end[-1]
    j = (row0 - offs[tile_expert]) // TM_E
    n = (region // TM_E)[tile_expert]
    streams = in_region & (tile_expert < N_EXP - 1)
    tile_tabs = (tile_expert, tile_valid, (in_region & (j == 0)).astype(I32),
                 jnp.where(streams, j * N_FF_CHUNKS // n, 0), jnp.where(streams, (j + 1) * N_FF_CHUNKS // n, 0))
    zero_start = offs + counts // CH * CH
    zero_tabs = (jnp.concatenate([zero_start, region_end[-1:]]),
                 jnp.concatenate([region_end - zero_start, MAX_TILES * TM_E - region_end[-1:]]) // CH)
    xs = _dispatch(tabs, zero_tabs, hs)
    ys = _experts(tile_tabs, xs, w_gu, w_down, slot)
    return _combine(tabs, y, mod, wf[0:2].T, ys, final_g)


def kernel(x_prompt, x_sample, state_rglru, cache_k, cache_v, c, c_ctx, norm_g, ada_w, ada_b,
           gm_w_in, gm_ln_g, gm_w_s, gm_b_s, gm_w_out, rg_w_in, rg_conv_w, rg_conv_b, rg_w_gate,
           rg_b_gate, rg_lambda, rg_w_out, att_w_qkv, att_lambda, att_subln_g, att_w_out,
           ff_w_gu, ff_w_down, moe_router, moe_w_gu, moe_w_down, final_g):
    y = (x_prompt.reshape(N_P, D), x_sample.reshape(N_L, D))
    conds = jnp.concatenate([c_ctx[None], c, jnp.zeros((N_COND - 1 - B_L, D), F32)], axis=0)
    mods = _adaln(conds, ada_w, ada_b)
    nb = D_RNN // RNN_BS
    gm_w_in, gm_w_out, rg_w_in, rg_w_out, att_w_qkv, att_w_out, ff_w_gu, ff_w_down = (
        w.astype(BF16) for w in (gm_w_in, gm_w_out, rg_w_in, rg_w_out, att_w_qkv, att_w_out, ff_w_gu, ff_w_down))
    new_rnn = new_k = new_v = None
    for i in range(DEPTH):
        mod = mods[i]
        kind, slot = i % 3, i // 3
        mixer_out = None
        if kind == 0:
            y = _gmlp_mixer(y if i == 0 else (y,), norm_g[i, 0], mod, gm_w_in, gm_ln_g[slot],
                            gm_w_s[slot].astype(BF16), gm_b_s[slot].T, gm_w_out, slot)
        elif kind == 1:
            yx = _mod_matmul(y, norm_g[i, 0], mod, rg_w_in, slot, tn=256, gelu_cols=D_RNN)
            wg_cat = rg_w_gate[slot].transpose(2, 3, 0, 1, 4).reshape(nb, RNN_BS, 4 * RNN_BS).astype(BF16)
            bg_cat = rg_b_gate[slot].reshape(2, 2, nb, RNN_BS).transpose(2, 0, 1, 3).reshape(nb, 1, 4 * RNN_BS)
            rg_args = (yx, rg_conv_w[slot], rg_conv_b[slot], wg_cat, bg_cat, rg_lambda[slot])
            a_p, fin = _rglru(*rg_args, jnp.zeros((2, B_P, D_RNN), F32),
                              row0=0, n_batch=B_P, n_seq=SUBLANE, t_len=T_P)
            a_l, _ = _rglru(*rg_args, state_rglru[:, slot].transpose(1, 0, 2),
                            row0=N_P, n_batch=B_L, n_seq=B_L, t_len=T_L)
            new_rnn = fin.transpose(1, 0, 2)[:, None]
            mixer_out = ((a_p, a_l), rg_w_out, slot)
        else:
            lambda_init = 0.8 - 0.6 * math.exp(-0.3 * i)
            att = (y, norm_g[i, 0], mod, att_w_qkv, slot, att_lambda[slot], att_subln_g[slot], lambda_init)
            a_p, new_k, new_v = _attention(*att, row0=0, n_batch=B_P, t_len=T_P, heads=HEADS, tq=T_P)
            a_l = _attention(*att, row0=N_P, n_batch=B_L, t_len=T_L, heads=HEADS, tq=256,
                             ctx=(cache_k, cache_v, slot), rope_tabs=_rope_tables(T_L))
            new_k = new_k.reshape(B_P, 1, T_P, HEADS, 2 * HD)
            new_v = new_v.reshape(B_P, 1, T_P, HEADS, VD)
            mixer_out = ((a_p, a_l), att_w_out, slot)
        fslot = i // 2
        if i % 2 == 0:
            y = _ffn(y, norm_g[i, 1], mod, ff_w_gu, ff_w_down, fslot, mixer_out)
        else:
            y = _moe(y, norm_g[i, 1], mod, moe_router[fslot], moe_w_gu, moe_w_down, fslot, mixer_out,
                     final_g if i == DEPTH - 1 else None)
    y_prompt, y_sample = y
    return (y_prompt.reshape(B_P, T_P, D), y_sample.reshape(B_L, T_L, D), new_rnn, new_k, new_v)
```

```python
import functools
import math

import jax
import jax.numpy as jnp
import numpy as np
from jax import lax
from jax.experimental import pallas as pl
from jax.experimental.pallas import tpu as pltpu

F32 = jnp.float32
BF16 = jnp.bfloat16
I32 = jnp.int32

D = 1024
DEPTH = 4
B_P, T_P = 32, 256
B_L, T_L = 4, 2048
PAST = 256
N_P = B_P * T_P
N_L = B_L * T_L
N_TOK = N_P + N_L
N_COND = 8
GRID_W = 64
CHUNK = 128
GROUPS = 8
D_RNN = 1280
RNN_BS = 128
CONV_W = 4
RGLRU_C = 8.0
HD = 64
VD = 128
HEADS = 8
ROPE_THETA = 10000.0
LOG2_E = 1.4426950408889634
D_FF = 2816
N_EXP = 8

LANE = 128
SUBLANE = 8
VMEM_LIMIT = 56 * 1024 * 1024
TM = 1024
TM_E = 256
CH = 32
MAX_TILES = -(-(2 * N_TOK + N_EXP * (CH + TM_E - 1)) // TM_E)
STAGE_ROWS = 2 * TM + N_EXP * CH
SLAB = D // LANE


def _cparams(sem):
    return pltpu.CompilerParams(dimension_semantics=sem, vmem_limit_bytes=VMEM_LIMIT)


def _cond_of_tile(i, tm):
    r0 = i * tm
    return jnp.where(r0 < N_P, 0, 1 + (r0 - N_P) // T_L)


def _modnorm(x, g, shift, scale):
    ms = jnp.mean(x * x, axis=-1, keepdims=True)
    return (x * lax.rsqrt(ms + 1e-6)) * (g * (1.0 + scale)) + shift


def _gelu_tanh(x):
    c = math.sqrt(2.0 / math.pi)
    half_x = 0.5 * x
    return half_x + half_x * jnp.tanh(x * (c + (c * 0.044715) * (x * x)))


def _adaln_kernel(c_ref, w_ref, b_ref, o_ref):
    s = jax.nn.silu(c_ref[...]).astype(BF16)
    o_ref[...] = jnp.dot(s, w_ref[...].astype(BF16), preferred_element_type=F32) + b_ref[...]


def _adaln(conds, ada_w, ada_b):
    tn = 1536
    n6 = 6 * D
    out = pl.pallas_call(
        _adaln_kernel,
        grid=(DEPTH, n6 // tn),
        in_specs=[
            pl.BlockSpec((N_COND, D), lambda l, j: (0, 0)),
            pl.BlockSpec((None, D, tn), lambda l, j: (l, 0, j)),
            pl.BlockSpec((None, 1, tn), lambda l, j: (l, 0, j)),
        ],
        out_specs=pl.BlockSpec((None, N_COND, tn), lambda l, j: (l, 0, j)),
        out_shape=jax.ShapeDtypeStruct((DEPTH, N_COND, n6), F32),
        compiler_params=_cparams(("arbitrary", "arbitrary")),
        name="adaln",
    )(conds, ada_w, ada_b.reshape(DEPTH, 1, n6))
    return out.reshape(DEPTH, N_COND, 6, D)


def _mod_matmul_kernel(x_ref, g_ref, mod_ref, w_ref, o_ref, *, tn, gelu_cols):
    h = _modnorm(x_ref[...], g_ref[...], mod_ref[0:1, :], mod_ref[1:2, :]).astype(BF16)
    for c in range(o_ref.shape[1] // tn):
        acc = jnp.dot(h, w_ref[:, c * tn:(c + 1) * tn], preferred_element_type=F32)
        if c * tn < gelu_cols:
            acc = _gelu_tanh(acc)
        o_ref[:, c * tn:(c + 1) * tn] = acc


def _mod_matmul(y, g, mod, w_bf16, slot, *, tn, gelu_cols):
    nout = w_bf16.shape[2]
    return pl.pallas_call(
        functools.partial(_mod_matmul_kernel, tn=tn, gelu_cols=gelu_cols),
        grid=(N_TOK // TM,),
        in_specs=[
            pl.BlockSpec((TM, D), lambda i: (i, 0)),
            pl.BlockSpec((1, D), lambda i: (0, 0)),
            pl.BlockSpec((None, 6, D), lambda i: (_cond_of_tile(i, TM), 0, 0)),
            pl.BlockSpec((None, D, nout), lambda i: (slot, 0, 0)),
        ],
        out_specs=pl.BlockSpec((TM, nout), lambda i: (i, 0)),
        out_shape=jax.ShapeDtypeStruct((N_TOK, nout), F32),
        compiler_params=_cparams(("arbitrary",)),
        name="mod_matmul",
    )(y, g.reshape(1, D), mod, w_bf16)


N_PROMPT_TILES = N_P // TM


def _mixer_out_specs(a_parts, w_out_bf16, slot):
    k = a_parts[0].shape[1]
    return [
        pl.BlockSpec((TM, k), lambda i, *_: (jnp.minimum(i, N_PROMPT_TILES - 1), 0)),
        pl.BlockSpec((TM, k), lambda i, *_: (jnp.maximum(i - N_PROMPT_TILES, 0), 0)),
        pl.BlockSpec((None, k, D), lambda i, *_: (slot, 0, 0)),
    ]


def _for_stream_of_tile(body):
    pl.when(pl.program_id(0) < N_PROMPT_TILES)(lambda: body(0))
    pl.when(pl.program_id(0) >= N_PROMPT_TILES)(lambda: body(1))


def _gmlp_kernel(*refs):
    *y_refs, g_ref, mod_ref, win_ref, lng_ref, ws_ref, bs_ref, wout_ref, o_ref, uv_ref, a_ref = refs
    body = functools.partial(_gmlp_tile, g_ref, mod_ref, win_ref, lng_ref, ws_ref, bs_ref, wout_ref,
                             o_ref, uv_ref, a_ref)
    if len(y_refs) == 1:
        body(y_refs[0])
    else:
        _for_stream_of_tile(lambda part: body(y_refs[part]))


def _gmlp_tile(g_ref, mod_ref, win_ref, lng_ref, ws_ref, bs_ref, wout_ref, o_ref, uv_ref, a_ref, y_ref):
    y = y_ref[...]
    h = _modnorm(y, g_ref[...], mod_ref[0:1, :], mod_ref[1:2, :]).astype(BF16)
    tn = 512
    for c in range(2 * D // tn):
        uv = jnp.dot(h, win_ref[:, c * tn:(c + 1) * tn], preferred_element_type=F32)
        uv_ref[:, c * tn:(c + 1) * tn] = _gelu_tanh(uv)
    for ci in range(TM // CHUNK):
        r0 = ci * CHUNK
        v = uv_ref[r0:r0 + CHUNK, D:]
        vc = v - jnp.mean(v, axis=-1, keepdims=True)
        vn = vc * lax.rsqrt(jnp.mean(vc * vc, axis=-1, keepdims=True) + 1e-5) * lng_ref[...]
        vn = vn.astype(BF16)
        for g in range(GROUPS):
            c0 = g * LANE
            s = jnp.dot(ws_ref[g], vn[:, c0:c0 + LANE], preferred_element_type=F32)
            s = s + bs_ref[:, g:g + 1]
            a_ref[r0:r0 + CHUNK, c0:c0 + LANE] = (uv_ref[r0:r0 + CHUNK, c0:c0 + LANE] * s).astype(BF16)
    o_ref[...] = y + mod_ref[2:3, :] * jnp.dot(a_ref[...], wout_ref[...], preferred_element_type=F32)


def _gmlp_mixer(y_parts, g, mod, w_in_bf16, ln_g, w_s_bf16, b_s_t, w_out_bf16, slot):
    if len(y_parts) == 1:
        y_specs = [pl.BlockSpec((TM, D), lambda i: (i, 0))]
    else:
        y_specs = [pl.BlockSpec((TM, D), lambda i: (jnp.minimum(i, N_PROMPT_TILES - 1), 0)),
                   pl.BlockSpec((TM, D), lambda i: (jnp.maximum(i - N_PROMPT_TILES, 0), 0))]
    return pl.pallas_call(
        _gmlp_kernel,
        grid=(N_TOK // TM,),
        in_specs=y_specs + [
            pl.BlockSpec((1, D), lambda i: (0, 0)),
            pl.BlockSpec((None, 6, D), lambda i: (_cond_of_tile(i, TM), 0, 0)),
            pl.BlockSpec((None, D, 2 * D), lambda i: (slot, 0, 0)),
            pl.BlockSpec((1, D), lambda i: (0, 0)),
            pl.BlockSpec((GROUPS, CHUNK, CHUNK), lambda i: (0, 0, 0)),
            pl.BlockSpec((CHUNK, GROUPS), lambda i: (0, 0)),
            pl.BlockSpec((None, D, D), lambda i: (slot, 0, 0)),
        ],
        out_specs=pl.BlockSpec((TM, D), lambda i: (i, 0)),
        out_shape=jax.ShapeDtypeStruct((N_TOK, D), F32),
        scratch_shapes=[pltpu.VMEM((TM, 2 * D), F32), pltpu.VMEM((TM, D), BF16)],
        compiler_params=_cparams(("arbitrary",)),
        name="gmlp_mixer",
    )(*y_parts, g.reshape(1, D), mod, w_in_bf16, ln_g.reshape(1, D), w_s_bf16, b_s_t, w_out_bf16)


SCAN_UNROLL = 8


def _rglru_kernel(yb_ref, xb_ref, cw_ref, cb_ref, wg_ref, bg_ref, lam_ref, h0_ref,
                  o_ref, fin_ref, af_ref, bf_ref, ab_ref, bb_ref, *, n_seq, t_len, row_chunk):
    stride = t_len + SUBLANE
    tix = lax.broadcasted_iota(I32, (t_len, 1), 0)
    sp_f = jax.nn.softplus(-lam_ref[0:1, :])
    sp_b = jax.nn.softplus(-lam_ref[1:2, :])
    for g in range(n_seq):
        x = xb_ref[g * t_len:(g + 1) * t_len, :]
        xm1 = jnp.where(tix >= 1, pltpu.roll(x, 1, 0), 0.0)
        xp1 = jnp.where(tix <= t_len - 2, pltpu.roll(x, t_len - 1, 0), 0.0)
        xp2 = jnp.where(tix <= t_len - 3, pltpu.roll(x, t_len - 2, 0), 0.0)
        xc = cb_ref[...] + xm1 * cw_ref[0:1, :]
        xc = xc + x * cw_ref[1:2, :]
        xc = xc + xp1 * cw_ref[2:3, :]
        xc = xc + xp2 * cw_ref[3:4, :]
        for r0 in range(0, t_len, row_chunk):
            xcc = xc[r0:r0 + row_chunk]
            gates = jnp.dot(xcc.astype(BF16), wg_ref[...], preferred_element_type=F32) + bg_ref[...]
            d0 = g * stride + r0
            for d, (sp, a_ref, b_ref) in enumerate(((sp_f, af_ref, bf_ref), (sp_b, ab_ref, bb_ref))):
                r = jax.nn.sigmoid(gates[:, (2 * d) * LANE:(2 * d + 1) * LANE])
                i = jax.nn.sigmoid(gates[:, (2 * d + 1) * LANE:(2 * d + 2) * LANE])
                neg_log_a = r * (RGLRU_C * sp)
                a = jnp.exp2(r * ((-RGLRU_C * LOG2_E) * sp))
                one_minus_a2 = jnp.tanh(neg_log_a) * (a * a + 1.0)
                a_ref[d0:d0 + row_chunk, :] = a
                b_ref[d0:d0 + row_chunk, :] = jnp.sqrt(one_minus_a2) * (i * xcc)

    def block_scan(a_ref, b_ref, rows, h_in):
        a = [a_ref[r, :] for r in rows]
        b = [b_ref[r, :] for r in rows]
        d = 1
        while d < SCAN_UNROLL:
            a, b = ([a[u] if u < d else a[u] * a[u - d] for u in range(SCAN_UNROLL)],
                    [b[u] if u < d else a[u] * b[u - d] + b[u] for u in range(SCAN_UNROLL)])
            d *= 2
        h = [a[u] * h_in + b[u] for u in range(SCAN_UNROLL)]
        for r, hu in zip(rows, h):
            b_ref[r, :] = hu
        return h[-1]

    def steps(s8, carry):
        hf, hb = carry
        s0 = s8 * SCAN_UNROLL
        hf = block_scan(af_ref, bf_ref, [pl.ds(s0 + u, n_seq, stride=stride) for u in range(SCAN_UNROLL)], hf)
        hb = block_scan(ab_ref, bb_ref,
                        [pl.ds(t_len - 1 - s0 - u, n_seq, stride=stride) for u in range(SCAN_UNROLL)], hb)
        return hf, hb

    hf, hb = lax.fori_loop(0, t_len // SCAN_UNROLL, steps, (h0_ref[0], h0_ref[1]))
    fin_ref[0] = hf
    fin_ref[1] = hb
    for g in range(n_seq):
        rs = slice(g * stride, g * stride + t_len)
        hsum = bf_ref[rs, :] + bb_ref[rs, :]
        o_ref[g * t_len:(g + 1) * t_len, :] = (hsum * yb_ref[g * t_len:(g + 1) * t_len, :]).astype(BF16)


def _rglru(yx, cw, cb, wg_cat, bg_cat, lam, h0, *, row0, n_batch, n_seq, t_len):
    rows = n_seq * t_len
    nb = D_RNN // RNN_BS
    rb0 = row0 // rows
    kern = functools.partial(_rglru_kernel, n_seq=n_seq, t_len=t_len, row_chunk=min(t_len, 512))
    scan_buf = pltpu.VMEM((n_seq * (t_len + SUBLANE), RNN_BS), F32)
    return pl.pallas_call(
        kern,
        grid=(n_batch // n_seq, nb),
        in_specs=[
            pl.BlockSpec((rows, RNN_BS), lambda s, c: (rb0 + s, c)),
            pl.BlockSpec((rows, RNN_BS), lambda s, c: (rb0 + s, nb + c)),
            pl.BlockSpec((CONV_W, RNN_BS), lambda s, c: (0, c)),
            pl.BlockSpec((1, RNN_BS), lambda s, c: (0, c)),
            pl.BlockSpec((None, RNN_BS, 4 * RNN_BS), lambda s, c: (c, 0, 0)),
            pl.BlockSpec((None, 1, 4 * RNN_BS), lambda s, c: (c, 0, 0)),
            pl.BlockSpec((2, RNN_BS), lambda s, c: (0, c)),
            pl.BlockSpec((2, n_seq, RNN_BS), lambda s, c: (0, s, c)),
        ],
        out_specs=[
            pl.BlockSpec((rows, RNN_BS), lambda s, c: (s, c)),
            pl.BlockSpec((2, n_seq, RNN_BS), lambda s, c: (0, s, c)),
        ],
        out_shape=[
            jax.ShapeDtypeStruct((n_batch * t_len, D_RNN), BF16),
            jax.ShapeDtypeStruct((2, n_batch, D_RNN), F32),
        ],
        scratch_shapes=[scan_buf, scan_buf, scan_buf, scan_buf],
        compiler_params=_cparams(("arbitrary", "arbitrary")),
        name="rglru",
    )(yx, yx, cw, cb.reshape(1, D_RNN), wg_cat, bg_cat, lam, h0)


def _rope(x, cos, sin_signed, first_of_pair):
    partner = jnp.where(first_of_pair, pltpu.roll(x, LANE - 16, 1), pltpu.roll(x, 16, 1))
    return x * cos + partner * sin_signed


def _attn_kernel(*refs, heads, t_len, tq, s_len, rope, lambda_init, mxu_denominator):
    if rope:
        (lam_ref, g_ref, y_ref, ng_ref, mod_ref, wq_ref, wk_ref, wv_ref, ck_ref, cv_ref, cq_ref, sq_ref,
         ckk_ref, skk_ref, o_ref, h_ref, kk_ref, vv_ref) = refs
    else:
        (lam_ref, g_ref, y_ref, ng_ref, mod_ref, wq_ref, wk_ref, wv_ref,
         o_ref, ko_ref, vo_ref, h_ref, kk_ref, vv_ref) = refs
    lane = lax.broadcasted_iota(I32, (1, LANE), 1)
    first_of_pair = (lane & 16) == 0
    map0 = lane < HD

    @pl.when((pl.program_id(1) == 0) & (pl.program_id(2) == 0))
    def _():
        h_ref[...] = _modnorm(y_ref[...], ng_ref[...], mod_ref[0:1, :], mod_ref[1:2, :]).astype(BF16)

    @pl.when(pl.program_id(2) == 0)
    def _():
        k_all = jnp.dot(h_ref[...], wk_ref[...], preferred_element_type=F32)
        v_all = jnp.dot(h_ref[...], wv_ref[...], preferred_element_type=F32)
        if not rope:
            ko_ref[...] = k_all
            vo_ref[...] = v_all
        for j in range(heads):
            cs = slice(j * LANE, (j + 1) * LANE)
            k = k_all[:, cs]
            if rope:
                k = _rope(k, ckk_ref[...], skk_ref[...], first_of_pair)
            kk_ref[j, 0:t_len, :] = k.astype(BF16)
            vv_ref[j, 0:t_len, 0:LANE] = v_all[:, cs].astype(BF16)
            if s_len > t_len:
                ctx_rows = pl.ds(pl.program_id(1) * heads + j, s_len - t_len, stride=HEADS)
                kk_ref[j, t_len:s_len, :] = ck_ref[ctx_rows, :].astype(BF16)
                vv_ref[j, t_len:s_len, 0:LANE] = cv_ref[ctx_rows, :].astype(BF16)
            if mxu_denominator:
                vv_ref[j, :, LANE:2 * LANE] = jnp.ones((s_len, LANE), BF16)

    lp = lam_ref[...]
    lam = (jnp.exp(jnp.sum(lp[0:1] * lp[1:2], axis=-1, keepdims=True))
           - jnp.exp(jnp.sum(lp[2:3] * lp[3:4], axis=-1, keepdims=True)) + lambda_init)
    q_rows = pl.ds(pl.multiple_of(pl.program_id(2) * tq, tq), tq)
    q_all = jnp.dot(h_ref[q_rows, :], wq_ref[...], preferred_element_type=F32)
    for j in range(heads):
        cs = slice(j * LANE, (j + 1) * LANE)
        q = q_all[:, cs]
        if rope:
            q = _rope(q, cq_ref[...], sq_ref[...], first_of_pair)
        q = q * (HD ** -0.5 * LOG2_E)
        q2 = jnp.concatenate([jnp.where(map0, q, 0.0), jnp.where(map0, 0.0, q)], axis=0).astype(BF16)
        s = lax.dot_general(q2, kk_ref[j], (((1,), (1,)), ((), ())), preferred_element_type=F32)
        e = jnp.exp2(s - jnp.max(s, axis=-1, keepdims=True))
        if mxu_denominator:
            nd = jnp.dot(e.astype(BF16), vv_ref[j], preferred_element_type=F32)
            av = nd[:, 0:LANE] * (1.0 / nd[:, LANE:LANE + 1])
            o = av[0:tq] - lam * av[tq:2 * tq]
        else:
            p = e * (1.0 / jnp.sum(e, axis=-1, keepdims=True))
            w = p[0:tq] - lam * p[tq:2 * tq]
            o = jnp.dot(w.astype(BF16), vv_ref[j], preferred_element_type=F32)
        o = o * lax.rsqrt(jnp.mean(o * o, axis=-1, keepdims=True) + 1e-6) * g_ref[...]
        o_ref[:, cs] = (o * (1.0 - lambda_init)).astype(BF16)


def _attention(y, norm_g, mod, w_qkv_bf16, wslot, lam_p, subln_g, lambda_init, *, row0, n_batch, t_len, heads, tq,
               ctx=None, rope_tabs=None):
    rope = rope_tabs is not None
    s_len = t_len + (PAST if rope else 0)
    w = heads * LANE
    nhb = HEADS // heads
    nq = t_len // tq
    rk0 = row0 // t_len
    cond0 = 1 if rope else 0
    mxu_denominator = s_len >= 1024
    kern = functools.partial(_attn_kernel, heads=heads, t_len=t_len, tq=tq, s_len=s_len, rope=rope,
                             lambda_init=lambda_init, mxu_denominator=mxu_denominator)
    v_width = 2 * LANE if mxu_denominator else LANE
    in_specs = [
        pl.BlockSpec((4, HD), lambda b, h, q: (0, 0)),
        pl.BlockSpec((1, VD), lambda b, h, q: (0, 0)),
        pl.BlockSpec((t_len, D), lambda b, h, q: (rk0 + b, 0), pipeline_mode=pl.Buffered(1 if rope else 2)),
        pl.BlockSpec((1, D), lambda b, h, q: (0, 0)),
        pl.BlockSpec((None, 6, D), lambda b, h, q: (b * cond0 + cond0, 0, 0)),
        pl.BlockSpec((None, D, w), lambda b, h, q: (wslot, 0, h)),
        pl.BlockSpec((None, D, w), lambda b, h, q: (wslot, 0, nhb + h)),
        pl.BlockSpec((None, D, w), lambda b, h, q: (wslot, 0, 2 * nhb + h)),
    ]
    args = [lam_p, subln_g.reshape(1, VD), y, norm_g.reshape(1, D), mod, w_qkv_bf16, w_qkv_bf16, w_qkv_bf16]
    row_block = pl.BlockSpec((tq, w), lambda b, h, q: (b * nq + q, h))
    out_specs, out_shape = row_block, jax.ShapeDtypeStruct((n_batch * t_len, D), BF16)
    if not rope:
        seq_block = pl.BlockSpec((t_len, w), lambda b, h, q: (b, h))
        out_specs = [row_block, seq_block, seq_block]
        out_shape = [out_shape] + [jax.ShapeDtypeStruct((n_batch * t_len, D), F32)] * 2
    if rope:
        cos_t, sin_t = rope_tabs
        cache_k, cache_v, slot = ctx
        n_slots = cache_k.shape[1]
        ctx_k = cache_k.reshape(n_batch, n_slots, PAST * HEADS, LANE)
        ctx_v = cache_v.reshape(n_batch, n_slots, PAST * HEADS, LANE)
        in_specs += [
            pl.BlockSpec((None, None, PAST * HEADS, LANE), lambda b, h, q: (b, slot, 0, 0)),
            pl.BlockSpec((None, None, PAST * HEADS, LANE), lambda b, h, q: (b, slot, 0, 0)),
            pl.BlockSpec((tq, LANE), lambda b, h, q: (q, 0)),
            pl.BlockSpec((tq, LANE), lambda b, h, q: (q, 0)),
            pl.BlockSpec((t_len, LANE), lambda b, h, q: (0, 0)),
            pl.BlockSpec((t_len, LANE), lambda b, h, q: (0, 0)),
        ]
        args += [ctx_k, ctx_v, cos_t, sin_t, cos_t, sin_t]
    return pl.pallas_call(
        kern,
        grid=(n_batch, nhb, nq),
        in_specs=in_specs,
        out_specs=out_specs,
        out_shape=out_shape,
        scratch_shapes=[pltpu.VMEM((t_len, D), BF16), pltpu.VMEM((heads, s_len, LANE), BF16),
                        pltpu.VMEM((heads, s_len, v_width), BF16)],
        compiler_params=_cparams(("arbitrary", "arbitrary", "arbitrary")),
        name="diff_attn",
    )(*args)


def _rope_tables(t_len):
    t = np.arange(t_len)
    half = HD // 2
    freqs = (ROPE_THETA ** (-np.arange(0, half, 2, dtype=np.float32) / half)).astype(np.float32)
    ang_r = (t // GRID_W).astype(np.float32)[:, None] * freqs[None]
    ang_c = (t % GRID_W).astype(np.float32)[:, None] * freqs[None]
    cr, sr, cc, sc = np.cos(ang_r), np.sin(ang_r), np.cos(ang_c), np.sin(ang_c)
    cos64 = np.concatenate([cr, cr, cc, cc], axis=-1)
    sin64 = np.concatenate([-sr, sr, -sc, sc], axis=-1)
    return (jnp.asarray(np.concatenate([cos64, cos64], axis=-1), F32),
            jnp.asarray(np.concatenate([sin64, sin64], axis=-1), F32))


def _swiglu_rows(h_bf16, wgu_ref, wd_ref, h1_ref):
    tn = 256
    for c in range(D_FF // tn):
        gg = jnp.dot(h_bf16, wgu_ref[:, c * tn:(c + 1) * tn], preferred_element_type=F32)
        uu = jnp.dot(h_bf16, wgu_ref[:, D_FF + c * tn:D_FF + (c + 1) * tn], preferred_element_type=F32)
        h1_ref[:, c * tn:(c + 1) * tn] = (jax.nn.silu(gg) * uu).astype(BF16)
    return jnp.dot(h1_ref[...], wd_ref[...], preferred_element_type=F32)


def _ffn_kernel(*refs, mixer_out):
    if mixer_out:
        ap_ref, al_ref, wout_ref, y_ref, g_ref, mod_ref, wgu_ref, wd_ref, o_ref, h1_ref = refs
    else:
        y_ref, g_ref, mod_ref, wgu_ref, wd_ref, o_ref, h1_ref = refs

    def body(a_ref):
        y = y_ref[...]
        if a_ref is not None:
            y = y + mod_ref[2:3, :] * jnp.dot(a_ref[...], wout_ref[...], preferred_element_type=F32)
        h = _modnorm(y, g_ref[...], mod_ref[3:4, :], mod_ref[4:5, :]).astype(BF16)
        o_ref[...] = y + mod_ref[5:6, :] * _swiglu_rows(h, wgu_ref, wd_ref, h1_ref)

    if mixer_out:
        _for_stream_of_tile(lambda part: body((ap_ref, al_ref)[part]))
    else:
        body(None)


def _ffn(y, g, mod, wgu_bf16, wd_bf16, slot, mixer_out=None):
    pre_specs, pre_args = [], []
    if mixer_out is not None:
        pre_specs = _mixer_out_specs(*mixer_out)
        pre_args = [*mixer_out[0], mixer_out[1]]
    return pl.pallas_call(
        functools.partial(_ffn_kernel, mixer_out=mixer_out is not None),
        grid=(N_TOK // TM,),
        in_specs=pre_specs + [
            pl.BlockSpec((TM, D), lambda i: (i, 0)),
            pl.BlockSpec((1, D), lambda i: (0, 0)),
            pl.BlockSpec((None, 6, D), lambda i: (_cond_of_tile(i, TM), 0, 0)),
            pl.BlockSpec((None, D, 2 * D_FF), lambda i: (slot, 0, 0), pipeline_mode=pl.Buffered(1)),
            pl.BlockSpec((None, D_FF, D), lambda i: (slot, 0, 0), pipeline_mode=pl.Buffered(1)),
        ],
        out_specs=pl.BlockSpec((TM, D), lambda i: (i, 0)),
        out_shape=jax.ShapeDtypeStruct((N_TOK, D), F32),
        scratch_shapes=[pltpu.VMEM((TM, D_FF), BF16)],
        compiler_params=_cparams(("arbitrary",)),
        name="ffn",
    )(*pre_args, y, g.reshape(1, D), mod, wgu_bf16, wd_bf16)


def _router_kernel(*refs, mixer_out):
    if mixer_out:
        (ap_ref, al_ref, wout_ref, y_ref, g_ref, mod_ref, rt_ref,
         ym_ref, hs_ref, ii_ref, wf_ref, cnt_ref, tb_ref, carry_ref, tri_ref) = refs

        def add_mixer_out(part):
            acc = jnp.dot((ap_ref, al_ref)[part][...], wout_ref[...], preferred_element_type=F32)
            ym_ref[...] = y_ref[...] + mod_ref[2:3, :] * acc

        _for_stream_of_tile(add_mixer_out)
        y = ym_ref[...]
    else:
        y_ref, g_ref, mod_ref, rt_ref, hs_ref, ii_ref, wf_ref, cnt_ref, tb_ref, carry_ref, tri_ref = refs
        y = y_ref[...]
    tm = y_ref.shape[0]

    @pl.when(pl.program_id(0) == 0)
    def _():
        carry_ref[...] = jnp.zeros_like(carry_ref)
        before = (lax.broadcasted_iota(I32, (tm, tm), 0) < lax.broadcasted_iota(I32, (tm, tm), 1))
        tri_ref[...] = jnp.where(before, 1.0, 0.0).astype(BF16)

    tb_ref[...] = carry_ref[...].astype(I32)

    h = _modnorm(y, g_ref[...], mod_ref[3:4, :], mod_ref[4:5, :])
    for k in range(SLAB):
        hs_ref[:, k, :] = h[:, k * LANE:(k + 1) * LANE]
    logits = lax.dot_general(rt_ref[...], h.astype(BF16), (((1,), (1,)), ((), ())),
                             preferred_element_type=F32)
    eidx = lax.broadcasted_iota(I32, (N_EXP, tm), 0)
    eidx_f = eidx.astype(F32)
    m1 = jnp.max(logits, axis=0, keepdims=True)
    i1 = jnp.min(jnp.where(logits == m1, eidx_f, float(N_EXP)), axis=0, keepdims=True)
    oh1 = eidx_f == i1
    rest = jnp.where(oh1, -jnp.inf, logits)
    m2 = jnp.max(rest, axis=0, keepdims=True)
    i2 = jnp.min(jnp.where(rest == m2, eidx_f, float(N_EXP)), axis=0, keepdims=True)
    oh2 = eidx_f == i2
    i1 = i1.astype(I32)
    i2 = i2.astype(I32)
    e2 = jnp.exp(m2 - m1)
    w1 = 1.0 / (1.0 + e2)
    w2 = e2 / (1.0 + e2)
    sel = jnp.where(oh1 | oh2, 1.0, 0.0)
    cum = jnp.dot(sel.astype(BF16), tri_ref[...], preferred_element_type=F32)
    cum = cum + carry_ref[:, 0:1]
    r1 = jnp.sum(jnp.where(oh1, cum, 0.0), axis=0, keepdims=True).astype(I32)
    r2 = jnp.sum(jnp.where(oh2, cum, 0.0), axis=0, keepdims=True).astype(I32)
    ii_ref[...] = jnp.where(eidx == 0, i1, jnp.where(eidx == 1, i2, jnp.where(eidx == 2, r1,
                            jnp.where(eidx == 3, r2, 0))))
    wf_ref[...] = jnp.where(eidx == 0, w1, jnp.where(eidx == 1, w2, 0.0))
    carry_ref[...] = carry_ref[...] + jnp.sum(sel, axis=1, keepdims=True)
    cnt_ref[...] = carry_ref[...].astype(I32)


def _router(y, g, mod, router_t_bf16, mixer_out=None):
    pre_specs, pre_args, pre_out_specs, pre_out_shape = [], [], [], []
    if mixer_out is not None:
        pre_specs = _mixer_out_specs(*mixer_out)
        pre_args = [*mixer_out[0], mixer_out[1]]
        pre_out_specs = [pl.BlockSpec((TM, D), lambda i: (i, 0))]
        pre_out_shape = [jax.ShapeDtypeStruct((N_TOK, D), F32)]
    outs = pl.pallas_call(
        functools.partial(_router_kernel, mixer_out=mixer_out is not None),
        grid=(N_TOK // TM,),
        in_specs=pre_specs + [
            pl.BlockSpec((TM, D), lambda i: (i, 0)),
            pl.BlockSpec((1, D), lambda i: (0, 0)),
            pl.BlockSpec((None, 6, D), lambda i: (_cond_of_tile(i, TM), 0, 0)),
            pl.BlockSpec((N_EXP, D), lambda i: (0, 0)),
        ],
        out_specs=pre_out_specs + [
            pl.BlockSpec((TM, SLAB, LANE), lambda i: (i, 0, 0)),
            pl.BlockSpec((N_EXP, TM), lambda i: (0, i)),
            pl.BlockSpec((N_EXP, TM), lambda i: (0, i)),
            pl.BlockSpec((N_EXP, LANE), lambda i: (0, 0)),
            pl.BlockSpec((N_EXP, LANE), lambda i: (i, 0)),
        ],
        out_shape=pre_out_shape + [
            jax.ShapeDtypeStruct((N_TOK, SLAB, LANE), F32),
            jax.ShapeDtypeStruct((N_EXP, N_TOK), I32),
            jax.ShapeDtypeStruct((N_EXP, N_TOK), F32),
            jax.ShapeDtypeStruct((N_EXP, LANE), I32),
            jax.ShapeDtypeStruct((N_TOK // TM * N_EXP, LANE), I32),
        ],
        scratch_shapes=[pltpu.VMEM((N_EXP, LANE), F32), pltpu.VMEM((TM, TM), BF16)],
        compiler_params=_cparams(("arbitrary",)),
        name="router",
    )(*pre_args, y, g.reshape(1, D), mod, router_t_bf16)
    return tuple(outs) if mixer_out is not None else (y, *outs)


class _SegmentCopies:
    def __init__(self, seg_ref, dst_ref, nch_ref, stage_ref, hbm_ref, sem, *, to_hbm):
        self.tabs = (seg_ref, dst_ref, nch_ref)
        self.stage_ref, self.hbm_ref, self.sem, self.to_hbm = stage_ref, hbm_ref, sem, to_hbm

    def _chunk(self, slot, s_row, d_row):
        st = self.stage_ref.at[slot, pl.ds(s_row, CH)]
        hb = self.hbm_ref.at[pl.ds(d_row, CH)]
        sem = self.sem.at[slot]
        return pltpu.make_async_copy(st, hb, sem) if self.to_hbm else pltpu.make_async_copy(hb, st, sem)

    def start(self, tile):
        seg_ref, dst_ref, nch_ref = self.tabs
        slot = tile % 2
        for e in range(N_EXP):
            k = tile * N_EXP + e
            s0, d0 = seg_ref[k], dst_ref[k]

            def start_chunk(c, carry, s0=s0, d0=d0):
                self._chunk(slot, s0 + c * CH, d0 + c * CH).start()
                return carry

            lax.fori_loop(0, nch_ref[k], start_chunk, 0)

    def wait(self, tile):
        nch_ref = self.tabs[2]
        slot = tile % 2
        total = 0
        for e in range(N_EXP):
            total = total + nch_ref[tile * N_EXP + e]

        def wait_chunk(c, carry):
            self._chunk(slot, 0, 0).wait()
            return carry

        lax.fori_loop(0, total, wait_chunk, 0)


def _dispatch_kernel(l1_ref, l2_ref, seg_ref, dst_ref, nch_ref, zs_ref, zn_ref, hs_ref, xs_hbm, stage_ref, sem):
    i = pl.program_id(0)
    n_tiles = pl.num_programs(0)
    copies = _SegmentCopies(seg_ref, dst_ref, nch_ref, stage_ref, xs_hbm, sem, to_hbm=True)

    @pl.when(i == 0)
    def _():
        stage_ref[...] = jnp.zeros_like(stage_ref)

        def zero_chunk(row):
            return pltpu.make_async_copy(stage_ref.at[0, pl.ds(0, CH)], xs_hbm.at[pl.ds(row, CH)], sem.at[0])

        total = 0
        for z in range(N_EXP + 1):
            z0, n = zs_ref[z], zn_ref[z]

            def start(c, carry, z0=z0):
                zero_chunk(z0 + c * CH).start()
                return carry

            lax.fori_loop(0, n, start, 0)
            total = total + n

        def wait(c, carry):
            zero_chunk(0).wait()
            return carry

        lax.fori_loop(0, total, wait, 0)

    slot = i % 2
    base = i * TM

    def place(t8, carry):
        for u in range(SUBLANE):
            t = t8 * SUBLANE + u
            row = hs_ref[t]
            stage_ref[slot, l1_ref[base + t]] = row
            stage_ref[slot, l2_ref[base + t]] = row
        return carry

    lax.fori_loop(0, TM // SUBLANE, place, 0)
    pl.when(i >= 1)(lambda: copies.wait(i - 1))
    copies.start(i)
    pl.when(i == n_tiles - 1)(lambda: copies.wait(i))


def _dispatch(tabs, zero_tabs, hs):
    return pl.pallas_call(
        _dispatch_kernel,
        grid_spec=pltpu.PrefetchScalarGridSpec(
            num_scalar_prefetch=7,
            grid=(N_TOK // TM,),
            in_specs=[pl.BlockSpec((TM, SLAB, LANE), lambda i, *_: (i, 0, 0))],
            out_specs=pl.BlockSpec(memory_space=pl.ANY),
            scratch_shapes=[pltpu.VMEM((2, STAGE_ROWS, SLAB, LANE), F32), pltpu.SemaphoreType.DMA((2,))],
        ),
        out_shape=jax.ShapeDtypeStruct((MAX_TILES * TM_E, SLAB, LANE), F32),
        compiler_params=_cparams(("arbitrary",)),
        name="moe_dispatch",
    )(*tabs, *zero_tabs, hs)


FF_CHUNK = 256
N_FF_CHUNKS = D_FF // FF_CHUNK


def _expert_kernel(te_ref, tv_ref, first_ref, clo_ref, chi_ref, x_ref, wgu_hbm, wd_hbm, o_ref,
                   wg_ref, wu_ref, wd_ref, sg_ref, su_ref, sd_ref, xb_ref, h1_ref, sem, *, layer):
    i = pl.program_id(0)
    e = te_ref[i]
    cur = e % 2

    def copies(expert, c):
        s = c % 2
        cols = slice(c * FF_CHUNK, (c + 1) * FF_CHUNK)
        ucols = slice(D_FF + c * FF_CHUNK, D_FF + (c + 1) * FF_CHUNK)
        return (pltpu.make_async_copy(wgu_hbm.at[layer, expert, :, cols], sg_ref.at[s], sem.at[s]),
                pltpu.make_async_copy(wgu_hbm.at[layer, expert, :, ucols], su_ref.at[s], sem.at[s]),
                pltpu.make_async_copy(wd_hbm.at[layer, expert, cols, :], sd_ref.at[s], sem.at[s]))

    def fetch(expert, c):
        for cp in copies(expert, c):
            cp.start()

    def convert(expert, c, wslot):
        for cp in copies(expert, c):
            cp.wait()
        s = c % 2
        wg_ref[wslot, c] = sg_ref[s].astype(BF16)
        wu_ref[wslot, c] = su_ref[s].astype(BF16)
        wd_ref[wslot, c * FF_CHUNK:(c + 1) * FF_CHUNK, :] = sd_ref[s].astype(BF16)
        if c + 2 < N_FF_CHUNKS:
            fetch(expert, c + 2)

    @pl.when(i == 0)
    def _():
        fetch(e, 0)
        fetch(e, 1)
        for c in range(N_FF_CHUNKS):
            convert(e, c, cur)

    nxt = jnp.minimum(e + 1, N_EXP - 1)

    @pl.when((first_ref[i] == 1) & (e + 1 < N_EXP))
    def _():
        fetch(nxt, 0)
        fetch(nxt, 1)

    valid = tv_ref[i] == 1

    @pl.when(valid)
    def _():
        for k in range(SLAB):
            xb_ref[:, k * LANE:(k + 1) * LANE] = x_ref[pl.ds(k, TM_E, stride=SLAB), :].astype(BF16)
        xb = xb_ref[...]
        for c in range(N_FF_CHUNKS):
            gg = jnp.dot(xb, wg_ref[cur, c], preferred_element_type=F32)
            uu = jnp.dot(xb, wu_ref[cur, c], preferred_element_type=F32)
            h1_ref[:, c * FF_CHUNK:(c + 1) * FF_CHUNK] = (jax.nn.silu(gg) * uu).astype(BF16)
        acc = jnp.dot(h1_ref[...], wd_ref[cur], preferred_element_type=F32)
        for k in range(SLAB):
            o_ref[:, k, :] = acc[:, k * LANE:(k + 1) * LANE]

    @pl.when(jnp.logical_not(valid))
    def _():
        o_ref[...] = jnp.zeros_like(o_ref)

    for c in range(N_FF_CHUNKS):
        pl.when((clo_ref[i] <= c) & (c < chi_ref[i]))(functools.partial(convert, nxt, c, 1 - cur))


def _experts(tile_tabs, xs, w_gu, w_down, layer):
    return pl.pallas_call(
        functools.partial(_expert_kernel, layer=layer),
        grid_spec=pltpu.PrefetchScalarGridSpec(
            num_scalar_prefetch=5,
            grid=(MAX_TILES,),
            in_specs=[
                pl.BlockSpec((TM_E * SLAB, LANE), lambda i, *_: (i, 0)),
                pl.BlockSpec(memory_space=pl.ANY),
                pl.BlockSpec(memory_space=pl.ANY),
            ],
            out_specs=pl.BlockSpec((TM_E, SLAB, LANE), lambda i, *_: (i, 0, 0)),
            scratch_shapes=[
                pltpu.VMEM((2, N_FF_CHUNKS, D, FF_CHUNK), BF16),
                pltpu.VMEM((2, N_FF_CHUNKS, D, FF_CHUNK), BF16),
                pltpu.VMEM((2, D_FF, D), BF16),
                pltpu.VMEM((2, D, FF_CHUNK), F32),
                pltpu.VMEM((2, D, FF_CHUNK), F32),
                pltpu.VMEM((2, FF_CHUNK, D), F32),
                pltpu.VMEM((TM_E, D), BF16),
                pltpu.VMEM((TM_E, D_FF), BF16),
                pltpu.SemaphoreType.DMA((2,)),
            ],
        ),
        out_shape=jax.ShapeDtypeStruct(xs.shape, F32),
        compiler_params=_cparams(("arbitrary",)),
        name="moe_experts",
    )(*tile_tabs, xs.reshape(-1, LANE), w_gu, w_down)


def _combine_kernel(l1_ref, l2_ref, seg_ref, dst_ref, nch_ref, *refs, final):
    if final:
        y_ref, mod_ref, w_ref, fg_ref, ys_hbm, op_ref, ol_ref, stage_ref, g1_ref, g2_ref, sem, o_ref = refs
    else:
        y_ref, mod_ref, w_ref, ys_hbm, o_ref, stage_ref, g1_ref, g2_ref, sem = refs
    i = pl.program_id(0)
    copies = _SegmentCopies(seg_ref, dst_ref, nch_ref, stage_ref, ys_hbm, sem, to_hbm=False)
    pl.when(i == 0)(lambda: copies.start(i))
    pl.when(i + 1 < pl.num_programs(0))(lambda: copies.start(i + 1))
    copies.wait(i)
    slot = i % 2
    base = i * TM

    def pick(t8, carry):
        for u in range(SUBLANE):
            t = t8 * SUBLANE + u
            rows = pl.ds(pl.multiple_of(t * SLAB, SLAB), SLAB)
            g1_ref[rows, :] = stage_ref[slot, l1_ref[base + t]]
            g2_ref[rows, :] = stage_ref[slot, l2_ref[base + t]]
        return carry

    lax.fori_loop(0, TM // SUBLANE, pick, 0)
    w1 = w_ref[:, 0:1]
    w2 = w_ref[:, 1:2]
    for k in range(SLAB):
        cs = slice(k * LANE, (k + 1) * LANE)
        chunk_k = pl.ds(k, TM, stride=SLAB)
        f = w1 * g1_ref[chunk_k, :] + w2 * g2_ref[chunk_k, :]
        o_ref[:, cs] = y_ref[:, cs] + mod_ref[5:6, cs] * f
    if final:
        yn = o_ref[...]
        out = yn * lax.rsqrt(jnp.mean(yn * yn, axis=-1, keepdims=True) + 1e-6) * fg_ref[...]

        def write(part):
            (op_ref, ol_ref)[part][...] = out

        _for_stream_of_tile(write)


def _combine(tabs, y, mod, w_cols, ys, final_g=None):
    final = final_g is not None
    row_spec = pl.BlockSpec((TM, D), lambda i, *_: (i, 0))
    scratch = [
        pltpu.VMEM((2, STAGE_ROWS, SLAB, LANE), F32),
        pltpu.VMEM((TM * SLAB, LANE), F32),
        pltpu.VMEM((TM * SLAB, LANE), F32),
        pltpu.SemaphoreType.DMA((2,)),
    ]
    if final:
        out_specs = [pl.BlockSpec((TM, D), lambda i, *_: (jnp.minimum(i, N_PROMPT_TILES - 1), 0)),
                     pl.BlockSpec((TM, D), lambda i, *_: (jnp.maximum(i - N_PROMPT_TILES, 0), 0))]
        out_shape = [jax.ShapeDtypeStruct((N_P, D), F32), jax.ShapeDtypeStruct((N_L, D), F32)]
        scratch.append(pltpu.VMEM((TM, D), F32))
        extra_specs, extra_args = [pl.BlockSpec((1, D), lambda i, *_: (0, 0))], [final_g.reshape(1, D)]
    else:
        out_specs, out_shape, extra_specs, extra_args = row_spec, jax.ShapeDtypeStruct((N_TOK, D), F32), [], []
    return pl.pallas_call(
        functools.partial(_combine_kernel, final=final),
        grid_spec=pltpu.PrefetchScalarGridSpec(
            num_scalar_prefetch=5,
            grid=(N_TOK // TM,),
            in_specs=[
                row_spec,
                pl.BlockSpec((None, 6, D), lambda i, *_: (_cond_of_tile(i, TM), 0, 0)),
                pl.BlockSpec((TM, 2), lambda i, *_: (i, 0)),
                *extra_specs,
                pl.BlockSpec(memory_space=pl.ANY),
            ],
            out_specs=out_specs,
            scratch_shapes=scratch,
        ),
        out_shape=out_shape,
        compiler_params=_cparams(("arbitrary",)),
        name="moe_combine",
    )(*tabs, y, mod, w_cols, *extra_args, ys)


def _moe(y, g, mod, router, w_gu, w_down, slot, mixer_out=None, final_g=None):
    y, hs, ii, wf, cnt, tbase = _router(y, g, mod, router.T.astype(BF16), mixer_out)
    n_tt = N_TOK // TM
    counts = cnt[:, 0]
    before = tbase[:, 0].reshape(n_tt, N_EXP)
    n_seg = jnp.concatenate([before[1:], counts[None]], axis=0) - before
    region = (counts + CH + TM_E - 1) // TM_E * TM_E
    region_end = jnp.cumsum(region)
    offs = region_end - region
    seg_pad = (n_seg + CH - 1) // CH * CH
    seg = jnp.cumsum(seg_pad, axis=1) - seg_pad
    dst = offs[None, :] + before
    e_ids = jnp.arange(N_EXP, dtype=I32)[:, None]
    shift = jnp.repeat((seg - before).T, TM, axis=1)
    l1 = jnp.sum(jnp.where(ii[0:1] == e_ids, shift, 0), axis=0) + ii[2]
    l2 = jnp.sum(jnp.where(ii[1:2] == e_ids, shift, 0), axis=0) + ii[3]
    tabs = (l1, l2, seg.reshape(-1), dst.reshape(-1), (seg_pad // CH).reshape(-1))
    row0 = jnp.arange(MAX_TILES, dtype=I32) * TM_E
    tile_expert = jnp.minimum(jnp.sum((row0[:, None] >= region_end[None, :]).astype(I32), axis=1), N_EXP - 1)
    tile_valid = (row0 < (offs + counts)[tile_expert]).astype(I32)
    in_region = row0 < region_end[-1]
    j = (row0 - offs[tile_expert]) // TM_E
    n = (region // TM_E)[tile_expert]
    streams = in_region & (tile_expert < N_EXP - 1)
    tile_tabs = (tile_expert, tile_valid, (in_region & (j == 0)).astype(I32),
                 jnp.where(streams, j * N_FF_CHUNKS // n, 0), jnp.where(streams, (j + 1) * N_FF_CHUNKS // n, 0))
    zero_start = offs + counts // CH * CH
    zero_tabs = (jnp.concatenate([zero_start, region_end[-1:]]),
                 jnp.concatenate([region_end - zero_start, MAX_TILES * TM_E - region_end[-1:]]) // CH)
    xs = _dispatch(tabs, zero_tabs, hs)
    ys = _experts(tile_tabs, xs, w_gu, w_down, slot)
    return _combine(tabs, y, mod, wf[0:2].T, ys, final_g)


def kernel(x_prompt, x_sample, state_rglru, cache_k, cache_v, c, c_ctx, norm_g, ada_w, ada_b,
           gm_w_in, gm_ln_g, gm_w_s, gm_b_s, gm_w_out, rg_w_in, rg_conv_w, rg_conv_b, rg_w_gate,
           rg_b_gate, rg_lambda, rg_w_out, att_w_qkv, att_lambda, att_subln_g, att_w_out,
           ff_w_gu, ff_w_down, moe_router, moe_w_gu, moe_w_down, final_g):
    y = (x_prompt.reshape(N_P, D), x_sample.reshape(N_L, D))
    conds = jnp.concatenate([c_ctx[None], c, jnp.zeros((N_COND - 1 - B_L, D), F32)], axis=0)
    mods = _adaln(conds, ada_w, ada_b)
    nb = D_RNN // RNN_BS
    gm_w_in, gm_w_out, rg_w_in, rg_w_out, att_w_qkv, att_w_out, ff_w_gu, ff_w_down = (
        w.astype(BF16) for w in (gm_w_in, gm_w_out, rg_w_in, rg_w_out, att_w_qkv, att_w_out, ff_w_gu, ff_w_down))
    new_rnn = new_k = new_v = None
    for i in range(DEPTH):
        mod = mods[i]
        kind, slot = i % 3, i // 3
        mixer_out = None
        if kind == 0:
            y = _gmlp_mixer(y if i == 0 else (y,), norm_g[i, 0], mod, gm_w_in, gm_ln_g[slot],
                            gm_w_s[slot].astype(BF16), gm_b_s[slot].T, gm_w_out, slot)
        elif kind == 1:
            yx = _mod_matmul(y, norm_g[i, 0], mod, rg_w_in, slot, tn=256, gelu_cols=D_RNN)
            wg_cat = rg_w_gate[slot].transpose(2, 3, 0, 1, 4).reshape(nb, RNN_BS, 4 * RNN_BS).astype(BF16)
            bg_cat = rg_b_gate[slot].reshape(2, 2, nb, RNN_BS).transpose(2, 0, 1, 3).reshape(nb, 1, 4 * RNN_BS)
            rg_args = (yx, rg_conv_w[slot], rg_conv_b[slot], wg_cat, bg_cat, rg_lambda[slot])
            a_p, fin = _rglru(*rg_args, jnp.zeros((2, B_P, D_RNN), F32),
                              row0=0, n_batch=B_P, n_seq=SUBLANE, t_len=T_P)
            a_l, _ = _rglru(*rg_args, state_rglru[:, slot].transpose(1, 0, 2),
                            row0=N_P, n_batch=B_L, n_seq=B_L, t_len=T_L)
            new_rnn = fin.transpose(1, 0, 2)[:, None]
            mixer_out = ((a_p, a_l), rg_w_out, slot)
        else:
            lambda_init = 0.8 - 0.6 * math.exp(-0.3 * i)
            att = (y, norm_g[i, 0], mod, att_w_qkv, slot, att_lambda[slot], att_subln_g[slot], lambda_init)
            a_p, new_k, new_v = _attention(*att, row0=0, n_batch=B_P, t_len=T_P, heads=HEADS, tq=T_P)
            a_l = _attention(*att, row0=N_P, n_batch=B_L, t_len=T_L, heads=HEADS, tq=256,
                             ctx=(cache_k, cache_v, slot), rope_tabs=_rope_tables(T_L))
            new_k = new_k.reshape(B_P, 1, T_P, HEADS, 2 * HD)
            new_v = new_v.reshape(B_P, 1, T_P, HEADS, VD)
            mixer_out = ((a_p, a_l), att_w_out, slot)
        fslot = i // 2
        if i % 2 == 0:
            y = _ffn(y, norm_g[i, 1], mod, ff_w_gu, ff_w_down, fslot, mixer_out)
        else:
            y = _moe(y, norm_g[i, 1], mod, moe_router[fslot], moe_w_gu, moe_w_down, fslot, mixer_out,
                     final_g if i == DEPTH - 1 else None)
    y_prompt, y_sample = y
    return (y_prompt.reshape(B_P, T_P, D), y_sample.reshape(B_L, T_L, D), new_rnn, new_k, new_v)
```
